```python
import math
import jax, jax.numpy as jnp
from jax import lax
import numpy as np

D_MODEL = 2048
BATCH = 4
SEQ = 4096
DEPTH = 1
DEC_BATCH = 4
DEC_SEQ = 8192
PAST_LEN = 128

RET_HEADS = 8
RET_QK_DIM = D_MODEL // RET_HEADS
RET_V_DIM = D_MODEL // RET_HEADS
RET_QK_W = RET_HEADS * RET_QK_DIM
RET_V_W = RET_HEADS * RET_V_DIM
RET_CHUNK = 128
ROPE_BASE = 10000.0
SGU_GROUPS = 8
SGU_WIDTH = D_MODEL
SGU_GROUP_DIM = SGU_WIDTH // SGU_GROUPS
SGU_CHUNK = 128
N_MEM = 256
XATTN_HEADS = 4
XATTN_HEAD_DIM = D_MODEL // XATTN_HEADS
D_FF = 256 * ((8 * D_MODEL // 3 + 255) // 256)
FFN_RES_SCALE = 0.5
IN_SIZES = (RET_QK_W, RET_QK_W, RET_V_W, RET_V_W, SGU_WIDTH, SGU_WIDTH, D_MODEL, D_MODEL)
IN_COLS = 2 * RET_QK_W + 2 * RET_V_W + 2 * SGU_WIDTH + 2 * D_MODEL
EPS = 1e-6

kernel_name = "hybrid_retention_sgu_encoder"


def rms_norm(x, w):
    xf = x.astype(jnp.float32)
    y = xf * lax.rsqrt(jnp.mean(xf * xf, axis=-1, keepdims=True) + EPS)
    return (y * w.astype(jnp.float32)).astype(x.dtype)


def layer_norm_gain(x, w):
    xf = x.astype(jnp.float32)
    mu = jnp.mean(xf, axis=-1, keepdims=True)
    var = jnp.mean(jnp.square(xf - mu), axis=-1, keepdims=True)
    return ((xf - mu) * lax.rsqrt(var + EPS) * w.astype(jnp.float32)).astype(x.dtype)


def swiglu_half_step(h, norm_w, w_gu, w_down):
    n = rms_norm(h, norm_w)
    g, u = jnp.split(n @ w_gu, 2, axis=-1)
    return h + FFN_RES_SCALE * ((jax.nn.silu(g) * u) @ w_down)


def rotary(x, pos):
    half = x.shape[-1] // 2
    freqs = ROPE_BASE ** (-jnp.linspace(0.0, 1.0, half, dtype=jnp.float32))
    ang = pos[:, None] * freqs[None, :]
    cos = jnp.cos(ang).astype(x.dtype)
    sin = jnp.sin(ang).astype(x.dtype)
    x1, x2 = x[..., :half], x[..., half:]
    return jnp.concatenate([x1 * cos - x2 * sin, x1 * sin + x2 * cos], axis=-1)


def retention_scan(q, k, v, log_gamma, strict):
    B, H, S, dk = q.shape
    dv = v.shape[-1]
    C = RET_CHUNK
    nC = S // C
    lg = log_gamma.astype(jnp.float32)
    idx = jnp.arange(C, dtype=jnp.float32)
    diff = idx[:, None] - idx[None, :]
    mask = (diff > 0) if strict else (diff >= 0)
    decay_in = jnp.where(mask[None], jnp.exp(lg[:, None, None] * jnp.maximum(diff, 0.0)[None]), 0.0).astype(q.dtype)
    q_dec = jnp.exp(lg[:, None] * (idx[None, :] + 1.0))[:, :, None].astype(q.dtype)
    k_dec = jnp.exp(lg[:, None] * (C - 1.0 - idx[None, :]))[:, :, None].astype(q.dtype)
    chunk_dec = jnp.exp(lg * C)[:, None, None].astype(q.dtype)

    def to_chunks(t):
        return jnp.moveaxis(t.reshape(B, H, nC, C, t.shape[-1]), 2, 0)

    def step(state, inp):
        qc, kc, vc = inp
        s = jnp.einsum('bhid,bhjd->bhij', qc, kc) * decay_in
        inner = jnp.einsum('bhij,bhjv->bhiv', s, vc)
        cross = jnp.einsum('bhid,bhdv->bhiv', qc * q_dec, state)
        state = state * chunk_dec + jnp.einsum('bhjd,bhjv->bhdv', kc * k_dec, vc)
        return state, inner + cross

    state0 = jnp.zeros((B, H, dk, dv), dtype=q.dtype)
    _, out = lax.scan(step, state0, (to_chunks(q), to_chunks(k), to_chunks(v)))
    return jnp.moveaxis(out, 0, 2).reshape(B, H, S, dv)


def retention_branch(q, k, v, g, decay_fwd, decay_bwd, gn_w):
    B, S, _ = q.shape
    def heads(t, d):
        return t.reshape(B, S, RET_HEADS, d).transpose(0, 2, 1, 3)
    pos = jnp.arange(S, dtype=jnp.float32)
    qh = rotary(heads(q, RET_QK_DIM), pos)
    kh = rotary(heads(k, RET_QK_DIM), pos) * (RET_QK_DIM ** -0.5)
    vh = heads(v, RET_V_DIM)
    lg_f = -jnp.exp(decay_fwd.astype(jnp.float32))
    lg_b = -jnp.exp(decay_bwd.astype(jnp.float32))
    fwd = retention_scan(qh, kh, vh, lg_f, False)
    bwd = jnp.flip(retention_scan(jnp.flip(qh, 2), jnp.flip(kh, 2), jnp.flip(vh, 2), lg_b, True), 2)
    o = (fwd + bwd).astype(jnp.float32)
    mu = jnp.mean(o, axis=-1, keepdims=True)
    var = jnp.mean(jnp.square(o - mu), axis=-1, keepdims=True)
    o = (o - mu) * lax.rsqrt(var + EPS)
    o = o.transpose(0, 2, 1, 3).reshape(B, S, RET_V_W) * gn_w.astype(jnp.float32)
    return jax.nn.silu(g) * o.astype(g.dtype)


def sgu_branch(u, vs, norm_w, w_s, b_s):
    B, S, _ = u.shape
    nC = S // SGU_CHUNK
    u = jax.nn.gelu(u)
    vs = layer_norm_gain(jax.nn.gelu(vs), norm_w)
    vc = vs.reshape(B, nC, SGU_CHUNK, SGU_GROUPS, SGU_GROUP_DIM)
    mixed = jnp.einsum('gij,bcjgd->bcigd', w_s, vc) + b_s.T[None, None, :, :, None]
    return u * mixed.reshape(B, S, SGU_WIDTH)


def memory_cross_attention(h, mem, norm_q, norm_mem, w_q, w_kv, w_o):
    B, S, _ = h.shape
    M = mem.shape[1]
    n = rms_norm(h, norm_q)
    m = rms_norm(mem, norm_mem)
    q = (n @ w_q).reshape(B, S, XATTN_HEADS, XATTN_HEAD_DIM)
    k, v = jnp.split(m @ w_kv, 2, axis=-1)
    k = k.reshape(B, M, XATTN_HEADS, XATTN_HEAD_DIM)
    v = v.reshape(B, M, XATTN_HEADS, XATTN_HEAD_DIM)
    s = jnp.einsum('bqhd,bkhd->bhqk', q, k).astype(jnp.float32) * (XATTN_HEAD_DIM ** -0.5)
    p = jax.nn.softmax(s, axis=-1).astype(v.dtype)
    o = jnp.einsum('bhqk,bkhd->bqhd', p, v).reshape(B, S, D_MODEL)
    return h + o @ w_o


def token_mixing(h, norm_w, w_in, gate_bias, decay_fwd, decay_bwd, ret_gn_w, w_ret_out,
                 sgu_norm_w, sgu_w_s, sgu_b_s, w_sgu_out, w_out):
    n = rms_norm(h, norm_w)
    z = n @ w_in
    cuts = [sum(IN_SIZES[:i + 1]) for i in range(len(IN_SIZES) - 1)]
    q, k, v, g, u, vs, gate_r, gate_s = jnp.split(z, cuts, axis=-1)
    ret = retention_branch(q, k, v, g, decay_fwd, decay_bwd, ret_gn_w) @ w_ret_out
    sgu = sgu_branch(u, vs, sgu_norm_w, sgu_w_s, sgu_b_s) @ w_sgu_out
    merged = jax.nn.sigmoid(gate_r + gate_bias[0]) * ret + jax.nn.sigmoid(gate_s + gate_bias[1]) * sgu
    return h + merged @ w_out


def encoder_trunk(x, mem, ffn1_norm, ffn1_w_gu, ffn1_w_down, mix_norm, w_in, gate_bias,
                  ret_decay_fwd, ret_decay_bwd, ret_gn_w, w_ret_out, sgu_norm_w, sgu_w_s, sgu_b_s,
                  w_sgu_out, w_out, xattn_norm_q, xattn_norm_mem, xattn_w_q, xattn_w_kv, xattn_w_o,
                  ffn2_norm, ffn2_w_gu, ffn2_w_down, final_norm):
    h = x
    for l in range(DEPTH):
        h = swiglu_half_step(h, ffn1_norm[l], ffn1_w_gu[l], ffn1_w_down[l])
        h = token_mixing(h, mix_norm[l], w_in[l], gate_bias[l], ret_decay_fwd[l], ret_decay_bwd[l],
                         ret_gn_w[l], w_ret_out[l], sgu_norm_w[l], sgu_w_s[l], sgu_b_s[l],
                         w_sgu_out[l], w_out[l])
        h = memory_cross_attention(h, mem, xattn_norm_q[l], xattn_norm_mem[l], xattn_w_q[l],
                                   xattn_w_kv[l], xattn_w_o[l])
        h = swiglu_half_step(h, ffn2_norm[l], ffn2_w_gu[l], ffn2_w_down[l])
    return rms_norm(h, final_norm)


def setup_inputs(seed: int = 0) -> dict:
    key = jax.random.key(seed)
    ks = iter(jax.random.split(key, 40))
    f32 = jnp.float32

    def nrm(shape, scale):
        return jax.random.normal(next(ks), shape, f32) * scale

    def gain(shape):
        return 1.0 + 0.01 * jax.random.normal(next(ks), shape, f32)

    L = DEPTH
    head_scales = -(5.0 + jnp.arange(RET_HEADS, dtype=f32)) * math.log(2.0)
    return {
        "x_prompt": nrm((BATCH, SEQ, D_MODEL), 1.0),
        "x_sample": nrm((DEC_BATCH, DEC_SEQ, D_MODEL), 1.0),
        "mem_prompt": nrm((BATCH, N_MEM, D_MODEL), 1.0),
        "mem_sample": nrm((DEC_BATCH, N_MEM, D_MODEL), 1.0),
        "ffn1_norm": gain((L, D_MODEL)),
        "ffn1_w_gu": nrm((L, D_MODEL, 2 * D_FF), D_MODEL ** -0.5),
        "ffn1_w_down": nrm((L, D_FF, D_MODEL), D_FF ** -0.5),
        "mix_norm": gain((L, D_MODEL)),
        "w_in": nrm((L, D_MODEL, IN_COLS), D_MODEL ** -0.5),
        "gate_bias": nrm((L, 2, D_MODEL), 0.02),
        "ret_decay_fwd": head_scales[None, :] + nrm((L, RET_HEADS), 0.1),
        "ret_decay_bwd": head_scales[None, :] + nrm((L, RET_HEADS), 0.1),
        "ret_gn_w": gain((L, RET_V_W)),
        "w_ret_out": nrm((L, RET_V_W, D_MODEL), RET_V_W ** -0.5),
        "sgu_norm_w": gain((L, SGU_WIDTH)),
        "sgu_w_s": nrm((L, SGU_GROUPS, SGU_CHUNK, SGU_CHUNK), SGU_CHUNK ** -0.5),
        "sgu_b_s": 1.0 + nrm((L, SGU_GROUPS, SGU_CHUNK), 0.1),
        "w_sgu_out": nrm((L, SGU_WIDTH, D_MODEL), SGU_WIDTH ** -0.5),
        "w_out": nrm((L, D_MODEL, D_MODEL), D_MODEL ** -0.5),
        "xattn_norm_q": gain((L, D_MODEL)),
        "xattn_norm_mem": gain((L, D_MODEL)),
        "xattn_w_q": nrm((L, D_MODEL, D_MODEL), D_MODEL ** -0.5),
        "xattn_w_kv": nrm((L, D_MODEL, 2 * D_MODEL), D_MODEL ** -0.5),
        "xattn_w_o": nrm((L, D_MODEL, D_MODEL), D_MODEL ** -0.5),
        "ffn2_norm": gain((L, D_MODEL)),
        "ffn2_w_gu": nrm((L, D_MODEL, 2 * D_FF), D_MODEL ** -0.5),
        "ffn2_w_down": nrm((L, D_FF, D_MODEL), D_FF ** -0.5),
        "final_norm": gain((D_MODEL,)),
    }


def reference(x_prompt, x_sample, mem_prompt, mem_sample, ffn1_norm, ffn1_w_gu, ffn1_w_down, mix_norm,
              w_in, gate_bias, ret_decay_fwd, ret_decay_bwd, ret_gn_w, w_ret_out, sgu_norm_w, sgu_w_s,
              sgu_b_s, w_sgu_out, w_out, xattn_norm_q, xattn_norm_mem, xattn_w_q, xattn_w_kv, xattn_w_o,
              ffn2_norm, ffn2_w_gu, ffn2_w_down, final_norm):
    y_prompt = encoder_trunk(x_prompt, mem_prompt, ffn1_norm, ffn1_w_gu, ffn1_w_down, mix_norm, w_in,
                             gate_bias, ret_decay_fwd, ret_decay_bwd, ret_gn_w, w_ret_out, sgu_norm_w,
                             sgu_w_s, sgu_b_s, w_sgu_out, w_out, xattn_norm_q, xattn_norm_mem, xattn_w_q,
                             xattn_w_kv, xattn_w_o, ffn2_norm, ffn2_w_gu, ffn2_w_down, final_norm)
    y_sample = encoder_trunk(x_sample, mem_sample, ffn1_norm, ffn1_w_gu, ffn1_w_down, mix_norm, w_in,
                             gate_bias, ret_decay_fwd, ret_decay_bwd, ret_gn_w, w_ret_out, sgu_norm_w,
                             sgu_w_s, sgu_b_s, w_sgu_out, w_out, xattn_norm_q, xattn_norm_mem, xattn_w_q,
                             xattn_w_kv, xattn_w_o, ffn2_norm, ffn2_w_gu, ffn2_w_down, final_norm)
    return (y_prompt, y_sample)
```

```python
import functools
import math

import jax
import jax.numpy as jnp
from jax import lax
from jax.experimental import pallas as pl
from jax.experimental.pallas import tpu as pltpu

F32 = jnp.float32
BF16 = jnp.bfloat16

EPS = 1e-6
FFN_RES_SCALE = 0.5
ROPE_BASE = 10000.0
RET_HEADS = 8
SGU_GROUPS = 8
SGU_CHUNK = 128
XATTN_HEADS = 4
N_IN_SEGMENTS = 8
SEG_Q, SEG_K, SEG_V, SEG_G, SEG_U, SEG_VS, SEG_GATE_R, SEG_GATE_S = range(N_IN_SEGMENTS)

V7X_VMEM_BYTES = 64 * 1024 * 1024
VMEM_LIMIT_BYTES = V7X_VMEM_BYTES - 8 * 1024 * 1024

RET_BLOCK = 256

FFN_ROWS = 512
FFN_COLS = 512
IN_ROWS = 512
SGU_ROWS = 512
OUT_ROWS = 256
KV_ROWS = 512
KV_COLS = 1024
XATTN_ROWS = 512


def _params(*semantics):
    return pltpu.CompilerParams(dimension_semantics=semantics, vmem_limit_bytes=VMEM_LIMIT_BYTES)


def _resident(shape):
    return pl.BlockSpec(shape, lambda *_: (0,) * len(shape), pipeline_mode=pl.Buffered(1))


def _rms(x, w):
    return x * lax.rsqrt(jnp.mean(x * x, axis=-1, keepdims=True) + EPS) * w


def _dot(a, b):
    return jnp.dot(a, b, preferred_element_type=F32)


def _dot_nt(a, b):
    return lax.dot_general(a, b, (((1,), (1,)), ((), ())), preferred_element_type=F32)


def _dot_tn(a, b):
    return lax.dot_general(a, b, (((0,), (0,)), ((), ())), preferred_element_type=F32)


def _gelu_tanh(x):
    return x * (0.5 * (1.0 + jnp.tanh(math.sqrt(2.0 / math.pi) * (x + 0.044715 * (x * x * x)))))


def _ffn_kernel(x_ref, nw_ref, wg_ref, wu_ref, wd_ref, *rest, n_col_steps, final_norm):
    if final_norm:
        fw_ref, o_ref, n_scr = rest
    else:
        o_ref, n_scr = rest
    j = pl.program_id(1)

    @pl.when(j == 0)
    def _():
        n_scr[...] = _rms(x_ref[...], nw_ref[...]).astype(BF16)
        o_ref[...] = jnp.zeros_like(o_ref)

    n = n_scr[...]
    g = _dot(n, wg_ref[...])
    u = _dot(n, wu_ref[...])
    a = (g * jax.nn.sigmoid(g) * u).astype(BF16)
    o_ref[...] += _dot(a, wd_ref[...])

    @pl.when(j == n_col_steps - 1)
    def _():
        h = x_ref[...] + FFN_RES_SCALE * o_ref[...]
        o_ref[...] = _rms(h, fw_ref[...]) if final_norm else h


def _ffn(x, norm_w, w_gu, w_down, final_w=None):
    t, d = x.shape
    d_ff = w_down.shape[0]
    n_col_steps = d_ff // FFN_COLS
    assert t % FFN_ROWS == 0 and d_ff % FFN_COLS == 0
    final_norm = final_w is not None
    in_specs = [
        pl.BlockSpec((FFN_ROWS, d), lambda i, j: (i, 0)),
        pl.BlockSpec((1, d), lambda i, j: (0, 0)),
        pl.BlockSpec((d, FFN_COLS), lambda i, j: (0, j)),
        pl.BlockSpec((d, FFN_COLS), lambda i, j: (0, j + n_col_steps)),
        pl.BlockSpec((FFN_COLS, d), lambda i, j: (j, 0)),
    ]
    args = [x, norm_w, w_gu, w_gu, w_down]
    if final_norm:
        in_specs.append(pl.BlockSpec((1, d), lambda i, j: (0, 0)))
        args.append(final_w)
    return pl.pallas_call(
        functools.partial(_ffn_kernel, n_col_steps=n_col_steps, final_norm=final_norm),
        grid=(t // FFN_ROWS, n_col_steps),
        in_specs=in_specs,
        out_specs=pl.BlockSpec((FFN_ROWS, d), lambda i, j: (i, 0)),
        out_shape=jax.ShapeDtypeStruct((t, d), F32),
        scratch_shapes=[pltpu.VMEM((FFN_ROWS, d), BF16)],
        compiler_params=_params("parallel", "arbitrary"),
        name="ffn_final" if final_norm else "ffn",
    )(*args)


def _in_proj_kernel(x_ref, nw_ref, w_ref, cos_ref, sin_ref, gb_ref, lnw_ref, o_ref, n_scr, z_scr, *, k_scale):
    j = pl.program_id(1)
    width = z_scr.shape[1]
    head = width // RET_HEADS
    half = head // 2

    @pl.when(j == 0)
    def _():
        n_scr[...] = _rms(x_ref[...], nw_ref[...]).astype(BF16)

    z_scr[...] = _dot(n_scr[...], w_ref[...])

    def rotary(scale):
        cos = cos_ref[...]
        sin = sin_ref[...]
        for h in range(RET_HEADS):
            lo = h * head
            x1 = z_scr[:, lo:lo + half]
            x2 = z_scr[:, lo + half:lo + head]
            r1 = x1 * cos - x2 * sin
            r2 = x1 * sin + x2 * cos
            if scale is not None:
                r1 = r1 * scale
                r2 = r2 * scale
            o_ref[0, :, lo:lo + half] = r1.astype(BF16)
            o_ref[0, :, lo + half:lo + head] = r2.astype(BF16)

    @pl.when(j == SEG_Q)
    def _():
        rotary(None)

    @pl.when(j == SEG_K)
    def _():
        rotary(k_scale)

    @pl.when(j == SEG_V)
    def _():
        o_ref[0] = z_scr[...].astype(BF16)

    @pl.when(j == SEG_G)
    def _():
        z = z_scr[...]
        o_ref[0] = (z * jax.nn.sigmoid(z)).astype(BF16)

    @pl.when(j == SEG_U)
    def _():
        o_ref[0] = _gelu_tanh(z_scr[...]).astype(BF16)

    @pl.when(j == SEG_VS)
    def _():
        a = _gelu_tanh(z_scr[...])
        c = a - jnp.mean(a, axis=-1, keepdims=True)
        var = jnp.mean(c * c, axis=-1, keepdims=True)
        o_ref[0] = (c * lax.rsqrt(var + EPS) * lnw_ref[...]).astype(BF16)

    @pl.when(j >= SEG_GATE_R)
    def _():
        bias = gb_ref[pl.ds(j - SEG_GATE_R, 1), :]
        o_ref[0] = jax.nn.sigmoid(z_scr[...] + bias).astype(BF16)


def _in_proj(h, norm_w, w_in, cos, sin, gate_bias, sgu_norm_w):
    t, d = h.shape
    seq = cos.shape[0]
    assert t % IN_ROWS == 0 and seq % IN_ROWS == 0 and w_in.shape[1] == N_IN_SEGMENTS * d
    pos_blocks = seq // IN_ROWS
    half = cos.shape[1]
    return pl.pallas_call(
        functools.partial(_in_proj_kernel, k_scale=float(2 * half) ** -0.5),
        grid=(t // IN_ROWS, N_IN_SEGMENTS),
        in_specs=[
            pl.BlockSpec((IN_ROWS, d), lambda i, j: (i, 0)),
            pl.BlockSpec((1, d), lambda i, j: (0, 0)),
            pl.BlockSpec((d, d), lambda i, j: (0, j)),
            pl.BlockSpec((IN_ROWS, half), lambda i, j: (i % pos_blocks, 0)),
            pl.BlockSpec((IN_ROWS, half), lambda i, j: (i % pos_blocks, 0)),
            pl.BlockSpec((2, d), lambda i, j: (0, 0)),
            pl.BlockSpec((1, d), lambda i, j: (0, 0)),
        ],
        out_specs=pl.BlockSpec((1, IN_ROWS, d), lambda i, j: (j, i, 0)),
        out_shape=jax.ShapeDtypeStruct((N_IN_SEGMENTS, t, d), BF16),
        scratch_shapes=[pltpu.VMEM((IN_ROWS, d), BF16), pltpu.VMEM((IN_ROWS, d), F32)],
        compiler_params=_params("parallel", "arbitrary"),
        name="in_proj",
    )(h, norm_w, w_in, cos, sin, gate_bias, sgu_norm_w)


def _decay_rate(dec_ref, h, shape):
    return -jnp.exp(jnp.full(shape, dec_ref[h], F32))


def _ret_bwd_state_kernel(decb_ref, k_ref, v_ref, sb_ref, state, kdec, cdec):
    c = pl.program_id(1)
    blk = k_ref.shape[1]
    head = k_ref.shape[2] // RET_HEADS

    @pl.when((pl.program_id(0) == 0) & (c == 0))
    def _():
        row = lax.broadcasted_iota(jnp.int32, (blk, head), 0).astype(F32)
        for h in range(RET_HEADS):
            lg = _decay_rate(decb_ref, h, (blk, head))
            kdec[h] = jnp.exp(lg * row)
            cdec[h] = jnp.exp(_decay_rate(decb_ref, h, (1, head)) * float(blk))

    @pl.when(c == 0)
    def _():
        state[...] = jnp.zeros_like(state)

    for h in range(RET_HEADS):
        sl = slice(h * head, (h + 1) * head)
        st = state[h]
        sb_ref[0, 0, h] = st.astype(BF16)
        kd = (k_ref[0, :, sl].astype(F32) * kdec[h]).astype(BF16)
        state[h] = st * cdec[h] + _dot_tn(kd, v_ref[0, :, sl])


def _ret_bwd_states(z, dec_b, batch):
    _, t, d = z.shape
    n_blk = t // batch // RET_BLOCK
    head = d // RET_HEADS

    def kv_map(seg):
        return lambda b, c: (seg, b * n_blk + (n_blk - 1 - c), 0)

    return pl.pallas_call(
        _ret_bwd_state_kernel,
        grid=(batch, n_blk),
        in_specs=[
            pl.BlockSpec(memory_space=pltpu.SMEM),
            pl.BlockSpec((1, RET_BLOCK, d), kv_map(SEG_K)),
            pl.BlockSpec((1, RET_BLOCK, d), kv_map(SEG_V)),
        ],
        out_specs=pl.BlockSpec((1, 1, RET_HEADS, head, head), lambda b, c: (b, n_blk - 1 - c, 0, 0, 0)),
        out_shape=jax.ShapeDtypeStruct((batch, n_blk, RET_HEADS, head, head), BF16),
        scratch_shapes=[
            pltpu.VMEM((RET_HEADS, head, head), F32),
            pltpu.VMEM((RET_HEADS, RET_BLOCK, head), F32),
            pltpu.VMEM((RET_HEADS, 1, head), F32),
        ],
        compiler_params=_params("arbitrary", "arbitrary"),
        name="ret_bwd_states",
    )(dec_b, z, z)


def _ret_out_kernel(decf_ref, decb_ref, q_ref, k_ref, v_ref, g_ref, sb_ref, gnw_ref, o_ref,
                    state, dmat, qdf, qdb, kdf, cdf):
    c = pl.program_id(1)
    blk = q_ref.shape[1]
    head = q_ref.shape[2] // RET_HEADS

    @pl.when((pl.program_id(0) == 0) & (c == 0))
    def _():
        row = lax.broadcasted_iota(jnp.int32, (blk, head), 0).astype(F32)
        ri = lax.broadcasted_iota(jnp.int32, (blk, blk), 0)
        ci = lax.broadcasted_iota(jnp.int32, (blk, blk), 1)
        dist = (ri - ci).astype(F32)
        for h in range(RET_HEADS):
            lf = _decay_rate(decf_ref, h, (blk, head))
            lb = _decay_rate(decb_ref, h, (blk, head))
            qdf[h] = jnp.exp(lf * (row + 1.0))
            qdb[h] = jnp.exp(lb * (float(blk) - row))
            kdf[h] = jnp.exp(lf * (float(blk) - 1.0 - row))
            cdf[h] = jnp.exp(_decay_rate(decf_ref, h, (1, head)) * float(blk))
            lf2 = _decay_rate(decf_ref, h, (blk, blk))
            lb2 = _decay_rate(decb_ref, h, (blk, blk))
            dmat[h] = jnp.where(ri >= ci, jnp.exp(lf2 * jnp.maximum(dist, 0.0)), jnp.exp(lb2 * jnp.maximum(-dist, 0.0)))

    @pl.when(c == 0)
    def _():
        state[...] = jnp.zeros_like(state)

    for h in range(RET_HEADS):
        sl = slice(h * head, (h + 1) * head)
        q = q_ref[0, :, sl]
        k = k_ref[0, :, sl]
        v = v_ref[0, :, sl]
        st = state[h]
        p = (_dot_nt(q, k) * dmat[h]).astype(BF16)
        qf = q.astype(F32)
        o = _dot(p, v)
        o += _dot((qf * qdf[h]).astype(BF16), st.astype(BF16))
        o += _dot((qf * qdb[h]).astype(BF16), sb_ref[0, 0, h])
        cen = o - jnp.mean(o, axis=-1, keepdims=True)
        var = jnp.mean(cen * cen, axis=-1, keepdims=True)
        on = cen * lax.rsqrt(var + EPS) * gnw_ref[:, sl]
        o_ref[:, sl] = (g_ref[0, :, sl].astype(F32) * on).astype(BF16)
        kd = (k.astype(F32) * kdf[h]).astype(BF16)
        state[h] = st * cdf[h] + _dot_tn(kd, v)


def _ret_out(z, sb, dec_f, dec_b, gn_w, batch):
    _, t, d = z.shape
    n_blk = t // batch // RET_BLOCK
    head = d // RET_HEADS

    def seg_map(seg):
        return lambda b, c: (seg, b * n_blk + c, 0)

    seg_spec = lambda seg: pl.BlockSpec((1, RET_BLOCK, d), seg_map(seg))
    return pl.pallas_call(
        _ret_out_kernel,
        grid=(batch, n_blk),
        in_specs=[
            pl.BlockSpec(memory_space=pltpu.SMEM),
            pl.BlockSpec(memory_space=pltpu.SMEM),
            seg_spec(SEG_Q), seg_spec(SEG_K), seg_spec(SEG_V), seg_spec(SEG_G),
            pl.BlockSpec((1, 1, RET_HEADS, head, head), lambda b, c: (b, c, 0, 0, 0)),
            pl.BlockSpec((1, d), lambda b, c: (0, 0)),
        ],
        out_specs=pl.BlockSpec((RET_BLOCK, d), lambda b, c: (b * n_blk + c, 0)),
        out_shape=jax.ShapeDtypeStruct((t, d), BF16),
        scratch_shapes=[
            pltpu.VMEM((RET_HEADS, head, head), F32),
            pltpu.VMEM((RET_HEADS, RET_BLOCK, RET_BLOCK), F32),
            pltpu.VMEM((RET_HEADS, RET_BLOCK, head), F32),
            pltpu.VMEM((RET_HEADS, RET_BLOCK, head), F32),
            pltpu.VMEM((RET_HEADS, RET_BLOCK, head), F32),
            pltpu.VMEM((RET_HEADS, 1, head), F32),
        ],
        compiler_params=_params("arbitrary", "arbitrary"),
        name="ret_out",
    )(dec_f, dec_b, z, z, z, z, sb, gn_w)


def _sgu_kernel(u_ref, vs_ref, ws_ref, bs_ref, o_ref):
    rows = u_ref.shape[1]
    gdim = u_ref.shape[2] // SGU_GROUPS
    for c in range(rows // SGU_CHUNK):
        rs = slice(c * SGU_CHUNK, (c + 1) * SGU_CHUNK)
        for g in range(SGU_GROUPS):
            cs = slice(g * gdim, (g + 1) * gdim)
            mixed = _dot(ws_ref[g], vs_ref[0, rs, cs]) + bs_ref[g]
            o_ref[rs, cs] = (u_ref[0, rs, cs].astype(F32) * mixed).astype(BF16)


def _sgu(z, w_s, b_s):
    _, t, d = z.shape
    assert t % SGU_ROWS == 0 and SGU_ROWS % SGU_CHUNK == 0
    return pl.pallas_call(
        _sgu_kernel,
        grid=(t // SGU_ROWS,),
        in_specs=[
            pl.BlockSpec((1, SGU_ROWS, d), lambda i: (SEG_U, i, 0)),
            pl.BlockSpec((1, SGU_ROWS, d), lambda i: (SEG_VS, i, 0)),
            pl.BlockSpec((SGU_GROUPS, SGU_CHUNK, SGU_CHUNK), lambda i: (0, 0, 0)),
            pl.BlockSpec((SGU_GROUPS, SGU_CHUNK, 1), lambda i: (0, 0, 0)),
        ],
        out_specs=pl.BlockSpec((SGU_ROWS, d), lambda i: (i, 0)),
        out_shape=jax.ShapeDtypeStruct((t, d), BF16),
        compiler_params=_params("parallel"),
        name="sgu",
    )(z, z, w_s, b_s)


def _out_proj_kernel(ar_ref, as_ref, gr_ref, gs_ref, h_ref, wr_ref, ws_ref, wo_ref, o_ref):
    ret = _dot(ar_ref[...], wr_ref[...])
    sgu = _dot(as_ref[...], ws_ref[...])
    merged = gr_ref[0].astype(F32) * ret + gs_ref[0].astype(F32) * sgu
    o_ref[...] = h_ref[...] + _dot(merged.astype(BF16), wo_ref[...])


def _out_proj(ret_act, sgu_act, z, h, w_ret_out, w_sgu_out, w_out):
    t, d = h.shape
    assert t % OUT_ROWS == 0
    rows = lambda: pl.BlockSpec((OUT_ROWS, d), lambda i: (i, 0))
    seg = lambda s: pl.BlockSpec((1, OUT_ROWS, d), lambda i: (s, i, 0))
    return pl.pallas_call(
        _out_proj_kernel,
        grid=(t // OUT_ROWS,),
        in_specs=[rows(), rows(), seg(SEG_GATE_R), seg(SEG_GATE_S), rows(),
                  _resident((d, d)), _resident((d, d)), _resident((d, d))],
        out_specs=rows(),
        out_shape=jax.ShapeDtypeStruct((t, d), F32),
        compiler_params=_params("parallel"),
        name="out_proj",
    )(ret_act, sgu_act, z, z, h, w_ret_out, w_sgu_out, w_out)


def _mem_kv_kernel(m_ref, nw_ref, w_ref, o_ref, n_scr):
    @pl.when(pl.program_id(1) == 0)
    def _():
        n_scr[...] = _rms(m_ref[...], nw_ref[...]).astype(BF16)

    o_ref[...] = _dot(n_scr[...], w_ref[...]).astype(BF16)


def _mem_kv(mem, norm_w, w_kv):
    t, d = mem.shape
    rows = min(KV_ROWS, t)
    n_out = w_kv.shape[1]
    assert t % rows == 0 and n_out % KV_COLS == 0
    return pl.pallas_call(
        _mem_kv_kernel,
        grid=(t // rows, n_out // KV_COLS),
        in_specs=[
            pl.BlockSpec((rows, d), lambda i, j: (i, 0)),
            pl.BlockSpec((1, d), lambda i, j: (0, 0)),
            pl.BlockSpec((d, KV_COLS), lambda i, j: (0, j)),
        ],
        out_specs=pl.BlockSpec((rows, KV_COLS), lambda i, j: (i, j)),
        out_shape=jax.ShapeDtypeStruct((t, n_out), BF16),
        scratch_shapes=[pltpu.VMEM((rows, d), BF16)],
        compiler_params=_params("parallel", "arbitrary"),
        name="mem_kv",
    )(mem, norm_w, w_kv)


def _xattn_kernel(h_ref, nw_ref, kv_ref, wq_ref, wo_ref, o_ref, q_scr, a_scr):
    d = h_ref.shape[1]
    hd = d // XATTN_HEADS
    scale = float(hd) ** -0.5
    x = h_ref[...]
    q_scr[...] = _dot(_rms(x, nw_ref[...]).astype(BF16), wq_ref[...]).astype(BF16)
    for h in range(XATTN_HEADS):
        sl = slice(h * hd, (h + 1) * hd)
        s = _dot_nt(q_scr[:, sl], kv_ref[:, sl]) * scale
        e = jnp.exp(s - jnp.max(s, axis=-1, keepdims=True))
        p = e / jnp.sum(e, axis=-1, keepdims=True)
        a_scr[:, sl] = _dot(p.astype(BF16), kv_ref[:, d + h * hd:d + (h + 1) * hd]).astype(BF16)
    o_ref[...] = x + _dot(a_scr[...], wo_ref[...])


def _xattn(h, kv, norm_w, w_q, w_o, batch):
    t, d = h.shape
    n_mem = kv.shape[0] // batch
    row_blocks = t // batch // XATTN_ROWS
    assert t % (batch * XATTN_ROWS) == 0
    return pl.pallas_call(
        _xattn_kernel,
        grid=(batch, row_blocks),
        in_specs=[
            pl.BlockSpec((XATTN_ROWS, d), lambda b, i: (b * row_blocks + i, 0)),
            pl.BlockSpec((1, d), lambda b, i: (0, 0)),
            pl.BlockSpec((n_mem, 2 * d), lambda b, i: (b, 0)),
            _resident((d, d)), _resident((d, d)),
        ],
        out_specs=pl.BlockSpec((XATTN_ROWS, d), lambda b, i: (b * row_blocks + i, 0)),
        out_shape=jax.ShapeDtypeStruct((t, d), F32),
        scratch_shapes=[pltpu.VMEM((XATTN_ROWS, d), BF16), pltpu.VMEM((XATTN_ROWS, d), BF16)],
        compiler_params=_params("parallel", "parallel"),
        name="xattn",
    )(h, norm_w, kv, w_q, w_o)


def _rope_tables(seq, half):
    freqs = ROPE_BASE ** (-jnp.linspace(0.0, 1.0, half, dtype=F32))
    ang = jnp.arange(seq, dtype=F32)[:, None] * freqs[None, :]
    return jnp.cos(ang), jnp.sin(ang)


def _trunk(x, mem, w):
    batch, seq, d = x.shape
    t = batch * seq
    cos, sin = _rope_tables(seq, d // RET_HEADS // 2)
    h = _ffn(x.reshape(t, d), w["ffn1_norm"], w["ffn1_w_gu"], w["ffn1_w_down"])
    z = _in_proj(h, w["mix_norm"], w["w_in"], cos, sin, w["gate_bias"], w["sgu_norm_w"])
    sb = _ret_bwd_states(z, w["ret_decay_bwd"], batch)
    ret_act = _ret_out(z, sb, w["ret_decay_fwd"], w["ret_decay_bwd"], w["ret_gn_w"], batch)
    sgu_act = _sgu(z, w["sgu_w_s"], w["sgu_b_s"])
    h = _out_proj(ret_act, sgu_act, z, h, w["w_ret_out"], w["w_sgu_out"], w["w_out"])
    kv = _mem_kv(mem.reshape(-1, d), w["xattn_norm_mem"], w["xattn_w_kv"])
    h = _xattn(h, kv, w["xattn_norm_q"], w["xattn_w_q"], w["xattn_w_o"], batch)
    y = _ffn(h, w["ffn2_norm"], w["ffn2_w_gu"], w["ffn2_w_down"], final_w=w["final_norm"])
    return y.reshape(batch, seq, d)


def _prepare(ffn1_norm, ffn1_w_gu, ffn1_w_down, mix_norm, w_in, gate_bias, ret_decay_fwd, ret_decay_bwd, ret_gn_w, w_ret_out, sgu_norm_w, sgu_w_s, sgu_b_s, w_sgu_out, w_out, xattn_norm_q, xattn_norm_mem, xattn_w_q, xattn_w_kv, xattn_w_o, ffn2_norm, ffn2_w_gu, ffn2_w_down, final_norm):
    assert ffn1_norm.shape[0] == 1, "single-layer trunk"
    return {
        "ffn1_norm": ffn1_norm, "ffn1_w_gu": ffn1_w_gu[0].astype(BF16), "ffn1_w_down": ffn1_w_down[0].astype(BF16),
        "mix_norm": mix_norm, "w_in": w_in[0].astype(BF16), "gate_bias": gate_bias[0],
        "ret_decay_fwd": ret_decay_fwd[0], "ret_decay_bwd": ret_decay_bwd[0], "ret_gn_w": ret_gn_w,
        "w_ret_out": w_ret_out[0].astype(BF16), "sgu_norm_w": sgu_norm_w,
        "sgu_w_s": sgu_w_s[0].astype(BF16), "sgu_b_s": sgu_b_s[0][:, :, None],
        "w_sgu_out": w_sgu_out[0].astype(BF16), "w_out": w_out[0].astype(BF16),
        "xattn_norm_q": xattn_norm_q, "xattn_norm_mem": xattn_norm_mem,
        "xattn_w_q": xattn_w_q[0].astype(BF16), "xattn_w_kv": xattn_w_kv[0].astype(BF16),
        "xattn_w_o": xattn_w_o[0].astype(BF16),
        "ffn2_norm": ffn2_norm, "ffn2_w_gu": ffn2_w_gu[0].astype(BF16), "ffn2_w_down": ffn2_w_down[0].astype(BF16),
        "final_norm": final_norm[None, :],
    }


def kernel(x_prompt, x_sample, mem_prompt, mem_sample, ffn1_norm, ffn1_w_gu, ffn1_w_down, mix_norm, w_in, gate_bias, ret_decay_fwd, ret_decay_bwd, ret_gn_w, w_ret_out, sgu_norm_w, sgu_w_s, sgu_b_s, w_sgu_out, w_out, xattn_norm_q, xattn_norm_mem, xattn_w_q, xattn_w_kv, xattn_w_o, ffn2_norm, ffn2_w_gu, ffn2_w_down, final_norm):
    w = _prepare(ffn1_norm, ffn1_w_gu, ffn1_w_down, mix_norm, w_in, gate_bias, ret_decay_fwd, ret_decay_bwd, ret_gn_w, w_ret_out, sgu_norm_w, sgu_w_s, sgu_b_s, w_sgu_out, w_out, xattn_norm_q, xattn_norm_mem, xattn_w_q, xattn_w_kv, xattn_w_o, ffn2_norm, ffn2_w_gu, ffn2_w_down, final_norm)
    return _trunk(x_prompt, mem_prompt, w), _trunk(x_sample, mem_sample, w)
```

```python
import functools
import math

import jax
import jax.numpy as jnp
from jax import lax
from jax.experimental import pallas as pl
from jax.experimental.pallas import tpu as pltpu

F32 = jnp.float32
BF16 = jnp.bfloat16

EPS = 1e-6
FFN_RES_SCALE = 0.5
ROPE_BASE = 10000.0
RET_HEADS = 8
SGU_GROUPS = 8
SGU_CHUNK = 128
XATTN_HEADS = 4
SEG_Q, SEG_K, SEG_V, SEG_G, SEG_U, SEG_VS, SEG_GATE_R, SEG_GATE_S = range(8)

V7X_VMEM_BYTES = 64 * 1024 * 1024
V7X_MXU_COLS = 256

RET_BLOCK = 256

FFN_UP_ROWS = 1024
FFN_UP_COLS = 512
FFN_DOWN_ROWS = 512
IN_ROWS = 1024
IN_LN_ROWS = 512
SGU_ROWS = 512
OUT_ROWS = 256
KV_ROWS = 512
KV_COLS = 1024
XATTN_ROWS = 512
RESIDUAL_COL_TILE = 512
MATMUL_ROW_TILE = 512


def _nbytes(shape, dtype):
    return math.prod(shape) * jnp.dtype(dtype).itemsize


def _params(semantics, vmem_bytes):
    assert vmem_bytes <= V7X_VMEM_BYTES, vmem_bytes
    return pltpu.CompilerParams(dimension_semantics=semantics, vmem_limit_bytes=int(vmem_bytes))


def _resident(shape):
    return pl.BlockSpec(shape, lambda *_: (0,) * len(shape), pipeline_mode=pl.Buffered(1))


def _rms(x, w):
    return x * lax.rsqrt(jnp.mean(x * x, axis=-1, keepdims=True) + EPS) * w


def _dot(a, b):
    return jnp.dot(a, b, preferred_element_type=F32)


def _dot_nt(a, b):
    return lax.dot_general(a, b, (((1,), (1,)), ((), ())), preferred_element_type=F32)


def _dot_tn(a, b):
    return lax.dot_general(a, b, (((0,), (0,)), ((), ())), preferred_element_type=F32)


def _gelu_tanh(x):
    return x * (0.5 * (1.0 + jnp.tanh(math.sqrt(2.0 / math.pi) * (x + 0.044715 * (x * x * x)))))


def _residual_proj(x_ref, a, w_ref, o_ref, scale):
    rows, d = o_ref.shape
    ss = jnp.zeros((rows, 1), F32)
    for c in range(d // RESIDUAL_COL_TILE):
        cs = slice(c * RESIDUAL_COL_TILE, (c + 1) * RESIDUAL_COL_TILE)
        h = x_ref[:, cs] + scale * _dot(a, w_ref[:, cs])
        o_ref[:, cs] = h
        ss += jnp.sum(h * h, axis=-1, keepdims=True)
    return ss


def _rms_of_stored(o_ref, ss, w):
    return o_ref[...] * lax.rsqrt(ss * (1.0 / o_ref.shape[1]) + EPS) * w


def _ffn_up_kernel(*refs, norm_input):
    if norm_input:
        x_ref, nw_ref, wg_ref, wu_ref, o_ref, n_scr = refs

        @pl.when(pl.program_id(1) == 0)
        def _():
            n_scr[...] = _rms(x_ref[...], nw_ref[...]).astype(BF16)

        n_ref = n_scr
    else:
        n_ref, wg_ref, wu_ref, o_ref = refs
    for r in range(o_ref.shape[0] // MATMUL_ROW_TILE):
        rs = slice(r * MATMUL_ROW_TILE, (r + 1) * MATMUL_ROW_TILE)
        for c in range(o_ref.shape[1] // V7X_MXU_COLS):
            cs = slice(c * V7X_MXU_COLS, (c + 1) * V7X_MXU_COLS)
            g = _dot(n_ref[rs, :], wg_ref[:, cs])
            u = _dot(n_ref[rs, :], wu_ref[:, cs])
            o_ref[rs, cs] = (g * jax.nn.sigmoid(g) * u).astype(BF16)


def _ffn_up(x_or_n, norm_w, w_gu):
    t, d = x_or_n.shape
    d_ff = w_gu.shape[1] // 2
    n_col_steps = d_ff // FFN_UP_COLS
    assert t % FFN_UP_ROWS == 0 and d_ff % FFN_UP_COLS == 0
    norm_input = norm_w is not None
    row_spec = pl.BlockSpec((FFN_UP_ROWS, d), lambda i, j: (i, 0))
    w_specs = [pl.BlockSpec((d, FFN_UP_COLS), lambda i, j: (0, j)),
               pl.BlockSpec((d, FFN_UP_COLS), lambda i, j: (0, j + n_col_steps))]
    if norm_input:
        in_specs = [row_spec, pl.BlockSpec((1, d), lambda i, j: (0, 0))] + w_specs
        args = (x_or_n, norm_w, w_gu, w_gu)
        scratch = [pltpu.VMEM((FFN_UP_ROWS, d), BF16)]
    else:
        in_specs = [row_spec] + w_specs
        args = (x_or_n, w_gu, w_gu)
        scratch = []
    vmem = (2 * _nbytes((FFN_UP_ROWS, d), x_or_n.dtype) + _nbytes((FFN_UP_ROWS, d), BF16)
            + 4 * _nbytes((d, FFN_UP_COLS), BF16) + 2 * _nbytes((FFN_UP_ROWS, FFN_UP_COLS), BF16)
            + 4 * _nbytes((FFN_UP_ROWS, FFN_UP_COLS), F32))
    return pl.pallas_call(
        functools.partial(_ffn_up_kernel, norm_input=norm_input),
        grid=(t // FFN_UP_ROWS, n_col_steps),
        in_specs=in_specs,
        out_specs=pl.BlockSpec((FFN_UP_ROWS, FFN_UP_COLS), lambda i, j: (i, j)),
        out_shape=jax.ShapeDtypeStruct((t, d_ff), BF16),
        scratch_shapes=scratch,
        compiler_params=_params(("parallel", "arbitrary"), vmem),
        name="ffn_up_norm" if norm_input else "ffn_up",
    )(*args)


def _ffn_down_kernel(x_ref, hid_ref, wd_ref, nw_ref, o_ref, *n_out, final_norm):
    ss = _residual_proj(x_ref, hid_ref[...], wd_ref, o_ref, FFN_RES_SCALE)
    normed = _rms_of_stored(o_ref, ss, nw_ref[...])
    if final_norm:
        o_ref[...] = normed
    else:
        n_out[0][...] = normed.astype(BF16)


def _ffn_down(x, hidden, w_down, norm_w, final_norm):
    t, d = x.shape
    d_ff = hidden.shape[1]
    assert t % FFN_DOWN_ROWS == 0
    rows = lambda dt: jax.ShapeDtypeStruct((t, d), dt)
    row_spec = lambda: pl.BlockSpec((FFN_DOWN_ROWS, d), lambda i: (i, 0))
    vmem = (4 * _nbytes((FFN_DOWN_ROWS, d), F32) + 2 * _nbytes((FFN_DOWN_ROWS, d_ff), BF16)
            + _nbytes((d_ff, d), BF16) + 2 * _nbytes((FFN_DOWN_ROWS, d), BF16)
            + 4 * _nbytes((FFN_DOWN_ROWS, RESIDUAL_COL_TILE), F32))
    return pl.pallas_call(
        functools.partial(_ffn_down_kernel, final_norm=final_norm),
        grid=(t // FFN_DOWN_ROWS,),
        in_specs=[row_spec(), pl.BlockSpec((FFN_DOWN_ROWS, d_ff), lambda i: (i, 0)),
                  _resident((d_ff, d)), pl.BlockSpec((1, d), lambda i: (0, 0))],
        out_specs=row_spec() if final_norm else (row_spec(), row_spec()),
        out_shape=rows(F32) if final_norm else (rows(F32), rows(BF16)),
        compiler_params=_params(("parallel",), vmem),
        name="ffn_down_final" if final_norm else "ffn_down",
    )(x, hidden, w_down, norm_w)


def _in_proj_kernel(n_ref, w_ref, *refs, kind, k_scale):
    seg = pl.program_id(0)
    rows, d = n_ref.shape
    tile = V7X_MXU_COLS
    o_ref = refs[1] if kind == "gelu_layernorm" else refs[-1]

    def tiles():
        for r in range(rows // MATMUL_ROW_TILE):
            rs = slice(r * MATMUL_ROW_TILE, (r + 1) * MATMUL_ROW_TILE)
            for c in range(d // tile):
                cs = slice(c * tile, (c + 1) * tile)
                yield rs, cs, _dot(n_ref[rs, :], w_ref[:, cs])

    if kind == "rotary":
        cos_ref, sin_ref, _ = refs
        half = tile // 2
        scale = jnp.where(seg == 1, k_scale, 1.0).astype(F32)
        for rs, cs, z in tiles():
            x1, x2 = z[:, :half], z[:, half:]
            cos, sin = cos_ref[rs, :], sin_ref[rs, :]
            o_ref[0, rs, cs.start:cs.start + half] = ((x1 * cos - x2 * sin) * scale).astype(BF16)
            o_ref[0, rs, cs.start + half:cs.stop] = ((x1 * sin + x2 * cos) * scale).astype(BF16)
    elif kind == "plain":
        for rs, cs, z in tiles():
            o_ref[0, rs, cs] = z.astype(BF16)
    elif kind == "silu":
        for rs, cs, z in tiles():
            o_ref[0, rs, cs] = (z * jax.nn.sigmoid(z)).astype(BF16)
    elif kind == "gelu":
        for rs, cs, z in tiles():
            o_ref[0, rs, cs] = _gelu_tanh(z).astype(BF16)
    elif kind == "gelu_layernorm":
        lnw_ref, _, a_scr = refs
        for rs, cs, z in tiles():
            a_scr[rs, cs] = _gelu_tanh(z)
        a = a_scr[...]
        cen = a - jnp.mean(a, axis=-1, keepdims=True)
        var = jnp.mean(cen * cen, axis=-1, keepdims=True)
        o_ref[0] = (cen * lax.rsqrt(var + EPS) * lnw_ref[...]).astype(BF16)
    elif kind == "sigmoid_bias":
        gb_ref, _ = refs
        for rs, cs, z in tiles():
            o_ref[0, rs, cs] = jax.nn.sigmoid(z + gb_ref[pl.ds(seg, 1), cs]).astype(BF16)
    else:
        raise ValueError(kind)


def _in_proj(n, w_in, first_seg, n_seg, kind, extra=(), extra_specs=(), rows=IN_ROWS, k_scale=None):
    t, d = n.shape
    assert t % rows == 0 and w_in.shape[1] % d == 0
    scratch = [pltpu.VMEM((rows, d), F32)] if kind == "gelu_layernorm" else []
    vmem = (4 * _nbytes((rows, d), BF16) + 2 * _nbytes((d, d), BF16) + 3 * _nbytes((rows, d), F32))
    return pl.pallas_call(
        functools.partial(_in_proj_kernel, kind=kind, k_scale=k_scale),
        grid=(n_seg, t // rows),
        in_specs=[pl.BlockSpec((rows, d), lambda s, i: (i, 0)),
                  pl.BlockSpec((d, d), lambda s, i: (0, first_seg + s))] + list(extra_specs),
        out_specs=pl.BlockSpec((1, rows, d), lambda s, i: (s, i, 0)),
        out_shape=jax.ShapeDtypeStruct((n_seg, t, d), BF16),
        scratch_shapes=scratch,
        compiler_params=_params(("parallel", "parallel"), vmem),
        name="in_proj_" + kind,
    )(n, w_in, *extra)


def _decay_rate(dec_ref, h, shape):
    return -jnp.exp(jnp.full(shape, dec_ref[h], F32))


def _ret_bwd_state_kernel(decb_ref, k_ref, v_ref, sb_ref, state, kdec, cdec):
    c = pl.program_id(1)
    blk = k_ref.shape[1]
    head = k_ref.shape[2] // RET_HEADS

    @pl.when((pl.program_id(0) == 0) & (c == 0))
    def _():
        row = lax.broadcasted_iota(jnp.int32, (blk, head), 0).astype(F32)
        for h in range(RET_HEADS):
            kdec[h] = jnp.exp(_decay_rate(decb_ref, h, (blk, head)) * row)
            cdec[h] = jnp.exp(_decay_rate(decb_ref, h, (1, head)) * float(blk))

    @pl.when(c == 0)
    def _():
        state[...] = jnp.zeros_like(state)

    for h in range(RET_HEADS):
        sl = slice(h * head, (h + 1) * head)
        st = state[h]
        sb_ref[0, 0, h] = st.astype(BF16)
        kd = (k_ref[0, :, sl].astype(F32) * kdec[h]).astype(BF16)
        state[h] = st * cdec[h] + _dot_tn(kd, v_ref[0, :, sl])


def _ret_bwd_states(qk, v, dec_b, batch):
    _, t, d = qk.shape
    n_blk = t // batch // RET_BLOCK
    head = d // RET_HEADS

    def chunk_map(seg):
        return lambda b, c: (seg, b * n_blk + (n_blk - 1 - c), 0)

    blk_bytes = _nbytes((RET_BLOCK, d), BF16)
    table_bytes = _nbytes((RET_HEADS, RET_BLOCK, head), F32)
    return pl.pallas_call(
        _ret_bwd_state_kernel,
        grid=(batch, n_blk),
        in_specs=[
            pl.BlockSpec(memory_space=pltpu.SMEM),
            pl.BlockSpec((1, RET_BLOCK, d), chunk_map(1)),
            pl.BlockSpec((1, RET_BLOCK, d), chunk_map(0)),
        ],
        out_specs=pl.BlockSpec((1, 1, RET_HEADS, head, head), lambda b, c: (b, n_blk - 1 - c, 0, 0, 0)),
        out_shape=jax.ShapeDtypeStruct((batch, n_blk, RET_HEADS, head, head), BF16),
        scratch_shapes=[
            pltpu.VMEM((RET_HEADS, head, head), F32),
            pltpu.VMEM((RET_HEADS, RET_BLOCK, head), F32),
            pltpu.VMEM((RET_HEADS, 1, head), F32),
        ],
        compiler_params=_params(("arbitrary", "arbitrary"), 6 * blk_bytes + 4 * table_bytes),
        name="ret_bwd_states",
    )(dec_b, qk, v)


def _ret_out_kernel(decf_ref, decb_ref, q_ref, k_ref, v_ref, g_ref, sb_ref, gnw_ref, o_ref,
                    state, dmat, qdf, qdb, kdf, cdf):
    c = pl.program_id(1)
    blk = q_ref.shape[1]
    head = q_ref.shape[2] // RET_HEADS

    @pl.when((pl.program_id(0) == 0) & (c == 0))
    def _():
        row = lax.broadcasted_iota(jnp.int32, (blk, head), 0).astype(F32)
        ri = lax.broadcasted_iota(jnp.int32, (blk, blk), 0)
        ci = lax.broadcasted_iota(jnp.int32, (blk, blk), 1)
        dist = (ri - ci).astype(F32)
        for h in range(RET_HEADS):
            lf = _decay_rate(decf_ref, h, (blk, head))
            lb = _decay_rate(decb_ref, h, (blk, head))
            qdf[h] = jnp.exp(lf * (row + 1.0))
            qdb[h] = jnp.exp(lb * (float(blk) - row))
            kdf[h] = jnp.exp(lf * (float(blk) - 1.0 - row))
            cdf[h] = jnp.exp(_decay_rate(decf_ref, h, (1, head)) * float(blk))
            lf2 = _decay_rate(decf_ref, h, (blk, blk))
            lb2 = _decay_rate(decb_ref, h, (blk, blk))
            dmat[h] = jnp.where(ri >= ci, jnp.exp(lf2 * jnp.maximum(dist, 0.0)), jnp.exp(lb2 * jnp.maximum(-dist, 0.0)))

    @pl.when(c == 0)
    def _():
        state[...] = jnp.zeros_like(state)

    for h in range(RET_HEADS):
        sl = slice(h * head, (h + 1) * head)
        q = q_ref[0, :, sl]
        k = k_ref[0, :, sl]
        v = v_ref[0, :, sl]
        st = state[h]
        p = (_dot_nt(q, k) * dmat[h]).astype(BF16)
        qf = q.astype(F32)
        o = _dot(p, v)
        o += _dot((qf * qdf[h]).astype(BF16), st.astype(BF16))
        o += _dot((qf * qdb[h]).astype(BF16), sb_ref[0, 0, h])
        cen = o - jnp.mean(o, axis=-1, keepdims=True)
        var = jnp.mean(cen * cen, axis=-1, keepdims=True)
        on = cen * lax.rsqrt(var + EPS) * gnw_ref[:, sl]
        o_ref[:, sl] = (g_ref[0, :, sl].astype(F32) * on).astype(BF16)
        kd = (k.astype(F32) * kdf[h]).astype(BF16)
        state[h] = st * cdf[h] + _dot_tn(kd, v)


def _ret_out(qk, v, g, sb, dec_f, dec_b, gn_w, batch):
    _, t, d = qk.shape
    n_blk = t // batch // RET_BLOCK
    head = d // RET_HEADS
    chunk = lambda seg: pl.BlockSpec((1, RET_BLOCK, d), lambda b, c: (seg, b * n_blk + c, 0))
    blk_bytes = _nbytes((RET_BLOCK, d), BF16)
    table_bytes = _nbytes((RET_HEADS, RET_BLOCK, head), F32)
    return pl.pallas_call(
        _ret_out_kernel,
        grid=(batch, n_blk),
        in_specs=[
            pl.BlockSpec(memory_space=pltpu.SMEM),
            pl.BlockSpec(memory_space=pltpu.SMEM),
            chunk(0), chunk(1), chunk(0), chunk(0),
            pl.BlockSpec((1, 1, RET_HEADS, head, head), lambda b, c: (b, c, 0, 0, 0)),
            pl.BlockSpec((1, d), lambda b, c: (0, 0)),
        ],
        out_specs=pl.BlockSpec((RET_BLOCK, d), lambda b, c: (b * n_blk + c, 0)),
        out_shape=jax.ShapeDtypeStruct((t, d), BF16),
        scratch_shapes=[
            pltpu.VMEM((RET_HEADS, head, head), F32),
            pltpu.VMEM((RET_HEADS, RET_BLOCK, RET_BLOCK), F32),
            pltpu.VMEM((RET_HEADS, RET_BLOCK, head), F32),
            pltpu.VMEM((RET_HEADS, RET_BLOCK, head), F32),
            pltpu.VMEM((RET_HEADS, RET_BLOCK, head), F32),
            pltpu.VMEM((RET_HEADS, 1, head), F32),
        ],
        compiler_params=_params(("arbitrary", "arbitrary"), 12 * blk_bytes + 7 * table_bytes),
        name="ret_out",
    )(dec_f, dec_b, qk, qk, v, g, sb, gn_w)


def _sgu_kernel(u_ref, vs_ref, ws_ref, bs_ref, o_ref):
    rows = u_ref.shape[1]
    gdim = u_ref.shape[2] // SGU_GROUPS
    for c in range(rows // SGU_CHUNK):
        rs = slice(c * SGU_CHUNK, (c + 1) * SGU_CHUNK)
        for g in range(SGU_GROUPS):
            cs = slice(g * gdim, (g + 1) * gdim)
            mixed = _dot(ws_ref[g], vs_ref[0, rs, cs]) + bs_ref[g]
            o_ref[rs, cs] = (u_ref[0, rs, cs].astype(F32) * mixed).astype(BF16)


def _sgu(u, vs, w_s, b_s):
    _, t, d = u.shape
    assert t % SGU_ROWS == 0 and SGU_ROWS % SGU_CHUNK == 0
    rows = lambda: pl.BlockSpec((1, SGU_ROWS, d), lambda i: (0, i, 0))
    return pl.pallas_call(
        _sgu_kernel,
        grid=(t // SGU_ROWS,),
        in_specs=[
            rows(), rows(),
            pl.BlockSpec((SGU_GROUPS, SGU_CHUNK, SGU_CHUNK), lambda i: (0, 0, 0)),
            pl.BlockSpec((SGU_GROUPS, SGU_CHUNK, 1), lambda i: (0, 0, 0)),
        ],
        out_specs=pl.BlockSpec((SGU_ROWS, d), lambda i: (i, 0)),
        out_shape=jax.ShapeDtypeStruct((t, d), BF16),
        compiler_params=_params(("parallel",), 8 * _nbytes((SGU_ROWS, d), BF16) + 4 * _nbytes((SGU_ROWS, d), F32)),
        name="sgu",
    )(u, vs, w_s, b_s)


def _out_proj_kernel(ar_ref, as_ref, gr_ref, gs_ref, h_ref, wr_ref, ws_ref, wo_ref, o_ref):
    ret = _dot(ar_ref[...], wr_ref[...])
    sgu = _dot(as_ref[...], ws_ref[...])
    merged = gr_ref[0].astype(F32) * ret + gs_ref[0].astype(F32) * sgu
    o_ref[...] = h_ref[...] + _dot(merged.astype(BF16), wo_ref[...])


def _out_proj(ret_act, sgu_act, gates, h, w_ret_out, w_sgu_out, w_out):
    t, d = h.shape
    assert t % OUT_ROWS == 0
    rows = lambda: pl.BlockSpec((OUT_ROWS, d), lambda i: (i, 0))
    gate = lambda s: pl.BlockSpec((1, OUT_ROWS, d), lambda i: (s, i, 0))
    vmem = (3 * _nbytes((d, d), BF16) + 8 * _nbytes((OUT_ROWS, d), BF16) + 4 * _nbytes((OUT_ROWS, d), F32)
            + 4 * _nbytes((OUT_ROWS, d), F32))
    return pl.pallas_call(
        _out_proj_kernel,
        grid=(t // OUT_ROWS,),
        in_specs=[rows(), rows(), gate(0), gate(1), rows(),
                  _resident((d, d)), _resident((d, d)), _resident((d, d))],
        out_specs=rows(),
        out_shape=jax.ShapeDtypeStruct((t, d), F32),
        compiler_params=_params(("parallel",), vmem),
        name="out_proj",
    )(ret_act, sgu_act, gates, gates, h, w_ret_out, w_sgu_out, w_out)


def _mem_kv_kernel(m_ref, nw_ref, w_ref, o_ref, n_scr):
    @pl.when(pl.program_id(1) == 0)
    def _():
        n_scr[...] = _rms(m_ref[...], nw_ref[...]).astype(BF16)

    o_ref[...] = _dot(n_scr[...], w_ref[...]).astype(BF16)


def _mem_kv(mem, norm_w, w_kv):
    t, d = mem.shape
    rows = min(KV_ROWS, t)
    n_out = w_kv.shape[1]
    assert t % rows == 0 and n_out % KV_COLS == 0
    vmem = (2 * _nbytes((rows, d), F32) + _nbytes((rows, d), BF16) + 2 * _nbytes((d, KV_COLS), BF16)
            + 2 * _nbytes((rows, KV_COLS), BF16) + 2 * _nbytes((rows, d), F32))
    return pl.pallas_call(
        _mem_kv_kernel,
        grid=(t // rows, n_out // KV_COLS),
        in_specs=[
            pl.BlockSpec((rows, d), lambda i, j: (i, 0)),
            pl.BlockSpec((1, d), lambda i, j: (0, 0)),
            pl.BlockSpec((d, KV_COLS), lambda i, j: (0, j)),
        ],
        out_specs=pl.BlockSpec((rows, KV_COLS), lambda i, j: (i, j)),
        out_shape=jax.ShapeDtypeStruct((t, n_out), BF16),
        scratch_shapes=[pltpu.VMEM((rows, d), BF16)],
        compiler_params=_params(("parallel", "arbitrary"), vmem),
        name="mem_kv",
    )(mem, norm_w, w_kv)


def _xattn_kernel(h_ref, nw_ref, kv_ref, wq_ref, wo_ref, next_nw_ref, o_ref, n_out_ref, q_scr, a_scr):
    d = h_ref.shape[1]
    hd = d // XATTN_HEADS
    scale = float(hd) ** -0.5
    q_scr[...] = _dot(_rms(h_ref[...], nw_ref[...]).astype(BF16), wq_ref[...]).astype(BF16)
    for h in range(XATTN_HEADS):
        sl = slice(h * hd, (h + 1) * hd)
        s = _dot_nt(q_scr[:, sl], kv_ref[:, sl]) * scale
        e = jnp.exp(s - jnp.max(s, axis=-1, keepdims=True))
        p = e / jnp.sum(e, axis=-1, keepdims=True)
        a_scr[:, sl] = _dot(p.astype(BF16), kv_ref[:, d + h * hd:d + (h + 1) * hd]).astype(BF16)
    ss = _residual_proj(h_ref, a_scr[...], wo_ref, o_ref, 1.0)
    n_out_ref[...] = _rms_of_stored(o_ref, ss, next_nw_ref[...]).astype(BF16)


def _xattn(h, kv, norm_w, w_q, w_o, next_norm_w, batch):
    t, d = h.shape
    n_mem = kv.shape[0] // batch
    row_blocks = t // batch // XATTN_ROWS
    assert t % (batch * XATTN_ROWS) == 0
    rows = lambda: pl.BlockSpec((XATTN_ROWS, d), lambda b, i: (b * row_blocks + i, 0))
    vec = lambda: pl.BlockSpec((1, d), lambda b, i: (0, 0))
    vmem = (2 * _nbytes((d, d), BF16) + 4 * _nbytes((XATTN_ROWS, d), F32) + 4 * _nbytes((XATTN_ROWS, d), BF16)
            + 2 * _nbytes((n_mem, 2 * d), BF16) + 4 * _nbytes((XATTN_ROWS, d), F32))
    return pl.pallas_call(
        _xattn_kernel,
        grid=(batch, row_blocks),
        in_specs=[rows(), vec(), pl.BlockSpec((n_mem, 2 * d), lambda b, i: (b, 0)),
                  _resident((d, d)), _resident((d, d)), vec()],
        out_specs=(rows(), rows()),
        out_shape=(jax.ShapeDtypeStruct((t, d), F32), jax.ShapeDtypeStruct((t, d), BF16)),
        scratch_shapes=[pltpu.VMEM((XATTN_ROWS, d), BF16), pltpu.VMEM((XATTN_ROWS, d), BF16)],
        compiler_params=_params(("parallel", "parallel"), vmem),
        name="xattn",
    )(h, norm_w, kv, w_q, w_o, next_norm_w)


def _rope_tables(seq, half):
    freqs = ROPE_BASE ** (-jnp.linspace(0.0, 1.0, half, dtype=F32))
    ang = jnp.arange(seq, dtype=F32)[:, None] * freqs[None, :]
    return jnp.cos(ang), jnp.sin(ang)


def _trunk(x, mem, w):
    batch, seq, d = x.shape
    t = batch * seq
    x = x.reshape(t, d)
    half = d // RET_HEADS // 2
    cos, sin = _rope_tables(seq, half)
    pos_blocks = seq // IN_ROWS
    assert seq % IN_ROWS == 0
    pos_spec = lambda: pl.BlockSpec((IN_ROWS, half), lambda s, i: (i % pos_blocks, 0))

    hidden = _ffn_up(x, w["ffn1_norm"], w["ffn1_w_gu"])
    h, n = _ffn_down(x, hidden, w["ffn1_w_down"], w["mix_norm"], final_norm=False)

    w_in = w["w_in"]
    qk = _in_proj(n, w_in, SEG_Q, 2, "rotary", (cos, sin), (pos_spec(), pos_spec()), k_scale=float(2 * half) ** -0.5)
    v = _in_proj(n, w_in, SEG_V, 1, "plain")
    g = _in_proj(n, w_in, SEG_G, 1, "silu")
    u = _in_proj(n, w_in, SEG_U, 1, "gelu")
    vs = _in_proj(n, w_in, SEG_VS, 1, "gelu_layernorm", (w["sgu_norm_w"],),
                  (pl.BlockSpec((1, d), lambda s, i: (0, 0)),), rows=IN_LN_ROWS)
    gates = _in_proj(n, w_in, SEG_GATE_R, 2, "sigmoid_bias", (w["gate_bias"],),
                     (pl.BlockSpec((2, d), lambda s, i: (0, 0)),))

    sb = _ret_bwd_states(qk, v, w["ret_decay_bwd"], batch)
    ret_act = _ret_out(qk, v, g, sb, w["ret_decay_fwd"], w["ret_decay_bwd"], w["ret_gn_w"], batch)
    sgu_act = _sgu(u, vs, w["sgu_w_s"], w["sgu_b_s"])
    h = _out_proj(ret_act, sgu_act, gates, h, w["w_ret_out"], w["w_sgu_out"], w["w_out"])

    kv = _mem_kv(mem.reshape(-1, d), w["xattn_norm_mem"], w["xattn_w_kv"])
    h, n = _xattn(h, kv, w["xattn_norm_q"], w["xattn_w_q"], w["xattn_w_o"], w["ffn2_norm"], batch)

    hidden = _ffn_up(n, None, w["ffn2_w_gu"])
    y = _ffn_down(h, hidden, w["ffn2_w_down"], w["final_norm"], final_norm=True)
    return y.reshape(batch, seq, d)


def _prepare(ffn1_norm, ffn1_w_gu, ffn1_w_down, mix_norm, w_in, gate_bias, ret_decay_fwd, ret_decay_bwd, ret_gn_w, w_ret_out, sgu_norm_w, sgu_w_s, sgu_b_s, w_sgu_out, w_out, xattn_norm_q, xattn_norm_mem, xattn_w_q, xattn_w_kv, xattn_w_o, ffn2_norm, ffn2_w_gu, ffn2_w_down, final_norm):
    assert ffn1_norm.shape[0] == 1, "single-layer trunk"
    return {
        "ffn1_norm": ffn1_norm, "ffn1_w_gu": ffn1_w_gu[0].astype(BF16), "ffn1_w_down": ffn1_w_down[0].astype(BF16),
        "mix_norm": mix_norm, "w_in": w_in[0].astype(BF16), "gate_bias": gate_bias[0],
        "ret_decay_fwd": ret_decay_fwd[0], "ret_decay_bwd": ret_decay_bwd[0], "ret_gn_w": ret_gn_w,
        "w_ret_out": w_ret_out[0].astype(BF16), "sgu_norm_w": sgu_norm_w,
        "sgu_w_s": sgu_w_s[0].astype(BF16), "sgu_b_s": sgu_b_s[0][:, :, None],
        "w_sgu_out": w_sgu_out[0].astype(BF16), "w_out": w_out[0].astype(BF16),
        "xattn_norm_q": xattn_norm_q, "xattn_norm_mem": xattn_norm_mem,
        "xattn_w_q": xattn_w_q[0].astype(BF16), "xattn_w_kv": xattn_w_kv[0].astype(BF16),
        "xattn_w_o": xattn_w_o[0].astype(BF16),
        "ffn2_norm": ffn2_norm, "ffn2_w_gu": ffn2_w_gu[0].astype(BF16), "ffn2_w_down": ffn2_w_down[0].astype(BF16),
        "final_norm": final_norm[None, :],
    }


def kernel(x_prompt, x_sample, mem_prompt, mem_sample, ffn1_norm, ffn1_w_gu, ffn1_w_down, mix_norm, w_in, gate_bias, ret_decay_fwd, ret_decay_bwd, ret_gn_w, w_ret_out, sgu_norm_w, sgu_w_s, sgu_b_s, w_sgu_out, w_out, xattn_norm_q, xattn_norm_mem, xattn_w_q, xattn_w_kv, xattn_w_o, ffn2_norm, ffn2_w_gu, ffn2_w_down, final_norm):
    w = _prepare(ffn1_norm, ffn1_w_gu, ffn1_w_down, mix_norm, w_in, gate_bias, ret_decay_fwd, ret_decay_bwd, ret_gn_w, w_ret_out, sgu_norm_w, sgu_w_s, sgu_b_s, w_sgu_out, w_out, xattn_norm_q, xattn_norm_mem, xattn_w_q, xattn_w_kv, xattn_w_o, ffn2_norm, ffn2_w_gu, ffn2_w_down, final_norm)
    return _trunk(x_prompt, mem_prompt, w), _trunk(x_sample, mem_sample, w)
```

```python
import functools
import math

import jax
import jax.numpy as jnp
from jax import lax
from jax.experimental import pallas as pl
from jax.experimental.pallas import tpu as pltpu

F32 = jnp.float32
BF16 = jnp.bfloat16
U32 = jnp.uint32
PACK = 2

EPS = 1e-6
FFN_RES_SCALE = 0.5
ROPE_BASE = 10000.0
RET_HEADS = 8
SGU_GROUPS = 8
SGU_CHUNK = 128
XATTN_HEADS = 4
SEG_Q, SEG_K, SEG_V, SEG_G, SEG_U, SEG_VS, SEG_GATE_R, SEG_GATE_S = range(8)

V7X_VMEM_BYTES = 64 * 1024 * 1024
V7X_MXU_COLS = 256

RET_BLOCK = 256

PACK_ROWS = 512
FFN_UP_ROWS = 1024
FFN_UP_COLS = 512
FFN_DOWN_ROWS = 512
IN_ROWS = 1024
IN_LN_ROWS = 512
SGU_ROWS = 512
OUT_ROWS = 256
KV_ROWS = 512
KV_COLS = 1024
XATTN_ROWS = 512
RESIDUAL_COL_TILE = 512
MATMUL_ROW_TILE = 512


def _nbytes(shape, dtype):
    return math.prod(shape) * jnp.dtype(dtype).itemsize


def _params(semantics, vmem_bytes):
    assert vmem_bytes <= V7X_VMEM_BYTES, vmem_bytes
    return pltpu.CompilerParams(dimension_semantics=semantics, vmem_limit_bytes=int(vmem_bytes))


def _resident(shape):
    return pl.BlockSpec(shape, lambda *_: (0,) * len(shape), pipeline_mode=pl.Buffered(1))


def _unpack(words):
    return pltpu.bitcast(words, BF16)


def _pack(vals):
    return pltpu.bitcast(vals.astype(BF16), U32)


def _packed(rows):
    return slice(rows.start // PACK, rows.stop // PACK)


def _tiles(n_rows, n_cols, row_tile, col_tile):
    for r in range(n_rows // row_tile):
        for c in range(n_cols // col_tile):
            yield slice(r * row_tile, (r + 1) * row_tile), slice(c * col_tile, (c + 1) * col_tile)


def _rms(x, w):
    return x * lax.rsqrt(jnp.mean(x * x, axis=-1, keepdims=True) + EPS) * w


def _dot(a, b):
    return jnp.dot(a, b, preferred_element_type=F32)


def _dot_nt(a, b):
    return lax.dot_general(a, b, (((1,), (1,)), ((), ())), preferred_element_type=F32)


def _dot_tn(a, b):
    return lax.dot_general(a, b, (((0,), (0,)), ((), ())), preferred_element_type=F32)


def _gelu_tanh(x):
    return x * (0.5 * (1.0 + jnp.tanh(math.sqrt(2.0 / math.pi) * (x + 0.044715 * (x * x * x)))))


def _residual_proj(x_ref, a, w_ref, o_ref, scale):
    rows, d = o_ref.shape
    ss = jnp.zeros((rows, 1), F32)
    for _, cs in _tiles(rows, d, rows, RESIDUAL_COL_TILE):
        h = x_ref[:, cs] + scale * _dot(a, _unpack(w_ref[:, cs]))
        o_ref[:, cs] = h
        ss += jnp.sum(h * h, axis=-1, keepdims=True)
    return ss


def _rms_of_stored(o_ref, ss, w):
    return o_ref[...] * lax.rsqrt(ss * (1.0 / o_ref.shape[1]) + EPS) * w


def _pack_kernel(w_ref, o_ref):
    o_ref[...] = _pack(w_ref[...])


def _pack_weight(w):
    k, n = w.shape
    rows = min(PACK_ROWS, k)
    cols = next(c for c in (2048, 1408, 1024, 512, 256, 128) if n % c == 0)
    assert k % rows == 0 and rows % (8 * PACK) == 0
    return pl.pallas_call(
        _pack_kernel,
        grid=(k // rows, n // cols),
        in_specs=[pl.BlockSpec((rows, cols), lambda i, j: (i, j))],
        out_specs=pl.BlockSpec((rows // PACK, cols), lambda i, j: (i, j)),
        out_shape=jax.ShapeDtypeStruct((k // PACK, n), U32),
        compiler_params=_params(("parallel", "parallel"), 4 * _nbytes((rows, cols), F32)),
        name="pack_weight",
    )(w)


def _ffn_up_kernel(*refs, norm_input):
    if norm_input:
        x_ref, nw_ref, wg_ref, wu_ref, o_ref, n_scr = refs

        @pl.when(pl.program_id(1) == 0)
        def _():
            n_scr[...] = _rms(x_ref[...], nw_ref[...]).astype(BF16)

        lhs = lambda rs: n_scr[rs, :]
    else:
        n_ref, wg_ref, wu_ref, o_ref = refs
        lhs = lambda rs: _unpack(n_ref[_packed(rs), :])
    for rs, cs in _tiles(o_ref.shape[0] * PACK, o_ref.shape[1], MATMUL_ROW_TILE, V7X_MXU_COLS):
        g = _dot(lhs(rs), _unpack(wg_ref[:, cs]))
        u = _dot(lhs(rs), _unpack(wu_ref[:, cs]))
        o_ref[_packed(rs), cs] = _pack(g * jax.nn.sigmoid(g) * u)


def _ffn_up(x_or_n, norm_w, w_gu, t):
    d = x_or_n.shape[1]
    d_ff = w_gu.shape[1] // 2
    n_col_steps = d_ff // FFN_UP_COLS
    assert t % FFN_UP_ROWS == 0 and d_ff % FFN_UP_COLS == 0
    norm_input = norm_w is not None
    w_specs = [pl.BlockSpec((d // PACK, FFN_UP_COLS), lambda i, j: (0, j)),
               pl.BlockSpec((d // PACK, FFN_UP_COLS), lambda i, j: (0, j + n_col_steps))]
    if norm_input:
        in_specs = [pl.BlockSpec((FFN_UP_ROWS, d), lambda i, j: (i, 0)), pl.BlockSpec((1, d), lambda i, j: (0, 0))] + w_specs
        args = (x_or_n, norm_w, w_gu, w_gu)
        scratch = [pltpu.VMEM((FFN_UP_ROWS, d), BF16)]
    else:
        in_specs = [pl.BlockSpec((FFN_UP_ROWS // PACK, d), lambda i, j: (i, 0))] + w_specs
        args = (x_or_n, w_gu, w_gu)
        scratch = []
    vmem = (2 * _nbytes((FFN_UP_ROWS, d), F32 if norm_input else BF16) + _nbytes((FFN_UP_ROWS, d), BF16)
            + 4 * _nbytes((d, FFN_UP_COLS), BF16) + 2 * _nbytes((FFN_UP_ROWS, FFN_UP_COLS), BF16)
            + 4 * _nbytes((FFN_UP_ROWS, FFN_UP_COLS), F32))
    return pl.pallas_call(
        functools.partial(_ffn_up_kernel, norm_input=norm_input),
        grid=(t // FFN_UP_ROWS, n_col_steps),
        in_specs=in_specs,
        out_specs=pl.BlockSpec((FFN_UP_ROWS // PACK, FFN_UP_COLS), lambda i, j: (i, j)),
        out_shape=jax.ShapeDtypeStruct((t // PACK, d_ff), U32),
        scratch_shapes=scratch,
        compiler_params=_params(("parallel", "arbitrary"), vmem),
        name="ffn_up_norm" if norm_input else "ffn_up",
    )(*args)


def _ffn_down_kernel(x_ref, hid_ref, wd_ref, nw_ref, o_ref, *n_out, final_norm):
    ss = _residual_proj(x_ref, _unpack(hid_ref[...]), wd_ref, o_ref, FFN_RES_SCALE)
    normed = _rms_of_stored(o_ref, ss, nw_ref[...])
    if final_norm:
        o_ref[...] = normed
    else:
        n_out[0][...] = _pack(normed)


def _ffn_down(x, hidden, w_down, norm_w, final_norm):
    t, d = x.shape
    d_ff = hidden.shape[1]
    assert t % FFN_DOWN_ROWS == 0
    row_spec = lambda: pl.BlockSpec((FFN_DOWN_ROWS, d), lambda i: (i, 0))
    packed_spec = lambda cols: pl.BlockSpec((FFN_DOWN_ROWS // PACK, cols), lambda i: (i, 0))
    vmem = (4 * _nbytes((FFN_DOWN_ROWS, d), F32) + 2 * _nbytes((FFN_DOWN_ROWS, d_ff), BF16)
            + _nbytes((d_ff, d), BF16) + 2 * _nbytes((FFN_DOWN_ROWS, d), BF16)
            + 4 * _nbytes((FFN_DOWN_ROWS, RESIDUAL_COL_TILE), F32))
    out_f32 = jax.ShapeDtypeStruct((t, d), F32)
    return pl.pallas_call(
        functools.partial(_ffn_down_kernel, final_norm=final_norm),
        grid=(t // FFN_DOWN_ROWS,),
        in_specs=[row_spec(), packed_spec(d_ff), _resident((d_ff // PACK, d)), pl.BlockSpec((1, d), lambda i: (0, 0))],
        out_specs=row_spec() if final_norm else (row_spec(), packed_spec(d)),
        out_shape=out_f32 if final_norm else (out_f32, jax.ShapeDtypeStruct((t // PACK, d), U32)),
        compiler_params=_params(("parallel",), vmem),
        name="ffn_down_final" if final_norm else "ffn_down",
    )(x, hidden, w_down, norm_w)


def _in_proj_kernel(n_ref, w_ref, *refs, kind, k_scale):
    seg = pl.program_id(0)
    rows, d = n_ref.shape[0] * PACK, n_ref.shape[1]
    o_ref = refs[1] if kind == "gelu_layernorm" else refs[-1]

    def z_tiles():
        for rs, cs in _tiles(rows, d, MATMUL_ROW_TILE, V7X_MXU_COLS):
            yield rs, cs, _dot(_unpack(n_ref[_packed(rs), :]), _unpack(w_ref[:, cs]))

    def store(rs, cs, val):
        o_ref[0, _packed(rs), cs] = _pack(val)

    if kind == "rotary":
        cos_ref, sin_ref, _ = refs
        half = V7X_MXU_COLS // 2
        scale = jnp.where(seg == 1, k_scale, 1.0).astype(F32)
        for rs, cs, z in z_tiles():
            x1, x2 = z[:, :half], z[:, half:]
            cos, sin = cos_ref[rs, :], sin_ref[rs, :]
            store(rs, slice(cs.start, cs.start + half), (x1 * cos - x2 * sin) * scale)
            store(rs, slice(cs.start + half, cs.stop), (x1 * sin + x2 * cos) * scale)
    elif kind == "plain":
        for rs, cs, z in z_tiles():
            store(rs, cs, z)
    elif kind == "silu":
        for rs, cs, z in z_tiles():
            store(rs, cs, z * jax.nn.sigmoid(z))
    elif kind == "gelu":
        for rs, cs, z in z_tiles():
            store(rs, cs, _gelu_tanh(z))
    elif kind == "gelu_layernorm":
        lnw_ref, _, a_scr = refs
        for rs, cs, z in z_tiles():
            a_scr[rs, cs] = _gelu_tanh(z)
        a = a_scr[...]
        cen = a - jnp.mean(a, axis=-1, keepdims=True)
        var = jnp.mean(cen * cen, axis=-1, keepdims=True)
        o_ref[0] = _pack(cen * lax.rsqrt(var + EPS) * lnw_ref[...])
    elif kind == "sigmoid_bias":
        gb_ref, _ = refs
        for rs, cs, z in z_tiles():
            store(rs, cs, jax.nn.sigmoid(z + gb_ref[pl.ds(seg, 1), cs]))
    else:
        raise ValueError(kind)


def _in_proj(n, w_in, first_seg, n_seg, kind, extra=(), extra_specs=(), rows=IN_ROWS, k_scale=None):
    t, d = n.shape[0] * PACK, n.shape[1]
    assert t % rows == 0 and w_in.shape[1] % d == 0
    scratch = [pltpu.VMEM((rows, d), F32)] if kind == "gelu_layernorm" else []
    vmem = (4 * _nbytes((rows, d), BF16) + 2 * _nbytes((d, d), BF16) + 3 * _nbytes((rows, d), F32))
    return pl.pallas_call(
        functools.partial(_in_proj_kernel, kind=kind, k_scale=k_scale),
        grid=(n_seg, t // rows),
        in_specs=[pl.BlockSpec((rows // PACK, d), lambda s, i: (i, 0)),
                  pl.BlockSpec((d // PACK, d), lambda s, i: (0, first_seg + s))] + list(extra_specs),
        out_specs=pl.BlockSpec((1, rows // PACK, d), lambda s, i: (s, i, 0)),
        out_shape=jax.ShapeDtypeStruct((n_seg, t // PACK, d), U32),
        scratch_shapes=scratch,
        compiler_params=_params(("parallel", "parallel"), vmem),
        name="in_proj_" + kind,
    )(n, w_in, *extra)


def _decay_rate(dec_ref, h, shape):
    return -jnp.exp(jnp.full(shape, dec_ref[h], F32))


def _ret_bwd_state_kernel(decb_ref, k_ref, v_ref, sb_ref, state, kdec, cdec):
    c = pl.program_id(1)
    blk = k_ref.shape[1] * PACK
    head = k_ref.shape[2] // RET_HEADS

    @pl.when((pl.program_id(0) == 0) & (c == 0))
    def _():
        row = lax.broadcasted_iota(jnp.int32, (blk, head), 0).astype(F32)
        for h in range(RET_HEADS):
            kdec[h] = jnp.exp(_decay_rate(decb_ref, h, (blk, head)) * row)
            cdec[h] = jnp.exp(_decay_rate(decb_ref, h, (1, head)) * float(blk))

    @pl.when(c == 0)
    def _():
        state[...] = jnp.zeros_like(state)

    for h in range(RET_HEADS):
        sl = slice(h * head, (h + 1) * head)
        st = state[h]
        sb_ref[0, 0, h] = _pack(st)
        kd = (_unpack(k_ref[0, :, sl]).astype(F32) * kdec[h]).astype(BF16)
        state[h] = st * cdec[h] + _dot_tn(kd, _unpack(v_ref[0, :, sl]))


def _ret_bwd_states(qk, v, dec_b, batch):
    _, t_packed, d = qk.shape
    n_blk = t_packed * PACK // batch // RET_BLOCK
    head = d // RET_HEADS

    def chunk_map(seg):
        return lambda b, c: (seg, b * n_blk + (n_blk - 1 - c), 0)

    blk_bytes = _nbytes((RET_BLOCK, d), BF16)
    table_bytes = _nbytes((RET_HEADS, RET_BLOCK, head), F32)
    return pl.pallas_call(
        _ret_bwd_state_kernel,
        grid=(batch, n_blk),
        in_specs=[
            pl.BlockSpec(memory_space=pltpu.SMEM),
            pl.BlockSpec((1, RET_BLOCK // PACK, d), chunk_map(1)),
            pl.BlockSpec((1, RET_BLOCK // PACK, d), chunk_map(0)),
        ],
        out_specs=pl.BlockSpec((1, 1, RET_HEADS, head // PACK, head), lambda b, c: (b, n_blk - 1 - c, 0, 0, 0)),
        out_shape=jax.ShapeDtypeStruct((batch, n_blk, RET_HEADS, head // PACK, head), U32),
        scratch_shapes=[
            pltpu.VMEM((RET_HEADS, head, head), F32),
            pltpu.VMEM((RET_HEADS, RET_BLOCK, head), F32),
            pltpu.VMEM((RET_HEADS, 1, head), F32),
        ],
        compiler_params=_params(("arbitrary", "arbitrary"), 6 * blk_bytes + 4 * table_bytes),
        name="ret_bwd_states",
    )(dec_b, qk, v)


def _ret_out_kernel(decf_ref, decb_ref, q_ref, k_ref, v_ref, g_ref, sb_ref, gnw_ref, o_ref,
                    state, dmat, qdf, qdb, kdf, cdf):
    c = pl.program_id(1)
    blk = q_ref.shape[1] * PACK
    head = q_ref.shape[2] // RET_HEADS

    @pl.when((pl.program_id(0) == 0) & (c == 0))
    def _():
        row = lax.broadcasted_iota(jnp.int32, (blk, head), 0).astype(F32)
        ri = lax.broadcasted_iota(jnp.int32, (blk, blk), 0)
        ci = lax.broadcasted_iota(jnp.int32, (blk, blk), 1)
        dist = (ri - ci).astype(F32)
        for h in range(RET_HEADS):
            lf = _decay_rate(decf_ref, h, (blk, head))
            lb = _decay_rate(decb_ref, h, (blk, head))
            qdf[h] = jnp.exp(lf * (row + 1.0))
            qdb[h] = jnp.exp(lb * (float(blk) - row))
            kdf[h] = jnp.exp(lf * (float(blk) - 1.0 - row))
            cdf[h] = jnp.exp(_decay_rate(decf_ref, h, (1, head)) * float(blk))
            lf2 = _decay_rate(decf_ref, h, (blk, blk))
            lb2 = _decay_rate(decb_ref, h, (blk, blk))
            dmat[h] = jnp.where(ri >= ci, jnp.exp(lf2 * jnp.maximum(dist, 0.0)), jnp.exp(lb2 * jnp.maximum(-dist, 0.0)))

    @pl.when(c == 0)
    def _():
        state[...] = jnp.zeros_like(state)

    for h in range(RET_HEADS):
        sl = slice(h * head, (h + 1) * head)
        q = _unpack(q_ref[0, :, sl])
        k = _unpack(k_ref[0, :, sl])
        v = _unpack(v_ref[0, :, sl])
        st = state[h]
        p = (_dot_nt(q, k) * dmat[h]).astype(BF16)
        qf = q.astype(F32)
        o = _dot(p, v)
        o += _dot((qf * qdf[h]).astype(BF16), st.astype(BF16))
        o += _dot((qf * qdb[h]).astype(BF16), _unpack(sb_ref[0, 0, h]))
        cen = o - jnp.mean(o, axis=-1, keepdims=True)
        var = jnp.mean(cen * cen, axis=-1, keepdims=True)
        on = cen * lax.rsqrt(var + EPS) * gnw_ref[:, sl]
        o_ref[:, sl] = _pack(_unpack(g_ref[0, :, sl]).astype(F32) * on)
        kd = (k.astype(F32) * kdf[h]).astype(BF16)
        state[h] = st * cdf[h] + _dot_tn(kd, v)


def _ret_out(qk, v, g, sb, dec_f, dec_b, gn_w, batch):
    _, t_packed, d = qk.shape
    n_blk = t_packed * PACK // batch // RET_BLOCK
    head = d // RET_HEADS
    chunk = lambda seg: pl.BlockSpec((1, RET_BLOCK // PACK, d), lambda b, c: (seg, b * n_blk + c, 0))
    blk_bytes = _nbytes((RET_BLOCK, d), BF16)
    table_bytes = _nbytes((RET_HEADS, RET_BLOCK, head), F32)
    return pl.pallas_call(
        _ret_out_kernel,
        grid=(batch, n_blk),
        in_specs=[
            pl.BlockSpec(memory_space=pltpu.SMEM),
            pl.BlockSpec(memory_space=pltpu.SMEM),
            chunk(0), chunk(1), chunk(0), chunk(0),
            pl.BlockSpec((1, 1, RET_HEADS, head // PACK, head), lambda b, c: (b, c, 0, 0, 0)),
            pl.BlockSpec((1, d), lambda b, c: (0, 0)),
        ],
        out_specs=pl.BlockSpec((RET_BLOCK // PACK, d), lambda b, c: (b * n_blk + c, 0)),
        out_shape=jax.ShapeDtypeStruct((t_packed, d), U32),
        scratch_shapes=[
            pltpu.VMEM((RET_HEADS, head, head), F32),
            pltpu.VMEM((RET_HEADS, RET_BLOCK, RET_BLOCK), F32),
            pltpu.VMEM((RET_HEADS, RET_BLOCK, head), F32),
            pltpu.VMEM((RET_HEADS, RET_BLOCK, head), F32),
            pltpu.VMEM((RET_HEADS, RET_BLOCK, head), F32),
            pltpu.VMEM((RET_HEADS, 1, head), F32),
        ],
        compiler_params=_params(("arbitrary", "arbitrary"), 12 * blk_bytes + 7 * table_bytes),
        name="ret_out",
    )(dec_f, dec_b, qk, qk, v, g, sb, gn_w)


def _sgu_kernel(u_ref, vs_ref, ws_ref, bs_ref, o_ref):
    rows = u_ref.shape[1] * PACK
    d = u_ref.shape[2]
    for rs, cs in _tiles(rows, d, SGU_CHUNK, d // SGU_GROUPS):
        g = cs.start // (d // SGU_GROUPS)
        mixed = _dot(ws_ref[g], _unpack(vs_ref[0, _packed(rs), cs])) + bs_ref[g]
        o_ref[_packed(rs), cs] = _pack(_unpack(u_ref[0, _packed(rs), cs]).astype(F32) * mixed)


def _sgu(u, vs, w_s, b_s):
    _, t_packed, d = u.shape
    t = t_packed * PACK
    assert t % SGU_ROWS == 0 and SGU_ROWS % SGU_CHUNK == 0
    rows = lambda: pl.BlockSpec((1, SGU_ROWS // PACK, d), lambda i: (0, i, 0))
    return pl.pallas_call(
        _sgu_kernel,
        grid=(t // SGU_ROWS,),
        in_specs=[
            rows(), rows(),
            pl.BlockSpec((SGU_GROUPS, SGU_CHUNK, SGU_CHUNK), lambda i: (0, 0, 0)),
            pl.BlockSpec((SGU_GROUPS, SGU_CHUNK, 1), lambda i: (0, 0, 0)),
        ],
        out_specs=pl.BlockSpec((SGU_ROWS // PACK, d), lambda i: (i, 0)),
        out_shape=jax.ShapeDtypeStruct((t_packed, d), U32),
        compiler_params=_params(("parallel",), 8 * _nbytes((SGU_ROWS, d), BF16) + 4 * _nbytes((SGU_ROWS, d), F32)),
        name="sgu",
    )(u, vs, w_s, b_s)


def _out_proj_kernel(ar_ref, as_ref, gr_ref, gs_ref, h_ref, wr_ref, ws_ref, wo_ref, o_ref):
    ret = _dot(_unpack(ar_ref[...]), _unpack(wr_ref[...]))
    sgu = _dot(_unpack(as_ref[...]), _unpack(ws_ref[...]))
    merged = _unpack(gr_ref[0]).astype(F32) * ret + _unpack(gs_ref[0]).astype(F32) * sgu
    o_ref[...] = h_ref[...] + _dot(merged.astype(BF16), _unpack(wo_ref[...]))


def _out_proj(ret_act, sgu_act, gates, h, w_ret_out, w_sgu_out, w_out):
    t, d = h.shape
    assert t % OUT_ROWS == 0
    rows = lambda: pl.BlockSpec((OUT_ROWS, d), lambda i: (i, 0))
    packed_rows = lambda: pl.BlockSpec((OUT_ROWS // PACK, d), lambda i: (i, 0))
    gate = lambda s: pl.BlockSpec((1, OUT_ROWS // PACK, d), lambda i: (s, i, 0))
    weight = lambda: _resident((d // PACK, d))
    vmem = (3 * _nbytes((d, d), BF16) + 8 * _nbytes((OUT_ROWS, d), BF16) + 4 * _nbytes((OUT_ROWS, d), F32)
            + 4 * _nbytes((OUT_ROWS, d), F32))
    return pl.pallas_call(
        _out_proj_kernel,
        grid=(t // OUT_ROWS,),
        in_specs=[packed_rows(), packed_rows(), gate(0), gate(1), rows(), weight(), weight(), weight()],
        out_specs=rows(),
        out_shape=jax.ShapeDtypeStruct((t, d), F32),
        compiler_params=_params(("parallel",), vmem),
        name="out_proj",
    )(ret_act, sgu_act, gates, gates, h, w_ret_out, w_sgu_out, w_out)


def _mem_kv_kernel(m_ref, nw_ref, w_ref, o_ref, n_scr):
    @pl.when(pl.program_id(1) == 0)
    def _():
        n_scr[...] = _rms(m_ref[...], nw_ref[...]).astype(BF16)

    o_ref[...] = _pack(_dot(n_scr[...], _unpack(w_ref[...])))


def _mem_kv(mem, norm_w, w_kv):
    t, d = mem.shape
    rows = min(KV_ROWS, t)
    n_out = w_kv.shape[1]
    assert t % rows == 0 and n_out % KV_COLS == 0
    vmem = (2 * _nbytes((rows, d), F32) + _nbytes((rows, d), BF16) + 2 * _nbytes((d, KV_COLS), BF16)
            + 2 * _nbytes((rows, KV_COLS), BF16) + 2 * _nbytes((rows, d), F32))
    return pl.pallas_call(
        _mem_kv_kernel,
        grid=(t // rows, n_out // KV_COLS),
        in_specs=[
            pl.BlockSpec((rows, d), lambda i, j: (i, 0)),
            pl.BlockSpec((1, d), lambda i, j: (0, 0)),
            pl.BlockSpec((d // PACK, KV_COLS), lambda i, j: (0, j)),
        ],
        out_specs=pl.BlockSpec((rows // PACK, KV_COLS), lambda i, j: (i, j)),
        out_shape=jax.ShapeDtypeStruct((t // PACK, n_out), U32),
        scratch_shapes=[pltpu.VMEM((rows, d), BF16)],
        compiler_params=_params(("parallel", "arbitrary"), vmem),
        name="mem_kv",
    )(mem, norm_w, w_kv)


def _xattn_kernel(h_ref, nw_ref, kv_ref, wq_ref, wo_ref, next_nw_ref, o_ref, n_out_ref, q_scr, a_scr):
    d = h_ref.shape[1]
    hd = d // XATTN_HEADS
    scale = float(hd) ** -0.5
    q_scr[...] = _dot(_rms(h_ref[...], nw_ref[...]).astype(BF16), _unpack(wq_ref[...])).astype(BF16)
    for h in range(XATTN_HEADS):
        sl = slice(h * hd, (h + 1) * hd)
        s = _dot_nt(q_scr[:, sl], _unpack(kv_ref[:, sl])) * scale
        e = jnp.exp(s - jnp.max(s, axis=-1, keepdims=True))
        p = e / jnp.sum(e, axis=-1, keepdims=True)
        a_scr[:, sl] = _dot(p.astype(BF16), _unpack(kv_ref[:, d + h * hd:d + (h + 1) * hd])).astype(BF16)
    ss = _residual_proj(h_ref, a_scr[...], wo_ref, o_ref, 1.0)
    n_out_ref[...] = _pack(_rms_of_stored(o_ref, ss, next_nw_ref[...]))


def _xattn(h, kv, norm_w, w_q, w_o, next_norm_w, batch):
    t, d = h.shape
    n_mem = kv.shape[0] * PACK // batch
    row_blocks = t // batch // XATTN_ROWS
    assert t % (batch * XATTN_ROWS) == 0
    rows = lambda: pl.BlockSpec((XATTN_ROWS, d), lambda b, i: (b * row_blocks + i, 0))
    vec = lambda: pl.BlockSpec((1, d), lambda b, i: (0, 0))
    weight = lambda: _resident((d // PACK, d))
    vmem = (2 * _nbytes((d, d), BF16) + 4 * _nbytes((XATTN_ROWS, d), F32) + 4 * _nbytes((XATTN_ROWS, d), BF16)
            + 2 * _nbytes((n_mem, 2 * d), BF16) + 4 * _nbytes((XATTN_ROWS, d), F32))
    return pl.pallas_call(
        _xattn_kernel,
        grid=(batch, row_blocks),
        in_specs=[rows(), vec(), pl.BlockSpec((n_mem // PACK, 2 * d), lambda b, i: (b, 0)), weight(), weight(), vec()],
        out_specs=(rows(), pl.BlockSpec((XATTN_ROWS // PACK, d), lambda b, i: (b * row_blocks + i, 0))),
        out_shape=(jax.ShapeDtypeStruct((t, d), F32), jax.ShapeDtypeStruct((t // PACK, d), U32)),
        scratch_shapes=[pltpu.VMEM((XATTN_ROWS, d), BF16), pltpu.VMEM((XATTN_ROWS, d), BF16)],
        compiler_params=_params(("parallel", "parallel"), vmem),
        name="xattn",
    )(h, norm_w, kv, w_q, w_o, next_norm_w)


def _rope_tables(seq, half):
    freqs = ROPE_BASE ** (-jnp.linspace(0.0, 1.0, half, dtype=F32))
    ang = jnp.arange(seq, dtype=F32)[:, None] * freqs[None, :]
    return jnp.cos(ang), jnp.sin(ang)


def _trunk(x, mem, w):
    batch, seq, d = x.shape
    t = batch * seq
    x = x.reshape(t, d)
    half = d // RET_HEADS // 2
    cos, sin = _rope_tables(seq, half)
    pos_blocks = seq // IN_ROWS
    assert seq % IN_ROWS == 0
    pos_spec = lambda: pl.BlockSpec((IN_ROWS, half), lambda s, i: (i % pos_blocks, 0))

    hidden = _ffn_up(x, w["ffn1_norm"], w["ffn1_w_gu"], t)
    h, n = _ffn_down(x, hidden, w["ffn1_w_down"], w["mix_norm"], final_norm=False)

    w_in = w["w_in"]
    qk = _in_proj(n, w_in, SEG_Q, 2, "rotary", (cos, sin), (pos_spec(), pos_spec()), k_scale=float(2 * half) ** -0.5)
    v = _in_proj(n, w_in, SEG_V, 1, "plain")
    g = _in_proj(n, w_in, SEG_G, 1, "silu")
    u = _in_proj(n, w_in, SEG_U, 1, "gelu")
    vs = _in_proj(n, w_in, SEG_VS, 1, "gelu_layernorm", (w["sgu_norm_w"],),
                  (pl.BlockSpec((1, d), lambda s, i: (0, 0)),), rows=IN_LN_ROWS)
    gates = _in_proj(n, w_in, SEG_GATE_R, 2, "sigmoid_bias", (w["gate_bias"],),
                     (pl.BlockSpec((2, d), lambda s, i: (0, 0)),))

    sb = _ret_bwd_states(qk, v, w["ret_decay_bwd"], batch)
    ret_act = _ret_out(qk, v, g, sb, w["ret_decay_fwd"], w["ret_decay_bwd"], w["ret_gn_w"], batch)
    sgu_act = _sgu(u, vs, w["sgu_w_s"], w["sgu_b_s"])
    h = _out_proj(ret_act, sgu_act, gates, h, w["w_ret_out"], w["w_sgu_out"], w["w_out"])

    kv = _mem_kv(mem.reshape(-1, d), w["xattn_norm_mem"], w["xattn_w_kv"])
    h, n = _xattn(h, kv, w["xattn_norm_q"], w["xattn_w_q"], w["xattn_w_o"], w["ffn2_norm"], batch)

    hidden = _ffn_up(n, None, w["ffn2_w_gu"], t)
    y = _ffn_down(h, hidden, w["ffn2_w_down"], w["final_norm"], final_norm=True)
    return y.reshape(batch, seq, d)


def _prepare(ffn1_norm, ffn1_w_gu, ffn1_w_down, mix_norm, w_in, gate_bias, ret_decay_fwd, ret_decay_bwd, ret_gn_w, w_ret_out, sgu_norm_w, sgu_w_s, sgu_b_s, w_sgu_out, w_out, xattn_norm_q, xattn_norm_mem, xattn_w_q, xattn_w_kv, xattn_w_o, ffn2_norm, ffn2_w_gu, ffn2_w_down, final_norm):
    assert ffn1_norm.shape[0] == 1, "single-layer trunk"
    pw = lambda w: _pack_weight(w[0])
    return {
        "ffn1_norm": ffn1_norm, "ffn1_w_gu": pw(ffn1_w_gu), "ffn1_w_down": pw(ffn1_w_down),
        "mix_norm": mix_norm, "w_in": pw(w_in), "gate_bias": gate_bias[0],
        "ret_decay_fwd": ret_decay_fwd[0], "ret_decay_bwd": ret_decay_bwd[0], "ret_gn_w": ret_gn_w,
        "w_ret_out": pw(w_ret_out), "sgu_norm_w": sgu_norm_w,
        "sgu_w_s": sgu_w_s[0].astype(BF16), "sgu_b_s": sgu_b_s[0][:, :, None],
        "w_sgu_out": pw(w_sgu_out), "w_out": pw(w_out),
        "xattn_norm_q": xattn_norm_q, "xattn_norm_mem": xattn_norm_mem,
        "xattn_w_q": pw(xattn_w_q), "xattn_w_kv": pw(xattn_w_kv), "xattn_w_o": pw(xattn_w_o),
        "ffn2_norm": ffn2_norm, "ffn2_w_gu": pw(ffn2_w_gu), "ffn2_w_down": pw(ffn2_w_down),
        "final_norm": final_norm[None, :],
    }


def kernel(x_prompt, x_sample, mem_prompt, mem_sample, ffn1_norm, ffn1_w_gu, ffn1_w_down, mix_norm, w_in, gate_bias, ret_decay_fwd, ret_decay_bwd, ret_gn_w, w_ret_out, sgu_norm_w, sgu_w_s, sgu_b_s, w_sgu_out, w_out, xattn_norm_q, xattn_norm_mem, xattn_w_q, xattn_w_kv, xattn_w_o, ffn2_norm, ffn2_w_gu, ffn2_w_down, final_norm):
    w = _prepare(ffn1_norm, ffn1_w_gu, ffn1_w_down, mix_norm, w_in, gate_bias, ret_decay_fwd, ret_decay_bwd, ret_gn_w, w_ret_out, sgu_norm_w, sgu_w_s, sgu_b_s, w_sgu_out, w_out, xattn_norm_q, xattn_norm_mem, xattn_w_q, xattn_w_kv, xattn_w_o, ffn2_norm, ffn2_w_gu, ffn2_w_down, final_norm)
    return _trunk(x_prompt, mem_prompt, w), _trunk(x_sample, mem_sample, w)
```

```python
import functools
import math

import jax
import jax.numpy as jnp
from jax import lax
from jax.experimental import pallas as pl
from jax.experimental.pallas import tpu as pltpu

F32 = jnp.float32
BF16 = jnp.bfloat16
U32 = jnp.uint32
PACK = 2

EPS = 1e-6
FFN_RES_SCALE = 0.5
ROPE_BASE = 10000.0
RET_HEADS = 8
SGU_GROUPS = 8
SGU_CHUNK = 128
XATTN_HEADS = 4
SEG_Q, SEG_K, SEG_V, SEG_G, SEG_U, SEG_VS, SEG_GATE_R, SEG_GATE_S = range(8)

V7X_VMEM_BYTES = 64 * 1024 * 1024
V7X_MXU_COLS = 256

RET_BLOCK = 256

PACK_ROWS = 512
FFN_UP_ROWS = 1024
FFN_UP_COLS = 1408
GATE_UP_LANES = 128
FFN_DOWN_ROWS = 512
IN_ROWS = 1024
IN_LN_ROWS = 512
SGU_ROWS = 512
OUT_ROWS = 256
KV_ROWS = 512
KV_COLS = 1024
XATTN_ROWS = 512
RESIDUAL_COL_TILE = 512
MATMUL_ROW_TILE = 512


def _nbytes(shape, dtype):
    return math.prod(shape) * jnp.dtype(dtype).itemsize


def _params(semantics, vmem_bytes):
    assert vmem_bytes <= V7X_VMEM_BYTES, vmem_bytes
    return pltpu.CompilerParams(dimension_semantics=semantics, vmem_limit_bytes=int(vmem_bytes))


def _resident(shape):
    return pl.BlockSpec(shape, lambda *_: (0,) * len(shape), pipeline_mode=pl.Buffered(1))


def _unpack(words):
    return pltpu.bitcast(words, BF16)


def _pack(vals):
    return pltpu.bitcast(vals.astype(BF16), U32)


def _packed(rows):
    return slice(rows.start // PACK, rows.stop // PACK)


def _tiles(n_rows, n_cols, row_tile, col_tile):
    for r in range(n_rows // row_tile):
        for c in range(n_cols // col_tile):
            yield slice(r * row_tile, (r + 1) * row_tile), slice(c * col_tile, (c + 1) * col_tile)


def _rms(x, w):
    return x * lax.rsqrt(jnp.mean(x * x, axis=-1, keepdims=True) + EPS) * w


def _dot(a, b):
    return jnp.dot(a, b, preferred_element_type=F32)


def _dot_nt(a, b):
    return lax.dot_general(a, b, (((1,), (1,)), ((), ())), preferred_element_type=F32)


def _dot_tn(a, b):
    return lax.dot_general(a, b, (((0,), (0,)), ((), ())), preferred_element_type=F32)


def _sigmoid(x):
    return 0.5 * jnp.tanh(0.5 * x) + 0.5


def _gelu_tanh(x):
    c = math.sqrt(2.0 / math.pi)
    return x * (0.5 * jnp.tanh(x * (c + (c * 0.044715) * (x * x))) + 0.5)


def _residual_proj(x_ref, a, w_ref, o_ref, scale):
    rows, d = o_ref.shape
    ss = jnp.zeros((rows, 1), F32)
    for _, cs in _tiles(rows, d, rows, RESIDUAL_COL_TILE):
        h = x_ref[:, cs] + scale * _dot(a, _unpack(w_ref[:, cs]))
        o_ref[:, cs] = h
        ss += jnp.sum(h * h, axis=-1, keepdims=True)
    return ss


def _rms_of_stored(o_ref, ss, w):
    return o_ref[...] * lax.rsqrt(ss * (1.0 / o_ref.shape[1]) + EPS) * w


def _pack_kernel(w_ref, o_ref):
    o_ref[...] = _pack(w_ref[...])


def _pack_weight(w):
    k, n = w.shape
    rows = min(PACK_ROWS, k)
    cols = next(c for c in (2048, 1408, 1024, 512, 256, 128) if n % c == 0)
    assert k % rows == 0 and rows % (8 * PACK) == 0
    return pl.pallas_call(
        _pack_kernel,
        grid=(k // rows, n // cols),
        in_specs=[pl.BlockSpec((rows, cols), lambda i, j: (i, j))],
        out_specs=pl.BlockSpec((rows // PACK, cols), lambda i, j: (i, j)),
        out_shape=jax.ShapeDtypeStruct((k // PACK, n), U32),
        compiler_params=_params(("parallel", "parallel"), 4 * _nbytes((rows, cols), F32)),
        name="pack_weight",
    )(w)


def _pack_gate_up_kernel(g_ref, u_ref, o_ref):
    o_ref[:, :GATE_UP_LANES] = _pack(g_ref[...])
    o_ref[:, GATE_UP_LANES:] = _pack(u_ref[...])


def _pack_gate_up(w_gu):
    k, n = w_gu.shape
    n_blocks = n // 2 // GATE_UP_LANES
    assert n % (2 * GATE_UP_LANES) == 0
    return pl.pallas_call(
        _pack_gate_up_kernel,
        grid=(n_blocks,),
        in_specs=[pl.BlockSpec((k, GATE_UP_LANES), lambda j: (0, j)),
                  pl.BlockSpec((k, GATE_UP_LANES), lambda j: (0, j + n_blocks))],
        out_specs=pl.BlockSpec((k // PACK, 2 * GATE_UP_LANES), lambda j: (0, j)),
        out_shape=jax.ShapeDtypeStruct((k // PACK, n), U32),
        compiler_params=_params(("parallel",), 8 * _nbytes((k, GATE_UP_LANES), F32)),
        name="pack_gate_up",
    )(w_gu, w_gu)


def _ffn_up_kernel(*refs, norm_input):
    if norm_input:
        x_ref, nw_ref, w_ref, o_ref, n_scr = refs

        @pl.when(pl.program_id(1) == 0)
        def _():
            n_scr[...] = _rms(x_ref[...], nw_ref[...]).astype(BF16)

        lhs = lambda rs: n_scr[rs, :]
    else:
        n_ref, w_ref, o_ref = refs
        lhs = lambda rs: _unpack(n_ref[_packed(rs), :])
    for rs, cs in _tiles(o_ref.shape[0] * PACK, w_ref.shape[1], MATMUL_ROW_TILE, 2 * GATE_UP_LANES):
        z = _dot(lhs(rs), _unpack(w_ref[:, cs]))
        g, u = z[:, :GATE_UP_LANES], z[:, GATE_UP_LANES:]
        o_ref[_packed(rs), cs.start // 2:cs.stop // 2] = _pack(g * _sigmoid(g) * u)


def _ffn_up(x_or_n, norm_w, w_gu, t):
    d = x_or_n.shape[1]
    d_ff = w_gu.shape[1] // 2
    assert t % FFN_UP_ROWS == 0 and d_ff % FFN_UP_COLS == 0 and FFN_UP_COLS % GATE_UP_LANES == 0
    norm_input = norm_w is not None
    w_spec = pl.BlockSpec((d // PACK, 2 * FFN_UP_COLS), lambda i, j: (0, j))
    if norm_input:
        in_specs = [pl.BlockSpec((FFN_UP_ROWS, d), lambda i, j: (i, 0)), pl.BlockSpec((1, d), lambda i, j: (0, 0)), w_spec]
        args = (x_or_n, norm_w, w_gu)
        scratch = [pltpu.VMEM((FFN_UP_ROWS, d), BF16)]
    else:
        in_specs = [pl.BlockSpec((FFN_UP_ROWS // PACK, d), lambda i, j: (i, 0)), w_spec]
        args = (x_or_n, w_gu)
        scratch = []
    vmem = (2 * _nbytes((FFN_UP_ROWS, d), F32 if norm_input else BF16) + _nbytes((FFN_UP_ROWS, d), BF16)
            + 4 * _nbytes((d, FFN_UP_COLS), BF16) + 2 * _nbytes((FFN_UP_ROWS, FFN_UP_COLS), BF16)
            + 8 * _nbytes((MATMUL_ROW_TILE, 2 * GATE_UP_LANES), F32))
    return pl.pallas_call(
        functools.partial(_ffn_up_kernel, norm_input=norm_input),
        grid=(t // FFN_UP_ROWS, d_ff // FFN_UP_COLS),
        in_specs=in_specs,
        out_specs=pl.BlockSpec((FFN_UP_ROWS // PACK, FFN_UP_COLS), lambda i, j: (i, j)),
        out_shape=jax.ShapeDtypeStruct((t // PACK, d_ff), U32),
        scratch_shapes=scratch,
        compiler_params=_params(("parallel", "arbitrary"), vmem),
        name="ffn_up_norm" if norm_input else "ffn_up",
    )(*args)


def _ffn_down_kernel(x_ref, hid_ref, wd_ref, nw_ref, o_ref, *n_out, final_norm):
    ss = _residual_proj(x_ref, _unpack(hid_ref[...]), wd_ref, o_ref, FFN_RES_SCALE)
    normed = _rms_of_stored(o_ref, ss, nw_ref[...])
    if final_norm:
        o_ref[...] = normed
    else:
        n_out[0][...] = _pack(normed)


def _ffn_down(x, hidden, w_down, norm_w, final_norm):
    t, d = x.shape
    d_ff = hidden.shape[1]
    assert t % FFN_DOWN_ROWS == 0
    row_spec = lambda: pl.BlockSpec((FFN_DOWN_ROWS, d), lambda i: (i, 0))
    packed_spec = lambda cols: pl.BlockSpec((FFN_DOWN_ROWS // PACK, cols), lambda i: (i, 0))
    vmem = (4 * _nbytes((FFN_DOWN_ROWS, d), F32) + 2 * _nbytes((FFN_DOWN_ROWS, d_ff), BF16)
            + _nbytes((d_ff, d), BF16) + 2 * _nbytes((FFN_DOWN_ROWS, d), BF16)
            + 4 * _nbytes((FFN_DOWN_ROWS, RESIDUAL_COL_TILE), F32))
    out_f32 = jax.ShapeDtypeStruct((t, d), F32)
    return pl.pallas_call(
        functools.partial(_ffn_down_kernel, final_norm=final_norm),
        grid=(t // FFN_DOWN_ROWS,),
        in_specs=[row_spec(), packed_spec(d_ff), _resident((d_ff // PACK, d)), pl.BlockSpec((1, d), lambda i: (0, 0))],
        out_specs=row_spec() if final_norm else (row_spec(), packed_spec(d)),
        out_shape=out_f32 if final_norm else (out_f32, jax.ShapeDtypeStruct((t // PACK, d), U32)),
        compiler_params=_params(("parallel",), vmem),
        name="ffn_down_final" if final_norm else "ffn_down",
    )(x, hidden, w_down, norm_w)


def _in_proj_kernel(n_ref, w_ref, *refs, kind, k_scale):
    seg = pl.program_id(0)
    rows, d = n_ref.shape[0] * PACK, n_ref.shape[1]
    o_ref = refs[1] if kind == "gelu_layernorm" else refs[-1]

    def z_tiles():
        for rs, cs in _tiles(rows, d, MATMUL_ROW_TILE, V7X_MXU_COLS):
            yield rs, cs, _dot(_unpack(n_ref[_packed(rs), :]), _unpack(w_ref[:, cs]))

    def store(rs, cs, val):
        o_ref[0, _packed(rs), cs] = _pack(val)

    if kind in ("rotary", "rotary_transposed"):
        cos_ref, sin_ref, _ = refs
        half = V7X_MXU_COLS // 2
        for rs, cs, z in z_tiles():
            x1, x2 = z[:, :half], z[:, half:]
            cos, sin = cos_ref[rs, :], sin_ref[rs, :]
            if kind == "rotary":
                store(rs, slice(cs.start, cs.start + half), x1 * cos - x2 * sin)
                store(rs, slice(cs.start + half, cs.stop), x1 * sin + x2 * cos)
            else:
                rot = jnp.concatenate([x1 * cos - x2 * sin, x1 * sin + x2 * cos], axis=1) * k_scale
                o_ref[cs.start // V7X_MXU_COLS, :, rs] = _pack(rot.T)
    elif kind == "plain":
        for rs, cs, z in z_tiles():
            store(rs, cs, z)
    elif kind == "silu_gain":
        gain_ref, _ = refs
        for rs, cs, z in z_tiles():
            store(rs, cs, z * _sigmoid(z) * gain_ref[:, cs])
    elif kind == "gelu":
        for rs, cs, z in z_tiles():
            store(rs, cs, _gelu_tanh(z))
    elif kind == "gelu_layernorm":
        lnw_ref, _, a_scr = refs
        for rs, cs, z in z_tiles():
            a_scr[rs, cs] = _gelu_tanh(z)
        a = a_scr[...]
        cen = a - jnp.mean(a, axis=-1, keepdims=True)
        var = jnp.mean(cen * cen, axis=-1, keepdims=True)
        o_ref[0] = _pack(cen * lax.rsqrt(var + EPS) * lnw_ref[...])
    elif kind == "sigmoid_bias":
        gb_ref, _ = refs
        for rs, cs, z in z_tiles():
            store(rs, cs, _sigmoid(z + gb_ref[pl.ds(seg, 1), cs]))
    else:
        raise ValueError(kind)


def _in_proj(n, w_in, first_seg, n_seg, kind, extra=(), extra_specs=(), rows=IN_ROWS, k_scale=None):
    t, d = n.shape[0] * PACK, n.shape[1]
    assert t % rows == 0 and w_in.shape[1] % d == 0
    scratch = [pltpu.VMEM((rows, d), F32)] if kind == "gelu_layernorm" else []
    vmem = (4 * _nbytes((rows, d), BF16) + 2 * _nbytes((d, d), BF16) + 3 * _nbytes((rows, d), F32))
    if kind == "rotary_transposed":
        assert n_seg == 1
        head = d // RET_HEADS
        out_spec = pl.BlockSpec((RET_HEADS, head // PACK, rows), lambda s, i: (0, 0, i))
        out_shape = jax.ShapeDtypeStruct((RET_HEADS, head // PACK, t), U32)
    else:
        out_spec = pl.BlockSpec((1, rows // PACK, d), lambda s, i: (s, i, 0))
        out_shape = jax.ShapeDtypeStruct((n_seg, t // PACK, d), U32)
    return pl.pallas_call(
        functools.partial(_in_proj_kernel, kind=kind, k_scale=k_scale),
        grid=(n_seg, t // rows),
        in_specs=[pl.BlockSpec((rows // PACK, d), lambda s, i: (i, 0)),
                  pl.BlockSpec((d // PACK, d), lambda s, i: (0, first_seg + s))] + list(extra_specs),
        out_specs=out_spec,
        out_shape=out_shape,
        scratch_shapes=scratch,
        compiler_params=_params(("parallel", "parallel"), vmem),
        name="in_proj_" + kind,
    )(n, w_in, *extra)


def _decay_rate(dec_ref, h, shape):
    return -jnp.exp(jnp.full(shape, dec_ref[h], F32))


def _ret_bwd_state_kernel(decb_ref, kt_ref, v_ref, sb_ref, state, kdec, cdec):
    c = pl.program_id(1)
    blk = kt_ref.shape[2]
    head = v_ref.shape[2] // RET_HEADS

    @pl.when((pl.program_id(0) == 0) & (c == 0))
    def _():
        col = lax.broadcasted_iota(jnp.int32, (1, blk), 1).astype(F32)
        for h in range(RET_HEADS):
            kdec[h] = jnp.exp(_decay_rate(decb_ref, h, (1, blk)) * col)
            cdec[h] = jnp.exp(_decay_rate(decb_ref, h, (1, head)) * float(blk))

    @pl.when(c == 0)
    def _():
        state[...] = jnp.zeros_like(state)

    for h in range(RET_HEADS):
        sl = slice(h * head, (h + 1) * head)
        st = state[h]
        sb_ref[0, 0, h] = _pack(st)
        kd = (_unpack(kt_ref[h]).astype(F32) * kdec[h]).astype(BF16)
        state[h] = st * cdec[h] + _dot(kd, _unpack(v_ref[0, :, sl]))


def _ret_bwd_states(kt, v, dec_b, batch):
    _, t_packed, d = v.shape
    n_blk = t_packed * PACK // batch // RET_BLOCK
    head = d // RET_HEADS
    blk_bytes = _nbytes((RET_BLOCK, d), BF16)
    state_bytes = _nbytes((RET_HEADS, head, head), F32)
    return pl.pallas_call(
        _ret_bwd_state_kernel,
        grid=(batch, n_blk),
        in_specs=[
            pl.BlockSpec(memory_space=pltpu.SMEM),
            pl.BlockSpec((RET_HEADS, head // PACK, RET_BLOCK), lambda b, c: (0, 0, b * n_blk + (n_blk - 1 - c))),
            pl.BlockSpec((1, RET_BLOCK // PACK, d), lambda b, c: (0, b * n_blk + (n_blk - 1 - c), 0)),
        ],
        out_specs=pl.BlockSpec((1, 1, RET_HEADS, head // PACK, head), lambda b, c: (b, n_blk - 1 - c, 0, 0, 0)),
        out_shape=jax.ShapeDtypeStruct((batch, n_blk, RET_HEADS, head // PACK, head), U32),
        scratch_shapes=[
            pltpu.VMEM((RET_HEADS, head, head), F32),
            pltpu.VMEM((RET_HEADS, 1, RET_BLOCK), F32),
            pltpu.VMEM((RET_HEADS, 1, head), F32),
        ],
        compiler_params=_params(("arbitrary", "arbitrary"), 6 * blk_bytes + 4 * state_bytes),
        name="ret_bwd_states",
    )(dec_b, kt, v)


def _ret_out_kernel(decf_ref, decb_ref, q_ref, kt_ref, v_ref, g_ref, sb_ref, o_ref,
                    state, dmat, qdf, qdb, kdf, cdf):
    c = pl.program_id(1)
    blk = q_ref.shape[1] * PACK
    head = q_ref.shape[2] // RET_HEADS

    @pl.when((pl.program_id(0) == 0) & (c == 0))
    def _():
        row = lax.broadcasted_iota(jnp.int32, (blk, head), 0).astype(F32)
        col = lax.broadcasted_iota(jnp.int32, (1, blk), 1).astype(F32)
        ri = lax.broadcasted_iota(jnp.int32, (blk, blk), 0)
        ci = lax.broadcasted_iota(jnp.int32, (blk, blk), 1)
        dist = (ri - ci).astype(F32)
        for h in range(RET_HEADS):
            qdf[h] = jnp.exp(_decay_rate(decf_ref, h, (blk, head)) * (row + 1.0))
            qdb[h] = jnp.exp(_decay_rate(decb_ref, h, (blk, head)) * (float(blk) - row))
            kdf[h] = jnp.exp(_decay_rate(decf_ref, h, (1, blk)) * (float(blk) - 1.0 - col))
            cdf[h] = jnp.exp(_decay_rate(decf_ref, h, (1, head)) * float(blk))
            lf2 = _decay_rate(decf_ref, h, (blk, blk))
            lb2 = _decay_rate(decb_ref, h, (blk, blk))
            dmat[h] = jnp.where(ri >= ci, jnp.exp(lf2 * jnp.maximum(dist, 0.0)), jnp.exp(lb2 * jnp.maximum(-dist, 0.0)))

    @pl.when(c == 0)
    def _():
        state[...] = jnp.zeros_like(state)

    for h in range(RET_HEADS):
        sl = slice(h * head, (h + 1) * head)
        q = _unpack(q_ref[0, :, sl])
        kt = _unpack(kt_ref[h])
        v = _unpack(v_ref[0, :, sl])
        st = state[h]
        p = (_dot(q, kt) * dmat[h]).astype(BF16)
        o = _dot(p, v) + _dot(q, st.astype(BF16)) * qdf[h] + _dot(q, _unpack(sb_ref[0, 0, h])) * qdb[h]
        cen = o - jnp.mean(o, axis=-1, keepdims=True)
        var = jnp.mean(cen * cen, axis=-1, keepdims=True)
        o_ref[:, sl] = _pack(_unpack(g_ref[0, :, sl]).astype(F32) * (cen * lax.rsqrt(var + EPS)))
        kd = (kt.astype(F32) * kdf[h]).astype(BF16)
        state[h] = st * cdf[h] + _dot(kd, v)


def _ret_out(q, kt, v, g, sb, dec_f, dec_b, batch):
    _, t_packed, d = q.shape
    n_blk = t_packed * PACK // batch // RET_BLOCK
    head = d // RET_HEADS
    chunk = lambda: pl.BlockSpec((1, RET_BLOCK // PACK, d), lambda b, c: (0, b * n_blk + c, 0))
    blk_bytes = _nbytes((RET_BLOCK, d), BF16)
    table_bytes = _nbytes((RET_HEADS, RET_BLOCK, head), F32)
    return pl.pallas_call(
        _ret_out_kernel,
        grid=(batch, n_blk),
        in_specs=[
            pl.BlockSpec(memory_space=pltpu.SMEM),
            pl.BlockSpec(memory_space=pltpu.SMEM),
            chunk(),
            pl.BlockSpec((RET_HEADS, head // PACK, RET_BLOCK), lambda b, c: (0, 0, b * n_blk + c)),
            chunk(), chunk(),
            pl.BlockSpec((1, 1, RET_HEADS, head // PACK, head), lambda b, c: (b, c, 0, 0, 0)),
        ],
        out_specs=pl.BlockSpec((RET_BLOCK // PACK, d), lambda b, c: (b * n_blk + c, 0)),
        out_shape=jax.ShapeDtypeStruct((t_packed, d), U32),
        scratch_shapes=[
            pltpu.VMEM((RET_HEADS, head, head), F32),
            pltpu.VMEM((RET_HEADS, RET_BLOCK, RET_BLOCK), F32),
            pltpu.VMEM((RET_HEADS, RET_BLOCK, head), F32),
            pltpu.VMEM((RET_HEADS, RET_BLOCK, head), F32),
            pltpu.VMEM((RET_HEADS, 1, RET_BLOCK), F32),
            pltpu.VMEM((RET_HEADS, 1, head), F32),
        ],
        compiler_params=_params(("arbitrary", "arbitrary"), 12 * blk_bytes + 6 * table_bytes),
        name="ret_out",
    )(dec_f, dec_b, q, kt, v, g, sb)


def _sgu_kernel(u_ref, vs_ref, ws_ref, bs_ref, o_ref):
    rows = u_ref.shape[1] * PACK
    d = u_ref.shape[2]
    for rs, cs in _tiles(rows, d, SGU_CHUNK, d // SGU_GROUPS):
        g = cs.start // (d // SGU_GROUPS)
        mixed = _dot(ws_ref[g], _unpack(vs_ref[0, _packed(rs), cs])) + bs_ref[g]
        o_ref[_packed(rs), cs] = _pack(_unpack(u_ref[0, _packed(rs), cs]).astype(F32) * mixed)


def _sgu(u, vs, w_s, b_s):
    _, t_packed, d = u.shape
    t = t_packed * PACK
    assert t % SGU_ROWS == 0 and SGU_ROWS % SGU_CHUNK == 0
    rows = lambda: pl.BlockSpec((1, SGU_ROWS // PACK, d), lambda i: (0, i, 0))
    return pl.pallas_call(
        _sgu_kernel,
        grid=(t // SGU_ROWS,),
        in_specs=[
            rows(), rows(),
            pl.BlockSpec((SGU_GROUPS, SGU_CHUNK, SGU_CHUNK), lambda i: (0, 0, 0)),
            pl.BlockSpec((SGU_GROUPS, SGU_CHUNK, 1), lambda i: (0, 0, 0)),
        ],
        out_specs=pl.BlockSpec((SGU_ROWS // PACK, d), lambda i: (i, 0)),
        out_shape=jax.ShapeDtypeStruct((t_packed, d), U32),
        compiler_params=_params(("parallel",), 8 * _nbytes((SGU_ROWS, d), BF16) + 4 * _nbytes((SGU_ROWS, d), F32)),
        name="sgu",
    )(u, vs, w_s, b_s)


def _out_proj_kernel(ar_ref, as_ref, gr_ref, gs_ref, h_ref, wr_ref, ws_ref, wo_ref, o_ref):
    ret = _dot(_unpack(ar_ref[...]), _unpack(wr_ref[...]))
    sgu = _dot(_unpack(as_ref[...]), _unpack(ws_ref[...]))
    merged = _unpack(gr_ref[0]).astype(F32) * ret + _unpack(gs_ref[0]).astype(F32) * sgu
    o_ref[...] = h_ref[...] + _dot(merged.astype(BF16), _unpack(wo_ref[...]))


def _out_proj(ret_act, sgu_act, gates, h, w_ret_out, w_sgu_out, w_out):
    t, d = h.shape
    assert t % OUT_ROWS == 0
    rows = lambda: pl.BlockSpec((OUT_ROWS, d), lambda i: (i, 0))
    packed_rows = lambda: pl.BlockSpec((OUT_ROWS // PACK, d), lambda i: (i, 0))
    gate = lambda s: pl.BlockSpec((1, OUT_ROWS // PACK, d), lambda i: (s, i, 0))
    weight = lambda: _resident((d // PACK, d))
    vmem = (3 * _nbytes((d, d), BF16) + 8 * _nbytes((OUT_ROWS, d), BF16) + 4 * _nbytes((OUT_ROWS, d), F32)
            + 4 * _nbytes((OUT_ROWS, d), F32))
    return pl.pallas_call(
        _out_proj_kernel,
        grid=(t // OUT_ROWS,),
        in_specs=[packed_rows(), packed_rows(), gate(0), gate(1), rows(), weight(), weight(), weight()],
        out_specs=rows(),
        out_shape=jax.ShapeDtypeStruct((t, d), F32),
        compiler_params=_params(("parallel",), vmem),
        name="out_proj",
    )(ret_act, sgu_act, gates, gates, h, w_ret_out, w_sgu_out, w_out)


def _mem_kv_kernel(m_ref, nw_ref, w_ref, o_ref, n_scr):
    @pl.when(pl.program_id(1) == 0)
    def _():
        n_scr[...] = _rms(m_ref[...], nw_ref[...]).astype(BF16)

    o_ref[...] = _pack(_dot(n_scr[...], _unpack(w_ref[...])))


def _mem_kv(mem, norm_w, w_kv):
    t, d = mem.shape
    rows = min(KV_ROWS, t)
    n_out = w_kv.shape[1]
    assert t % rows == 0 and n_out % KV_COLS == 0
    vmem = (2 * _nbytes((rows, d), F32) + _nbytes((rows, d), BF16) + 2 * _nbytes((d, KV_COLS), BF16)
            + 2 * _nbytes((rows, KV_COLS), BF16) + 2 * _nbytes((rows, d), F32))
    return pl.pallas_call(
        _mem_kv_kernel,
        grid=(t // rows, n_out // KV_COLS),
        in_specs=[
            pl.BlockSpec((rows, d), lambda i, j: (i, 0)),
            pl.BlockSpec((1, d), lambda i, j: (0, 0)),
            pl.BlockSpec((d // PACK, KV_COLS), lambda i, j: (0, j)),
        ],
        out_specs=pl.BlockSpec((rows // PACK, KV_COLS), lambda i, j: (i, j)),
        out_shape=jax.ShapeDtypeStruct((t // PACK, n_out), U32),
        scratch_shapes=[pltpu.VMEM((rows, d), BF16)],
        compiler_params=_params(("parallel", "arbitrary"), vmem),
        name="mem_kv",
    )(mem, norm_w, w_kv)


def _xattn_kernel(h_ref, nw_ref, kv_ref, wq_ref, wo_ref, next_nw_ref, o_ref, n_out_ref, q_scr, a_scr):
    d = h_ref.shape[1]
    hd = d // XATTN_HEADS
    scale = float(hd) ** -0.5
    q_scr[...] = _dot(_rms(h_ref[...], nw_ref[...]).astype(BF16), _unpack(wq_ref[...])).astype(BF16)
    for h in range(XATTN_HEADS):
        sl = slice(h * hd, (h + 1) * hd)
        s = _dot_nt(q_scr[:, sl], _unpack(kv_ref[:, sl])) * scale
        e = jnp.exp(s - jnp.max(s, axis=-1, keepdims=True))
        p = e / jnp.sum(e, axis=-1, keepdims=True)
        a_scr[:, sl] = _dot(p.astype(BF16), _unpack(kv_ref[:, d + h * hd:d + (h + 1) * hd])).astype(BF16)
    ss = _residual_proj(h_ref, a_scr[...], wo_ref, o_ref, 1.0)
    n_out_ref[...] = _pack(_rms_of_stored(o_ref, ss, next_nw_ref[...]))


def _xattn(h, kv, norm_w, w_q, w_o, next_norm_w, batch):
    t, d = h.shape
    n_mem = kv.shape[0] * PACK // batch
    row_blocks = t // batch // XATTN_ROWS
    assert t % (batch * XATTN_ROWS) == 0
    rows = lambda: pl.BlockSpec((XATTN_ROWS, d), lambda b, i: (b * row_blocks + i, 0))
    vec = lambda: pl.BlockSpec((1, d), lambda b, i: (0, 0))
    weight = lambda: _resident((d // PACK, d))
    vmem = (2 * _nbytes((d, d), BF16) + 4 * _nbytes((XATTN_ROWS, d), F32) + 4 * _nbytes((XATTN_ROWS, d), BF16)
            + 2 * _nbytes((n_mem, 2 * d), BF16) + 4 * _nbytes((XATTN_ROWS, d), F32))
    return pl.pallas_call(
        _xattn_kernel,
        grid=(batch, row_blocks),
        in_specs=[rows(), vec(), pl.BlockSpec((n_mem // PACK, 2 * d), lambda b, i: (b, 0)), weight(), weight(), vec()],
        out_specs=(rows(), pl.BlockSpec((XATTN_ROWS // PACK, d), lambda b, i: (b * row_blocks + i, 0))),
        out_shape=(jax.ShapeDtypeStruct((t, d), F32), jax.ShapeDtypeStruct((t // PACK, d), U32)),
        scratch_shapes=[pltpu.VMEM((XATTN_ROWS, d), BF16), pltpu.VMEM((XATTN_ROWS, d), BF16)],
        compiler_params=_params(("parallel", "parallel"), vmem),
        name="xattn",
    )(h, norm_w, kv, w_q, w_o, next_norm_w)


def _rope_tables(seq, half):
    freqs = ROPE_BASE ** (-jnp.linspace(0.0, 1.0, half, dtype=F32))
    ang = jnp.arange(seq, dtype=F32)[:, None] * freqs[None, :]
    return jnp.cos(ang), jnp.sin(ang)


def _trunk(x, mem, w):
    batch, seq, d = x.shape
    t = batch * seq
    x = x.reshape(t, d)
    half = d // RET_HEADS // 2
    cos, sin = _rope_tables(seq, half)
    pos_blocks = seq // IN_ROWS
    assert seq % IN_ROWS == 0
    pos_spec = lambda: pl.BlockSpec((IN_ROWS, half), lambda s, i: (i % pos_blocks, 0))

    hidden = _ffn_up(x, w["ffn1_norm"], w["ffn1_w_gu"], t)
    h, n = _ffn_down(x, hidden, w["ffn1_w_down"], w["mix_norm"], final_norm=False)

    w_in = w["w_in"]
    vec_spec = lambda: pl.BlockSpec((1, d), lambda s, i: (0, 0))
    q = _in_proj(n, w_in, SEG_Q, 1, "rotary", (cos, sin), (pos_spec(), pos_spec()))
    kt = _in_proj(n, w_in, SEG_K, 1, "rotary_transposed", (cos, sin), (pos_spec(), pos_spec()),
                  k_scale=float(2 * half) ** -0.5)
    v = _in_proj(n, w_in, SEG_V, 1, "plain")
    g = _in_proj(n, w_in, SEG_G, 1, "silu_gain", (w["ret_gn_w"],), (vec_spec(),))
    u = _in_proj(n, w_in, SEG_U, 1, "gelu")
    vs = _in_proj(n, w_in, SEG_VS, 1, "gelu_layernorm", (w["sgu_norm_w"],), (vec_spec(),), rows=IN_LN_ROWS)
    gates = _in_proj(n, w_in, SEG_GATE_R, 2, "sigmoid_bias", (w["gate_bias"],),
                     (pl.BlockSpec((2, d), lambda s, i: (0, 0)),))

    sb = _ret_bwd_states(kt, v, w["ret_decay_bwd"], batch)
    ret_act = _ret_out(q, kt, v, g, sb, w["ret_decay_fwd"], w["ret_decay_bwd"], batch)
    sgu_act = _sgu(u, vs, w["sgu_w_s"], w["sgu_b_s"])
    h = _out_proj(ret_act, sgu_act, gates, h, w["w_ret_out"], w["w_sgu_out"], w["w_out"])

    kv = _mem_kv(mem.reshape(-1, d), w["xattn_norm_mem"], w["xattn_w_kv"])
    h, n = _xattn(h, kv, w["xattn_norm_q"], w["xattn_w_q"], w["xattn_w_o"], w["ffn2_norm"], batch)

    hidden = _ffn_up(n, None, w["ffn2_w_gu"], t)
    y = _ffn_down(h, hidden, w["ffn2_w_down"], w["final_norm"], final_norm=True)
    return y.reshape(batch, seq, d)


def _prepare(ffn1_norm, ffn1_w_gu, ffn1_w_down, mix_norm, w_in, gate_bias, ret_decay_fwd, ret_decay_bwd, ret_gn_w, w_ret_out, sgu_norm_w, sgu_w_s, sgu_b_s, w_sgu_out, w_out, xattn_norm_q, xattn_norm_mem, xattn_w_q, xattn_w_kv, xattn_w_o, ffn2_norm, ffn2_w_gu, ffn2_w_down, final_norm):
    assert ffn1_norm.shape[0] == 1, "single-layer trunk"
    pw = lambda w: _pack_weight(w[0])
    return {
        "ffn1_norm": ffn1_norm, "ffn1_w_gu": _pack_gate_up(ffn1_w_gu[0]), "ffn1_w_down": pw(ffn1_w_down),
        "mix_norm": mix_norm, "w_in": pw(w_in), "gate_bias": gate_bias[0],
        "ret_decay_fwd": ret_decay_fwd[0], "ret_decay_bwd": ret_decay_bwd[0], "ret_gn_w": ret_gn_w,
        "w_ret_out": pw(w_ret_out), "sgu_norm_w": sgu_norm_w,
        "sgu_w_s": sgu_w_s[0].astype(BF16), "sgu_b_s": sgu_b_s[0][:, :, None],
        "w_sgu_out": pw(w_sgu_out), "w_out": pw(w_out),
        "xattn_norm_q": xattn_norm_q, "xattn_norm_mem": xattn_norm_mem,
        "xattn_w_q": pw(xattn_w_q), "xattn_w_kv": pw(xattn_w_kv), "xattn_w_o": pw(xattn_w_o),
        "ffn2_norm": ffn2_norm, "ffn2_w_gu": _pack_gate_up(ffn2_w_gu[0]), "ffn2_w_down": pw(ffn2_w_down),
        "final_norm": final_norm[None, :],
    }


def kernel(x_prompt, x_sample, mem_prompt, mem_sample, ffn1_norm, ffn1_w_gu, ffn1_w_down, mix_norm, w_in, gate_bias, ret_decay_fwd, ret_decay_bwd, ret_gn_w, w_ret_out, sgu_norm_w, sgu_w_s, sgu_b_s, w_sgu_out, w_out, xattn_norm_q, xattn_norm_mem, xattn_w_q, xattn_w_kv, xattn_w_o, ffn2_norm, ffn2_w_gu, ffn2_w_down, final_norm):
    w = _prepare(ffn1_norm, ffn1_w_gu, ffn1_w_down, mix_norm, w_in, gate_bias, ret_decay_fwd, ret_decay_bwd, ret_gn_w, w_ret_out, sgu_norm_w, sgu_w_s, sgu_b_s, w_sgu_out, w_out, xattn_norm_q, xattn_norm_mem, xattn_w_q, xattn_w_kv, xattn_w_o, ffn2_norm, ffn2_w_gu, ffn2_w_down, final_norm)
    return _trunk(x_prompt, mem_prompt, w), _trunk(x_sample, mem_sample, w)
```

```python
import functools
import math

import jax
import jax.numpy as jnp
from jax import lax
from jax.experimental import pallas as pl
from jax.experimental.pallas import tpu as pltpu

F32 = jnp.float32
BF16 = jnp.bfloat16
U32 = jnp.uint32
PACK = 2

EPS = 1e-6
FFN_RES_SCALE = 0.5
ROPE_BASE = 10000.0
RET_HEADS = 8
SGU_GROUPS = 8
SGU_CHUNK = 128
XATTN_HEADS = 4
SEG_Q, SEG_K, SEG_V, SEG_G, SEG_U, SEG_VS, SEG_GATE_R, SEG_GATE_S = range(8)

V7X_VMEM_BYTES = 64 * 1024 * 1024
V7X_MXU_COLS = 256

RET_BLOCK = 256
RET_PROJECT_LAG = 2
SGU_PROJECT_LAG = 1

PACK_ROWS = 512
FFN_UP_ROWS = 1024
FFN_UP_COLS = 1408
GATE_UP_LANES = 128
FFN_DOWN_ROWS = 512
IN_ROWS = 1024
IN_LN_ROWS = 512
OUT_ROWS = 256
KV_ROWS = 512
KV_COLS = 1024
XATTN_ROWS = 512
RESIDUAL_COL_TILE = 512
MATMUL_ROW_TILE = 512


def _nbytes(shape, dtype):
    return math.prod(shape) * jnp.dtype(dtype).itemsize


def _params(semantics, vmem_bytes):
    assert vmem_bytes <= V7X_VMEM_BYTES, vmem_bytes
    return pltpu.CompilerParams(dimension_semantics=semantics, vmem_limit_bytes=int(vmem_bytes))


def _resident(shape):
    return pl.BlockSpec(shape, lambda *_: (0,) * len(shape), pipeline_mode=pl.Buffered(1))


def _unpack(words):
    return pltpu.bitcast(words, BF16)


def _pack(vals):
    return pltpu.bitcast(vals.astype(BF16), U32)


def _packed(rows):
    return slice(rows.start // PACK, rows.stop // PACK)


def _tiles(n_rows, n_cols, row_tile, col_tile):
    for r in range(n_rows // row_tile):
        for c in range(n_cols // col_tile):
            yield slice(r * row_tile, (r + 1) * row_tile), slice(c * col_tile, (c + 1) * col_tile)


def _rms(x, w):
    return x * lax.rsqrt(jnp.mean(x * x, axis=-1, keepdims=True) + EPS) * w


def _dot(a, b):
    return jnp.dot(a, b, preferred_element_type=F32)


def _dot_nt(a, b):
    return lax.dot_general(a, b, (((1,), (1,)), ((), ())), preferred_element_type=F32)


def _sigmoid(x):
    return 0.5 * jnp.tanh(0.5 * x) + 0.5


def _gelu_tanh(x):
    c = math.sqrt(2.0 / math.pi)
    return x * (0.5 * jnp.tanh(x * (c + (c * 0.044715) * (x * x))) + 0.5)


def _residual_proj(x_ref, a, w_ref, o_ref, scale):
    rows, d = o_ref.shape
    ss = jnp.zeros((rows, 1), F32)
    for _, cs in _tiles(rows, d, rows, RESIDUAL_COL_TILE):
        h = x_ref[:, cs] + scale * _dot(a, _unpack(w_ref[:, cs]))
        o_ref[:, cs] = h
        ss += jnp.sum(h * h, axis=-1, keepdims=True)
    return ss


def _rms_of_stored(o_ref, ss, w):
    return o_ref[...] * lax.rsqrt(ss * (1.0 / o_ref.shape[1]) + EPS) * w


def _pack_kernel(w_ref, o_ref):
    o_ref[...] = _pack(w_ref[...])


def _pack_weight(w):
    k, n = w.shape
    rows = min(PACK_ROWS, k)
    cols = next(c for c in (2048, 1408, 1024, 512, 256, 128) if n % c == 0)
    assert k % rows == 0 and rows % (8 * PACK) == 0
    return pl.pallas_call(
        _pack_kernel,
        grid=(k // rows, n // cols),
        in_specs=[pl.BlockSpec((rows, cols), lambda i, j: (i, j))],
        out_specs=pl.BlockSpec((rows // PACK, cols), lambda i, j: (i, j)),
        out_shape=jax.ShapeDtypeStruct((k // PACK, n), U32),
        compiler_params=_params(("parallel", "parallel"), 4 * _nbytes((rows, cols), F32)),
        name="pack_weight",
    )(w)


def _pack_gate_up_kernel(g_ref, u_ref, o_ref):
    o_ref[:, :GATE_UP_LANES] = _pack(g_ref[...])
    o_ref[:, GATE_UP_LANES:] = _pack(u_ref[...])


def _pack_gate_up(w_gu):
    k, n = w_gu.shape
    n_blocks = n // 2 // GATE_UP_LANES
    assert n % (2 * GATE_UP_LANES) == 0
    return pl.pallas_call(
        _pack_gate_up_kernel,
        grid=(n_blocks,),
        in_specs=[pl.BlockSpec((k, GATE_UP_LANES), lambda j: (0, j)),
                  pl.BlockSpec((k, GATE_UP_LANES), lambda j: (0, j + n_blocks))],
        out_specs=pl.BlockSpec((k // PACK, 2 * GATE_UP_LANES), lambda j: (0, j)),
        out_shape=jax.ShapeDtypeStruct((k // PACK, n), U32),
        compiler_params=_params(("parallel",), 8 * _nbytes((k, GATE_UP_LANES), F32)),
        name="pack_gate_up",
    )(w_gu, w_gu)


def _ffn_up_kernel(*refs, norm_input):
    if norm_input:
        x_ref, nw_ref, w_ref, o_ref, n_scr = refs

        @pl.when(pl.program_id(1) == 0)
        def _():
            n_scr[...] = _rms(x_ref[...], nw_ref[...]).astype(BF16)

        lhs = lambda rs: n_scr[rs, :]
    else:
        n_ref, w_ref, o_ref = refs
        lhs = lambda rs: _unpack(n_ref[_packed(rs), :])
    for rs, cs in _tiles(o_ref.shape[0] * PACK, w_ref.shape[1], MATMUL_ROW_TILE, 2 * GATE_UP_LANES):
        z = _dot(lhs(rs), _unpack(w_ref[:, cs]))
        g, u = z[:, :GATE_UP_LANES], z[:, GATE_UP_LANES:]
        o_ref[_packed(rs), cs.start // 2:cs.stop // 2] = _pack(g * _sigmoid(g) * u)


def _ffn_up(x_or_n, norm_w, w_gu, t):
    d = x_or_n.shape[1]
    d_ff = w_gu.shape[1] // 2
    assert t % FFN_UP_ROWS == 0 and d_ff % FFN_UP_COLS == 0 and FFN_UP_COLS % GATE_UP_LANES == 0
    norm_input = norm_w is not None
    w_spec = pl.BlockSpec((d // PACK, 2 * FFN_UP_COLS), lambda i, j: (0, j))
    if norm_input:
        in_specs = [pl.BlockSpec((FFN_UP_ROWS, d), lambda i, j: (i, 0)), pl.BlockSpec((1, d), lambda i, j: (0, 0)), w_spec]
        args = (x_or_n, norm_w, w_gu)
        scratch = [pltpu.VMEM((FFN_UP_ROWS, d), BF16)]
    else:
        in_specs = [pl.BlockSpec((FFN_UP_ROWS // PACK, d), lambda i, j: (i, 0)), w_spec]
        args = (x_or_n, w_gu)
        scratch = []
    vmem = (2 * _nbytes((FFN_UP_ROWS, d), F32 if norm_input else BF16) + _nbytes((FFN_UP_ROWS, d), BF16)
            + 4 * _nbytes((d, FFN_UP_COLS), BF16) + 2 * _nbytes((FFN_UP_ROWS, FFN_UP_COLS), BF16)
            + 8 * _nbytes((MATMUL_ROW_TILE, 2 * GATE_UP_LANES), F32))
    return pl.pallas_call(
        functools.partial(_ffn_up_kernel, norm_input=norm_input),
        grid=(t // FFN_UP_ROWS, d_ff // FFN_UP_COLS),
        in_specs=in_specs,
        out_specs=pl.BlockSpec((FFN_UP_ROWS // PACK, FFN_UP_COLS), lambda i, j: (i, j)),
        out_shape=jax.ShapeDtypeStruct((t // PACK, d_ff), U32),
        scratch_shapes=scratch,
        compiler_params=_params(("parallel", "arbitrary"), vmem),
        name="ffn_up_norm" if norm_input else "ffn_up",
    )(*args)


def _ffn_down_kernel(x_ref, hid_ref, wd_ref, nw_ref, o_ref, *n_out, final_norm):
    ss = _residual_proj(x_ref, _unpack(hid_ref[...]), wd_ref, o_ref, FFN_RES_SCALE)
    normed = _rms_of_stored(o_ref, ss, nw_ref[...])
    if final_norm:
        o_ref[...] = normed
    else:
        n_out[0][...] = _pack(normed)


def _ffn_down(x, hidden, w_down, norm_w, final_norm):
    t, d = x.shape
    d_ff = hidden.shape[1]
    assert t % FFN_DOWN_ROWS == 0
    row_spec = lambda: pl.BlockSpec((FFN_DOWN_ROWS, d), lambda i: (i, 0))
    packed_spec = lambda cols: pl.BlockSpec((FFN_DOWN_ROWS // PACK, cols), lambda i: (i, 0))
    vmem = (4 * _nbytes((FFN_DOWN_ROWS, d), F32) + 2 * _nbytes((FFN_DOWN_ROWS, d_ff), BF16)
            + _nbytes((d_ff, d), BF16) + 2 * _nbytes((FFN_DOWN_ROWS, d), BF16)
            + 4 * _nbytes((FFN_DOWN_ROWS, RESIDUAL_COL_TILE), F32))
    out_f32 = jax.ShapeDtypeStruct((t, d), F32)
    return pl.pallas_call(
        functools.partial(_ffn_down_kernel, final_norm=final_norm),
        grid=(t // FFN_DOWN_ROWS,),
        in_specs=[row_spec(), packed_spec(d_ff), _resident((d_ff // PACK, d)), pl.BlockSpec((1, d), lambda i: (0, 0))],
        out_specs=row_spec() if final_norm else (row_spec(), packed_spec(d)),
        out_shape=out_f32 if final_norm else (out_f32, jax.ShapeDtypeStruct((t // PACK, d), U32)),
        compiler_params=_params(("parallel",), vmem),
        name="ffn_down_final" if final_norm else "ffn_down",
    )(x, hidden, w_down, norm_w)


def _in_proj_kernel(n_ref, w_ref, *refs, kind, k_scale):
    seg = pl.program_id(0)
    rows, d = n_ref.shape[0] * PACK, n_ref.shape[1]
    o_ref = refs[1] if kind == "gelu_layernorm" else refs[-1]

    def z_tiles():
        for rs, cs in _tiles(rows, d, MATMUL_ROW_TILE, V7X_MXU_COLS):
            yield rs, cs, _dot(_unpack(n_ref[_packed(rs), :]), _unpack(w_ref[:, cs]))

    def store(rs, cs, val):
        o_ref[0, _packed(rs), cs] = _pack(val)

    if kind in ("rotary", "rotary_transposed"):
        cos_ref, sin_ref, _ = refs
        half = V7X_MXU_COLS // 2
        for rs, cs, z in z_tiles():
            x1, x2 = z[:, :half], z[:, half:]
            cos, sin = cos_ref[rs, :], sin_ref[rs, :]
            if kind == "rotary":
                store(rs, slice(cs.start, cs.start + half), x1 * cos - x2 * sin)
                store(rs, slice(cs.start + half, cs.stop), x1 * sin + x2 * cos)
            else:
                rot = jnp.concatenate([x1 * cos - x2 * sin, x1 * sin + x2 * cos], axis=1) * k_scale
                rot_t = _pack(rot.T)
                for j in range(MATMUL_ROW_TILE // RET_BLOCK):
                    o_ref[rs.start // RET_BLOCK + j, cs.start // V7X_MXU_COLS] = rot_t[:, j * RET_BLOCK:(j + 1) * RET_BLOCK]
    elif kind == "plain":
        for rs, cs, z in z_tiles():
            store(rs, cs, z)
    elif kind == "silu_gain":
        gain_ref, _ = refs
        for rs, cs, z in z_tiles():
            store(rs, cs, z * _sigmoid(z) * gain_ref[:, cs])
    elif kind == "gelu":
        for rs, cs, z in z_tiles():
            store(rs, cs, _gelu_tanh(z))
    elif kind == "gelu_layernorm":
        lnw_ref, _, a_scr = refs
        for rs, cs, z in z_tiles():
            a_scr[rs, cs] = _gelu_tanh(z)
        a = a_scr[...]
        cen = a - jnp.mean(a, axis=-1, keepdims=True)
        var = jnp.mean(cen * cen, axis=-1, keepdims=True)
        o_ref[0] = _pack(cen * lax.rsqrt(var + EPS) * lnw_ref[...])
    elif kind == "sigmoid_bias":
        gb_ref, _ = refs
        for rs, cs, z in z_tiles():
            store(rs, cs, _sigmoid(z + gb_ref[pl.ds(seg, 1), cs]))
    else:
        raise ValueError(kind)


def _in_proj(n, w_in, first_seg, n_seg, kind, extra=(), extra_specs=(), rows=IN_ROWS, k_scale=None):
    t, d = n.shape[0] * PACK, n.shape[1]
    assert t % rows == 0 and w_in.shape[1] % d == 0
    scratch = [pltpu.VMEM((rows, d), F32)] if kind == "gelu_layernorm" else []
    vmem = (4 * _nbytes((rows, d), BF16) + 2 * _nbytes((d, d), BF16) + 3 * _nbytes((rows, d), F32))
    if kind == "rotary_transposed":
        assert n_seg == 1
        head = d // RET_HEADS
        out_spec = pl.BlockSpec((rows // RET_BLOCK, RET_HEADS, head // PACK, RET_BLOCK), lambda s, i: (i, 0, 0, 0))
        out_shape = jax.ShapeDtypeStruct((t // RET_BLOCK, RET_HEADS, head // PACK, RET_BLOCK), U32)
    else:
        out_spec = pl.BlockSpec((1, rows // PACK, d), lambda s, i: (s, i, 0))
        out_shape = jax.ShapeDtypeStruct((n_seg, t // PACK, d), U32)
    return pl.pallas_call(
        functools.partial(_in_proj_kernel, kind=kind, k_scale=k_scale),
        grid=(n_seg, t // rows),
        in_specs=[pl.BlockSpec((rows // PACK, d), lambda s, i: (i, 0)),
                  pl.BlockSpec((d // PACK, d), lambda s, i: (0, first_seg + s))] + list(extra_specs),
        out_specs=out_spec,
        out_shape=out_shape,
        scratch_shapes=scratch,
        compiler_params=_params(("parallel", "parallel"), vmem),
        name="in_proj_" + kind,
    )(n, w_in, *extra)


def _decay_rate(dec_ref, h, shape):
    return -jnp.exp(jnp.full(shape, dec_ref[h], F32))


def _ret_bwd_state_kernel(decb_ref, kt_ref, v_ref, sb_ref, state, kdec, cdec):
    c = pl.program_id(1)
    blk = kt_ref.shape[3]
    head = v_ref.shape[2] // RET_HEADS

    @pl.when((pl.program_id(0) == 0) & (c == 0))
    def _():
        col = lax.broadcasted_iota(jnp.int32, (1, blk), 1).astype(F32)
        for h in range(RET_HEADS):
            kdec[h] = jnp.exp(_decay_rate(decb_ref, h, (1, blk)) * col)
            cdec[h] = jnp.exp(_decay_rate(decb_ref, h, (1, head)) * float(blk))

    @pl.when(c == 0)
    def _():
        state[...] = jnp.zeros_like(state)

    for h in range(RET_HEADS):
        sl = slice(h * head, (h + 1) * head)
        st = state[h]
        sb_ref[0, 0, h] = _pack(st)
        kd = (_unpack(kt_ref[0, h]).astype(F32) * kdec[h]).astype(BF16)
        state[h] = st * cdec[h] + _dot(kd, _unpack(v_ref[0, :, sl]))


def _ret_bwd_states(kt, v, dec_b, batch):
    _, t_packed, d = v.shape
    n_blk = t_packed * PACK // batch // RET_BLOCK
    head = d // RET_HEADS
    blk_bytes = _nbytes((RET_BLOCK, d), BF16)
    state_bytes = _nbytes((RET_HEADS, head, head), F32)
    return pl.pallas_call(
        _ret_bwd_state_kernel,
        grid=(batch, n_blk),
        in_specs=[
            pl.BlockSpec(memory_space=pltpu.SMEM),
            pl.BlockSpec((1, RET_HEADS, head // PACK, RET_BLOCK), lambda b, c: (b * n_blk + (n_blk - 1 - c), 0, 0, 0)),
            pl.BlockSpec((1, RET_BLOCK // PACK, d), lambda b, c: (0, b * n_blk + (n_blk - 1 - c), 0)),
        ],
        out_specs=pl.BlockSpec((1, 1, RET_HEADS, head // PACK, head), lambda b, c: (b, n_blk - 1 - c, 0, 0, 0)),
        out_shape=jax.ShapeDtypeStruct((batch, n_blk, RET_HEADS, head // PACK, head), U32),
        scratch_shapes=[
            pltpu.VMEM((RET_HEADS, head, head), F32),
            pltpu.VMEM((RET_HEADS, 1, RET_BLOCK), F32),
            pltpu.VMEM((RET_HEADS, 1, head), F32),
        ],
        compiler_params=_params(("arbitrary", "arbitrary"), 6 * blk_bytes + 4 * state_bytes),
        name="ret_bwd_states",
    )(dec_b, kt, v)


def _project_slice(act_ref, w_ref, acc_ref, i):
    width = act_ref.shape[2]
    w_rows = slice(i * width // PACK, (i + 1) * width // PACK)
    for _, cs in _tiles(1, acc_ref.shape[1], 1, RESIDUAL_COL_TILE):
        part = _dot(act_ref[i], _unpack(w_ref[w_rows, cs]))
        acc_ref[:, cs] = part if i == 0 else acc_ref[:, cs] + part


def _ret_out_kernel(decf_ref, decb_ref, q_ref, kt_ref, v_ref, g_ref, sb_ref, wr_ref, o_ref,
                    state, dmat, qdf, qdb, kdf, cdf, act, acc):
    c = pl.program_id(1)
    blk = q_ref.shape[1] * PACK
    head = q_ref.shape[2] // RET_HEADS

    @pl.when((pl.program_id(0) == 0) & (c == 0))
    def _():
        row = lax.broadcasted_iota(jnp.int32, (blk, head), 0).astype(F32)
        col = lax.broadcasted_iota(jnp.int32, (1, blk), 1).astype(F32)
        ri = lax.broadcasted_iota(jnp.int32, (blk, blk), 0)
        ci = lax.broadcasted_iota(jnp.int32, (blk, blk), 1)
        dist = (ri - ci).astype(F32)
        for h in range(RET_HEADS):
            qdf[h] = jnp.exp(_decay_rate(decf_ref, h, (blk, head)) * (row + 1.0))
            qdb[h] = jnp.exp(_decay_rate(decb_ref, h, (blk, head)) * (float(blk) - row))
            kdf[h] = jnp.exp(_decay_rate(decf_ref, h, (1, blk)) * (float(blk) - 1.0 - col))
            cdf[h] = jnp.exp(_decay_rate(decf_ref, h, (1, head)) * float(blk))
            lf2 = _decay_rate(decf_ref, h, (blk, blk))
            lb2 = _decay_rate(decb_ref, h, (blk, blk))
            dmat[h] = jnp.where(ri >= ci, jnp.exp(lf2 * jnp.maximum(dist, 0.0)), jnp.exp(lb2 * jnp.maximum(-dist, 0.0)))

    @pl.when(c == 0)
    def _():
        state[...] = jnp.zeros_like(state)

    for h in range(RET_HEADS):
        sl = slice(h * head, (h + 1) * head)
        q = _unpack(q_ref[0, :, sl])
        kt = _unpack(kt_ref[0, h])
        v = _unpack(v_ref[0, :, sl])
        st = state[h]
        p = (_dot(q, kt) * dmat[h]).astype(BF16)
        o = _dot(p, v) + _dot(q, st.astype(BF16)) * qdf[h] + _dot(q, _unpack(sb_ref[0, 0, h])) * qdb[h]
        cen = o - jnp.mean(o, axis=-1, keepdims=True)
        var = jnp.mean(cen * cen, axis=-1, keepdims=True)
        act[h] = (_unpack(g_ref[0, :, sl]).astype(F32) * (cen * lax.rsqrt(var + EPS))).astype(BF16)
        kd = (kt.astype(F32) * kdf[h]).astype(BF16)
        state[h] = st * cdf[h] + _dot(kd, v)
        if h >= RET_PROJECT_LAG:
            _project_slice(act, wr_ref, acc, h - RET_PROJECT_LAG)
    for h in range(RET_HEADS - RET_PROJECT_LAG, RET_HEADS):
        _project_slice(act, wr_ref, acc, h)
    o_ref[...] = _pack(acc[...])


def _ret_out(q, kt, v, g, sb, dec_f, dec_b, w_ret_out, batch):
    _, t_packed, d = q.shape
    n_blk = t_packed * PACK // batch // RET_BLOCK
    head = d // RET_HEADS
    chunk = lambda: pl.BlockSpec((1, RET_BLOCK // PACK, d), lambda b, c: (0, b * n_blk + c, 0))
    blk_bytes = _nbytes((RET_BLOCK, d), BF16)
    table_bytes = _nbytes((RET_HEADS, RET_BLOCK, head), F32)
    return pl.pallas_call(
        _ret_out_kernel,
        grid=(batch, n_blk),
        in_specs=[
            pl.BlockSpec(memory_space=pltpu.SMEM),
            pl.BlockSpec(memory_space=pltpu.SMEM),
            chunk(),
            pl.BlockSpec((1, RET_HEADS, head // PACK, RET_BLOCK), lambda b, c: (b * n_blk + c, 0, 0, 0)),
            chunk(), chunk(),
            pl.BlockSpec((1, 1, RET_HEADS, head // PACK, head), lambda b, c: (b, c, 0, 0, 0)),
            _resident((d // PACK, d)),
        ],
        out_specs=pl.BlockSpec((RET_BLOCK // PACK, d), lambda b, c: (b * n_blk + c, 0)),
        out_shape=jax.ShapeDtypeStruct((t_packed, d), U32),
        scratch_shapes=[
            pltpu.VMEM((RET_HEADS, head, head), F32),
            pltpu.VMEM((RET_HEADS, RET_BLOCK, RET_BLOCK), F32),
            pltpu.VMEM((RET_HEADS, RET_BLOCK, head), F32),
            pltpu.VMEM((RET_HEADS, RET_BLOCK, head), F32),
            pltpu.VMEM((RET_HEADS, 1, RET_BLOCK), F32),
            pltpu.VMEM((RET_HEADS, 1, head), F32),
            pltpu.VMEM((RET_HEADS, RET_BLOCK, head), BF16),
            pltpu.VMEM((RET_BLOCK, d), F32),
        ],
        compiler_params=_params(("arbitrary", "arbitrary"),
                                12 * blk_bytes + 6 * table_bytes + _nbytes((d, d), BF16) + 4 * _nbytes((RET_BLOCK, d), F32)),
        name="ret_out",
    )(dec_f, dec_b, q, kt, v, g, sb, w_ret_out)


def _mix_out_kernel(u_ref, vs_ref, ws_ref, bs_ref, ret_ref, gr_ref, gs_ref, h_ref, wsg_ref, wo_ref, o_ref, act, acc):
    rows, d = h_ref.shape
    gdim = d // SGU_GROUPS
    for g in range(SGU_GROUPS):
        cs = slice(g * gdim, (g + 1) * gdim)
        for rs, _ in _tiles(rows, 1, SGU_CHUNK, 1):
            mixed = _dot(ws_ref[g], _unpack(vs_ref[0, _packed(rs), cs])) + bs_ref[g]
            act[g, rs, :] = (_unpack(u_ref[0, _packed(rs), cs]).astype(F32) * mixed).astype(BF16)
        if g >= SGU_PROJECT_LAG:
            _project_slice(act, wsg_ref, acc, g - SGU_PROJECT_LAG)
    for g in range(SGU_GROUPS - SGU_PROJECT_LAG, SGU_GROUPS):
        _project_slice(act, wsg_ref, acc, g)
    merged = _unpack(gr_ref[0]).astype(F32) * _unpack(ret_ref[...]).astype(F32) + _unpack(gs_ref[0]).astype(F32) * acc[...]
    _residual_proj(h_ref, merged.astype(BF16), wo_ref, o_ref, 1.0)


def _mix_out(u, vs, w_s, b_s, ret_proj, gates, h, w_sgu_out, w_out):
    t, d = h.shape
    gdim = d // SGU_GROUPS
    assert t % OUT_ROWS == 0 and OUT_ROWS % SGU_CHUNK == 0
    rows = lambda: pl.BlockSpec((OUT_ROWS, d), lambda i: (i, 0))
    packed_rows = lambda: pl.BlockSpec((OUT_ROWS // PACK, d), lambda i: (i, 0))
    seg_rows = lambda s: pl.BlockSpec((1, OUT_ROWS // PACK, d), lambda i: (s, i, 0))
    weight = lambda: _resident((d // PACK, d))
    vmem = (2 * _nbytes((d, d), BF16) + 10 * _nbytes((OUT_ROWS, d), BF16) + 4 * _nbytes((OUT_ROWS, d), F32)
            + _nbytes((OUT_ROWS, d), BF16) + 4 * _nbytes((OUT_ROWS, d), F32))
    return pl.pallas_call(
        _mix_out_kernel,
        grid=(t // OUT_ROWS,),
        in_specs=[
            seg_rows(0), seg_rows(0),
            pl.BlockSpec((SGU_GROUPS, SGU_CHUNK, SGU_CHUNK), lambda i: (0, 0, 0)),
            pl.BlockSpec((SGU_GROUPS, SGU_CHUNK, 1), lambda i: (0, 0, 0)),
            packed_rows(), seg_rows(0), seg_rows(1), rows(), weight(), weight(),
        ],
        out_specs=rows(),
        out_shape=jax.ShapeDtypeStruct((t, d), F32),
        scratch_shapes=[pltpu.VMEM((SGU_GROUPS, OUT_ROWS, gdim), BF16), pltpu.VMEM((OUT_ROWS, d), F32)],
        compiler_params=_params(("parallel",), vmem),
        name="mix_out",
    )(u, vs, w_s, b_s, ret_proj, gates, gates, h, w_sgu_out, w_out)


def _mem_kv_kernel(m_ref, nw_ref, w_ref, o_ref, n_scr):
    @pl.when(pl.program_id(1) == 0)
    def _():
        n_scr[...] = _rms(m_ref[...], nw_ref[...]).astype(BF16)

    o_ref[...] = _pack(_dot(n_scr[...], _unpack(w_ref[...])))


def _mem_kv(mem, norm_w, w_kv):
    t, d = mem.shape
    rows = min(KV_ROWS, t)
    n_out = w_kv.shape[1]
    assert t % rows == 0 and n_out % KV_COLS == 0
    vmem = (2 * _nbytes((rows, d), F32) + _nbytes((rows, d), BF16) + 2 * _nbytes((d, KV_COLS), BF16)
            + 2 * _nbytes((rows, KV_COLS), BF16) + 2 * _nbytes((rows, d), F32))
    return pl.pallas_call(
        _mem_kv_kernel,
        grid=(t // rows, n_out // KV_COLS),
        in_specs=[
            pl.BlockSpec((rows, d), lambda i, j: (i, 0)),
            pl.BlockSpec((1, d), lambda i, j: (0, 0)),
            pl.BlockSpec((d // PACK, KV_COLS), lambda i, j: (0, j)),
        ],
        out_specs=pl.BlockSpec((rows // PACK, KV_COLS), lambda i, j: (i, j)),
        out_shape=jax.ShapeDtypeStruct((t // PACK, n_out), U32),
        scratch_shapes=[pltpu.VMEM((rows, d), BF16)],
        compiler_params=_params(("parallel", "arbitrary"), vmem),
        name="mem_kv",
    )(mem, norm_w, w_kv)


def _xattn_kernel(h_ref, nw_ref, kv_ref, wq_ref, wo_ref, next_nw_ref, o_ref, n_out_ref, q_scr, a_scr):
    d = h_ref.shape[1]
    hd = d // XATTN_HEADS
    scale = float(hd) ** -0.5
    q_scr[...] = _dot(_rms(h_ref[...], nw_ref[...]).astype(BF16), _unpack(wq_ref[...])).astype(BF16)
    for h in range(XATTN_HEADS):
        sl = slice(h * hd, (h + 1) * hd)
        s = _dot_nt(q_scr[:, sl], _unpack(kv_ref[:, sl])) * scale
        e = jnp.exp(s - jnp.max(s, axis=-1, keepdims=True))
        p = e / jnp.sum(e, axis=-1, keepdims=True)
        a_scr[:, sl] = _dot(p.astype(BF16), _unpack(kv_ref[:, d + h * hd:d + (h + 1) * hd])).astype(BF16)
    ss = _residual_proj(h_ref, a_scr[...], wo_ref, o_ref, 1.0)
    n_out_ref[...] = _pack(_rms_of_stored(o_ref, ss, next_nw_ref[...]))


def _xattn(h, kv, norm_w, w_q, w_o, next_norm_w, batch):
    t, d = h.shape
    n_mem = kv.shape[0] * PACK // batch
    row_blocks = t // batch // XATTN_ROWS
    assert t % (batch * XATTN_ROWS) == 0
    rows = lambda: pl.BlockSpec((XATTN_ROWS, d), lambda b, i: (b * row_blocks + i, 0))
    vec = lambda: pl.BlockSpec((1, d), lambda b, i: (0, 0))
    weight = lambda: _resident((d // PACK, d))
    vmem = (2 * _nbytes((d, d), BF16) + 4 * _nbytes((XATTN_ROWS, d), F32) + 4 * _nbytes((XATTN_ROWS, d), BF16)
            + 2 * _nbytes((n_mem, 2 * d), BF16) + 4 * _nbytes((XATTN_ROWS, d), F32))
    return pl.pallas_call(
        _xattn_kernel,
        grid=(batch, row_blocks),
        in_specs=[rows(), vec(), pl.BlockSpec((n_mem // PACK, 2 * d), lambda b, i: (b, 0)), weight(), weight(), vec()],
        out_specs=(rows(), pl.BlockSpec((XATTN_ROWS // PACK, d), lambda b, i: (b * row_blocks + i, 0))),
        out_shape=(jax.ShapeDtypeStruct((t, d), F32), jax.ShapeDtypeStruct((t // PACK, d), U32)),
        scratch_shapes=[pltpu.VMEM((XATTN_ROWS, d), BF16), pltpu.VMEM((XATTN_ROWS, d), BF16)],
        compiler_params=_params(("parallel", "parallel"), vmem),
        name="xattn",
    )(h, norm_w, kv, w_q, w_o, next_norm_w)


def _rope_tables(seq, half):
    freqs = ROPE_BASE ** (-jnp.linspace(0.0, 1.0, half, dtype=F32))
    ang = jnp.arange(seq, dtype=F32)[:, None] * freqs[None, :]
    return jnp.cos(ang), jnp.sin(ang)


def _trunk(x, mem, w):
    batch, seq, d = x.shape
    t = batch * seq
    x = x.reshape(t, d)
    half = d // RET_HEADS // 2
    cos, sin = _rope_tables(seq, half)
    pos_blocks = seq // IN_ROWS
    assert seq % IN_ROWS == 0
    pos_spec = lambda: pl.BlockSpec((IN_ROWS, half), lambda s, i: (i % pos_blocks, 0))

    hidden = _ffn_up(x, w["ffn1_norm"], w["ffn1_w_gu"], t)
    h, n = _ffn_down(x, hidden, w["ffn1_w_down"], w["mix_norm"], final_norm=False)

    w_in = w["w_in"]
    vec_spec = lambda: pl.BlockSpec((1, d), lambda s, i: (0, 0))
    q = _in_proj(n, w_in, SEG_Q, 1, "rotary", (cos, sin), (pos_spec(), pos_spec()))
    kt = _in_proj(n, w_in, SEG_K, 1, "rotary_transposed", (cos, sin), (pos_spec(), pos_spec()),
                  k_scale=float(2 * half) ** -0.5)
    v = _in_proj(n, w_in, SEG_V, 1, "plain")
    g = _in_proj(n, w_in, SEG_G, 1, "silu_gain", (w["ret_gn_w"],), (vec_spec(),))
    u = _in_proj(n, w_in, SEG_U, 1, "gelu")
    vs = _in_proj(n, w_in, SEG_VS, 1, "gelu_layernorm", (w["sgu_norm_w"],), (vec_spec(),), rows=IN_LN_ROWS)
    gates = _in_proj(n, w_in, SEG_GATE_R, 2, "sigmoid_bias", (w["gate_bias"],),
                     (pl.BlockSpec((2, d), lambda s, i: (0, 0)),))

    sb = _ret_bwd_states(kt, v, w["ret_decay_bwd"], batch)
    ret_proj = _ret_out(q, kt, v, g, sb, w["ret_decay_fwd"], w["ret_decay_bwd"], w["w_ret_out"], batch)
    h = _mix_out(u, vs, w["sgu_w_s"], w["sgu_b_s"], ret_proj, gates, h, w["w_sgu_out"], w["w_out"])

    kv = _mem_kv(mem.reshape(-1, d), w["xattn_norm_mem"], w["xattn_w_kv"])
    h, n = _xattn(h, kv, w["xattn_norm_q"], w["xattn_w_q"], w["xattn_w_o"], w["ffn2_norm"], batch)

    hidden = _ffn_up(n, None, w["ffn2_w_gu"], t)
    y = _ffn_down(h, hidden, w["ffn2_w_down"], w["final_norm"], final_norm=True)
    return y.reshape(batch, seq, d)


def _prepare(ffn1_norm, ffn1_w_gu, ffn1_w_down, mix_norm, w_in, gate_bias, ret_decay_fwd, ret_decay_bwd, ret_gn_w, w_ret_out, sgu_norm_w, sgu_w_s, sgu_b_s, w_sgu_out, w_out, xattn_norm_q, xattn_norm_mem, xattn_w_q, xattn_w_kv, xattn_w_o, ffn2_norm, ffn2_w_gu, ffn2_w_down, final_norm):
    assert ffn1_norm.shape[0] == 1, "single-layer trunk"
    pw = lambda w: _pack_weight(w[0])
    return {
        "ffn1_norm": ffn1_norm, "ffn1_w_gu": _pack_gate_up(ffn1_w_gu[0]), "ffn1_w_down": pw(ffn1_w_down),
        "mix_norm": mix_norm, "w_in": pw(w_in), "gate_bias": gate_bias[0],
        "ret_decay_fwd": ret_decay_fwd[0], "ret_decay_bwd": ret_decay_bwd[0], "ret_gn_w": ret_gn_w,
        "w_ret_out": pw(w_ret_out), "sgu_norm_w": sgu_norm_w,
        "sgu_w_s": sgu_w_s[0].astype(BF16), "sgu_b_s": sgu_b_s[0][:, :, None],
        "w_sgu_out": pw(w_sgu_out), "w_out": pw(w_out),
        "xattn_norm_q": xattn_norm_q, "xattn_norm_mem": xattn_norm_mem,
        "xattn_w_q": pw(xattn_w_q), "xattn_w_kv": pw(xattn_w_kv), "xattn_w_o": pw(xattn_w_o),
        "ffn2_norm": ffn2_norm, "ffn2_w_gu": _pack_gate_up(ffn2_w_gu[0]), "ffn2_w_down": pw(ffn2_w_down),
        "final_norm": final_norm[None, :],
    }


def kernel(x_prompt, x_sample, mem_prompt, mem_sample, ffn1_norm, ffn1_w_gu, ffn1_w_down, mix_norm, w_in, gate_bias, ret_decay_fwd, ret_decay_bwd, ret_gn_w, w_ret_out, sgu_norm_w, sgu_w_s, sgu_b_s, w_sgu_out, w_out, xattn_norm_q, xattn_norm_mem, xattn_w_q, xattn_w_kv, xattn_w_o, ffn2_norm, ffn2_w_gu, ffn2_w_down, final_norm):
    w = _prepare(ffn1_norm, ffn1_w_gu, ffn1_w_down, mix_norm, w_in, gate_bias, ret_decay_fwd, ret_decay_bwd, ret_gn_w, w_ret_out, sgu_norm_w, sgu_w_s, sgu_b_s, w_sgu_out, w_out, xattn_norm_q, xattn_norm_mem, xattn_w_q, xattn_w_kv, xattn_w_o, ffn2_norm, ffn2_w_gu, ffn2_w_down, final_norm)
    return _trunk(x_prompt, mem_prompt, w), _trunk(x_sample, mem_sample, w)
```

```python
import functools
import math

import jax
import jax.numpy as jnp
from jax import lax
from jax.experimental import pallas as pl
from jax.experimental.pallas import tpu as pltpu

F32 = jnp.float32
BF16 = jnp.bfloat16
U32 = jnp.uint32
PACK = 2

EPS = 1e-6
FFN_RES_SCALE = 0.5
ROPE_BASE = 10000.0
RET_HEADS = 8
SGU_GROUPS = 8
SGU_CHUNK = 128
XATTN_HEADS = 4
SEG_Q, SEG_K, SEG_V, SEG_G, SEG_U, SEG_VS, SEG_GATE_R, SEG_GATE_S = range(8)

V7X_VMEM_BYTES = 64 * 1024 * 1024
V7X_MXU_COLS = 256

RET_BLOCK = 256
RET_PROJECT_LAG = 2
SGU_PROJECT_LAG = 1

PACK_ROWS = 512
FFN_UP_ROWS = 1024
FFN_UP_COLS = 1408
GATE_UP_LANES = 128
FFN_DOWN_ROWS = 512
IN_ROWS = 2048
IN_LN_ROWS = 1024
LN_ROW_TILE = 256
OUT_ROWS = 256
KV_ROWS = 512
KV_COLS = 1024
XATTN_ROWS = 512
RESIDUAL_COL_TILE = 512
MATMUL_ROW_TILE = 512


def _nbytes(shape, dtype):
    return math.prod(shape) * jnp.dtype(dtype).itemsize


def _params(semantics, vmem_bytes):
    assert vmem_bytes <= V7X_VMEM_BYTES, vmem_bytes
    return pltpu.CompilerParams(dimension_semantics=semantics, vmem_limit_bytes=int(vmem_bytes))


def _resident(shape):
    return pl.BlockSpec(shape, lambda *_: (0,) * len(shape), pipeline_mode=pl.Buffered(1))


def _unpack(words):
    return pltpu.bitcast(words, BF16)


def _pack(vals):
    return pltpu.bitcast(vals.astype(BF16), U32)


def _packed(rows):
    return slice(rows.start // PACK, rows.stop // PACK)


def _tiles(n_rows, n_cols, row_tile, col_tile):
    for r in range(n_rows // row_tile):
        for c in range(n_cols // col_tile):
            yield slice(r * row_tile, (r + 1) * row_tile), slice(c * col_tile, (c + 1) * col_tile)


def _rms(x, w):
    return x * lax.rsqrt(jnp.mean(x * x, axis=-1, keepdims=True) + EPS) * w


def _dot(a, b):
    return jnp.dot(a, b, preferred_element_type=F32)


def _dot_nt(a, b):
    return lax.dot_general(a, b, (((1,), (1,)), ((), ())), preferred_element_type=F32)


def _sigmoid(x):
    return 0.5 * jnp.tanh(0.5 * x) + 0.5


def _gelu_tanh(x):
    c = math.sqrt(2.0 / math.pi)
    return x * (0.5 * jnp.tanh(x * (c + (c * 0.044715) * (x * x))) + 0.5)


def _residual_proj(x_ref, a, w_ref, o_ref):
    for _, cs in _tiles(1, o_ref.shape[1], 1, RESIDUAL_COL_TILE):
        o_ref[:, cs] = x_ref[:, cs] + _dot(a, _unpack(w_ref[:, cs]))


def _residual_proj_norm(x_ref, lhs, w_ref, o_ref, scale, norm_w, emit, w_rows=slice(None), base_ref=None):
    rows, d = o_ref.shape
    for rs, _ in _tiles(rows, 1, rows // 2, 1):
        a = lhs(rs)
        ss = jnp.zeros((a.shape[0], 1), F32)
        for _, cs in _tiles(1, d, 1, RESIDUAL_COL_TILE):
            y = _dot(a, _unpack(w_ref[w_rows, cs]))
            if base_ref is not None:
                y = base_ref[rs, cs] + y
            o = x_ref[rs, cs] + (y if scale is None else scale * y)
            o_ref[rs, cs] = o
            ss += jnp.sum(o * o, axis=-1, keepdims=True)
        emit(rs, o_ref[rs, :] * lax.rsqrt(ss * (1.0 / d) + EPS) * norm_w)


def _pack_kernel(w_ref, o_ref):
    o_ref[...] = _pack(w_ref[...])


def _pack_weight(w):
    k, n = w.shape
    rows = min(PACK_ROWS, k)
    cols = next(c for c in (2048, 1408, 1024, 512, 256, 128) if n % c == 0)
    assert k % rows == 0 and rows % (8 * PACK) == 0
    return pl.pallas_call(
        _pack_kernel,
        grid=(k // rows, n // cols),
        in_specs=[pl.BlockSpec((rows, cols), lambda i, j: (i, j))],
        out_specs=pl.BlockSpec((rows // PACK, cols), lambda i, j: (i, j)),
        out_shape=jax.ShapeDtypeStruct((k // PACK, n), U32),
        compiler_params=_params(("parallel", "parallel"), 4 * _nbytes((rows, cols), F32)),
        name="pack_weight",
    )(w)


def _pack_gate_up_kernel(g_ref, u_ref, o_ref):
    o_ref[:, :GATE_UP_LANES] = _pack(g_ref[...])
    o_ref[:, GATE_UP_LANES:] = _pack(u_ref[...])


def _pack_gate_up(w_gu):
    k, n = w_gu.shape
    n_blocks = n // 2 // GATE_UP_LANES
    assert n % (2 * GATE_UP_LANES) == 0
    return pl.pallas_call(
        _pack_gate_up_kernel,
        grid=(n_blocks,),
        in_specs=[pl.BlockSpec((k, GATE_UP_LANES), lambda j: (0, j)),
                  pl.BlockSpec((k, GATE_UP_LANES), lambda j: (0, j + n_blocks))],
        out_specs=pl.BlockSpec((k // PACK, 2 * GATE_UP_LANES), lambda j: (0, j)),
        out_shape=jax.ShapeDtypeStruct((k // PACK, n), U32),
        compiler_params=_params(("parallel",), 8 * _nbytes((k, GATE_UP_LANES), F32)),
        name="pack_gate_up",
    )(w_gu, w_gu)


def _ffn_up_kernel(*refs, norm_input):
    if norm_input:
        x_ref, nw_ref, w_ref, o_ref, n_scr = refs

        @pl.when(pl.program_id(1) == 0)
        def _():
            n_scr[...] = _rms(x_ref[...], nw_ref[...]).astype(BF16)

        lhs = lambda rs: n_scr[rs, :]
    else:
        n_ref, w_ref, o_ref = refs
        lhs = lambda rs: _unpack(n_ref[_packed(rs), :])
    for rs, cs in _tiles(o_ref.shape[0] * PACK, w_ref.shape[1], MATMUL_ROW_TILE, 2 * GATE_UP_LANES):
        z = _dot(lhs(rs), _unpack(w_ref[:, cs]))
        g, u = z[:, :GATE_UP_LANES], z[:, GATE_UP_LANES:]
        o_ref[_packed(rs), cs.start // 2:cs.stop // 2] = _pack(g * _sigmoid(g) * u)


def _ffn_up(x_or_n, norm_w, w_gu, t):
    d = x_or_n.shape[1]
    d_ff = w_gu.shape[1] // 2
    assert t % FFN_UP_ROWS == 0 and d_ff % FFN_UP_COLS == 0 and FFN_UP_COLS % GATE_UP_LANES == 0
    norm_input = norm_w is not None
    w_spec = pl.BlockSpec((d // PACK, 2 * FFN_UP_COLS), lambda i, j: (0, j))
    if norm_input:
        in_specs = [pl.BlockSpec((FFN_UP_ROWS, d), lambda i, j: (i, 0)), pl.BlockSpec((1, d), lambda i, j: (0, 0)), w_spec]
        args = (x_or_n, norm_w, w_gu)
        scratch = [pltpu.VMEM((FFN_UP_ROWS, d), BF16)]
    else:
        in_specs = [pl.BlockSpec((FFN_UP_ROWS // PACK, d), lambda i, j: (i, 0)), w_spec]
        args = (x_or_n, w_gu)
        scratch = []
    vmem = (2 * _nbytes((FFN_UP_ROWS, d), F32 if norm_input else BF16) + _nbytes((FFN_UP_ROWS, d), BF16)
            + 4 * _nbytes((d, FFN_UP_COLS), BF16) + 2 * _nbytes((FFN_UP_ROWS, FFN_UP_COLS), BF16)
            + 8 * _nbytes((MATMUL_ROW_TILE, 2 * GATE_UP_LANES), F32))
    return pl.pallas_call(
        functools.partial(_ffn_up_kernel, norm_input=norm_input),
        grid=(t // FFN_UP_ROWS, d_ff // FFN_UP_COLS),
        in_specs=in_specs,
        out_specs=pl.BlockSpec((FFN_UP_ROWS // PACK, FFN_UP_COLS), lambda i, j: (i, j)),
        out_shape=jax.ShapeDtypeStruct((t // PACK, d_ff), U32),
        scratch_shapes=scratch,
        compiler_params=_params(("parallel", "arbitrary"), vmem),
        name="ffn_up_norm" if norm_input else "ffn_up",
    )(*args)


def _ffn_down_kernel(x_ref, hid_ref, wd_ref, nw_ref, o_ref, *n_out, final_norm):
    def emit(rs, normed):
        if final_norm:
            o_ref[rs, :] = normed
        else:
            n_out[0][_packed(rs), :] = _pack(normed)

    _residual_proj_norm(x_ref, lambda rs: _unpack(hid_ref[_packed(rs), :]), wd_ref, o_ref, FFN_RES_SCALE,
                        nw_ref[...], emit)


def _ffn_down(x, hidden, w_down, norm_w, final_norm):
    t, d = x.shape
    d_ff = hidden.shape[1]
    assert t % FFN_DOWN_ROWS == 0
    row_spec = lambda: pl.BlockSpec((FFN_DOWN_ROWS, d), lambda i: (i, 0))
    packed_spec = lambda cols: pl.BlockSpec((FFN_DOWN_ROWS // PACK, cols), lambda i: (i, 0))
    vmem = (4 * _nbytes((FFN_DOWN_ROWS, d), F32) + 2 * _nbytes((FFN_DOWN_ROWS, d_ff), BF16)
            + _nbytes((d_ff, d), BF16) + 2 * _nbytes((FFN_DOWN_ROWS, d), BF16)
            + 4 * _nbytes((FFN_DOWN_ROWS, RESIDUAL_COL_TILE), F32))
    out_f32 = jax.ShapeDtypeStruct((t, d), F32)
    return pl.pallas_call(
        functools.partial(_ffn_down_kernel, final_norm=final_norm),
        grid=(t // FFN_DOWN_ROWS,),
        in_specs=[row_spec(), packed_spec(d_ff), _resident((d_ff // PACK, d)), pl.BlockSpec((1, d), lambda i: (0, 0))],
        out_specs=row_spec() if final_norm else (row_spec(), packed_spec(d)),
        out_shape=out_f32 if final_norm else (out_f32, jax.ShapeDtypeStruct((t // PACK, d), U32)),
        compiler_params=_params(("parallel",), vmem),
        name="ffn_down_final" if final_norm else "ffn_down",
    )(x, hidden, w_down, norm_w)


def _in_proj_kernel(n_ref, w_ref, *refs, kind, k_scale):
    seg = pl.program_id(0)
    rows, d = n_ref.shape[0] * PACK, n_ref.shape[1]
    o_ref = refs[1] if kind == "gelu_layernorm" else refs[-1]

    def z_tiles():
        for rs, cs in _tiles(rows, d, MATMUL_ROW_TILE, V7X_MXU_COLS):
            yield rs, cs, _dot(_unpack(n_ref[_packed(rs), :]), _unpack(w_ref[:, cs]))

    def store(rs, cs, val):
        o_ref[0, _packed(rs), cs] = _pack(val)

    if kind in ("rotary", "rotary_transposed"):
        cos_ref, sin_ref, _ = refs
        half = V7X_MXU_COLS // 2
        for rs, cs, z in z_tiles():
            x1, x2 = z[:, :half], z[:, half:]
            cos, sin = cos_ref[rs, :], sin_ref[rs, :]
            if kind == "rotary":
                store(rs, slice(cs.start, cs.start + half), x1 * cos - x2 * sin)
                store(rs, slice(cs.start + half, cs.stop), x1 * sin + x2 * cos)
            else:
                rot = jnp.concatenate([x1 * cos - x2 * sin, x1 * sin + x2 * cos], axis=1) * k_scale
                rot_t = _pack(rot.T)
                for j in range(MATMUL_ROW_TILE // RET_BLOCK):
                    o_ref[rs.start // RET_BLOCK + j, cs.start // V7X_MXU_COLS] = rot_t[:, j * RET_BLOCK:(j + 1) * RET_BLOCK]
    elif kind == "plain":
        for rs, cs, z in z_tiles():
            store(rs, cs, z)
    elif kind == "silu_gain":
        gain_ref, _ = refs
        for rs, cs, z in z_tiles():
            store(rs, cs, z * _sigmoid(z) * gain_ref[:, cs])
    elif kind == "gelu":
        for rs, cs, z in z_tiles():
            store(rs, cs, _gelu_tanh(z))
    elif kind == "gelu_layernorm":
        lnw_ref, _, a_scr = refs
        for rs, _ in _tiles(rows, 1, LN_ROW_TILE, 1):
            for _, cs in _tiles(1, d, 1, V7X_MXU_COLS):
                a_scr[rs, cs] = _gelu_tanh(_dot(_unpack(n_ref[_packed(rs), :]), _unpack(w_ref[:, cs])))
            a = a_scr[rs, :]
            cen = a - jnp.mean(a, axis=-1, keepdims=True)
            var = jnp.mean(cen * cen, axis=-1, keepdims=True)
            o_ref[0, _packed(rs), :] = _pack(cen * lax.rsqrt(var + EPS) * lnw_ref[...])
    elif kind == "sigmoid_bias":
        gb_ref, _ = refs
        for rs, cs, z in z_tiles():
            store(rs, cs, _sigmoid(z + gb_ref[pl.ds(seg, 1), cs]))
    else:
        raise ValueError(kind)


def _in_proj(n, w_in, first_seg, n_seg, kind, extra=(), extra_specs=(), rows=IN_ROWS, k_scale=None):
    t, d = n.shape[0] * PACK, n.shape[1]
    assert t % rows == 0 and w_in.shape[1] % d == 0
    scratch = [pltpu.VMEM((rows, d), F32)] if kind == "gelu_layernorm" else []
    vmem = (4 * _nbytes((rows, d), BF16) + 2 * _nbytes((d, d), BF16) + 4 * _nbytes((rows, d // RET_HEADS // 2), F32)
            + (2 * _nbytes((rows, d), F32) if scratch else 0) + 16 * _nbytes((MATMUL_ROW_TILE, V7X_MXU_COLS), F32))
    if kind == "rotary_transposed":
        assert n_seg == 1
        head = d // RET_HEADS
        out_spec = pl.BlockSpec((rows // RET_BLOCK, RET_HEADS, head // PACK, RET_BLOCK), lambda s, i: (i, 0, 0, 0))
        out_shape = jax.ShapeDtypeStruct((t // RET_BLOCK, RET_HEADS, head // PACK, RET_BLOCK), U32)
    else:
        out_spec = pl.BlockSpec((1, rows // PACK, d), lambda s, i: (s, i, 0))
        out_shape = jax.ShapeDtypeStruct((n_seg, t // PACK, d), U32)
    return pl.pallas_call(
        functools.partial(_in_proj_kernel, kind=kind, k_scale=k_scale),
        grid=(n_seg, t // rows),
        in_specs=[pl.BlockSpec((rows // PACK, d), lambda s, i: (i, 0)),
                  pl.BlockSpec((d // PACK, d), lambda s, i: (0, first_seg + s))] + list(extra_specs),
        out_specs=out_spec,
        out_shape=out_shape,
        scratch_shapes=scratch,
        compiler_params=_params(("parallel", "parallel"), vmem),
        name="in_proj_" + kind,
    )(n, w_in, *extra)


def _decay_rate(dec_ref, h, shape):
    return -jnp.exp(jnp.full(shape, dec_ref[h], F32))


def _ret_bwd_state_kernel(decb_ref, kt_ref, v_ref, sb_ref, state, kdec, cdec):
    c = pl.program_id(1)
    blk = kt_ref.shape[3]
    head = v_ref.shape[2] // RET_HEADS

    @pl.when((pl.program_id(0) == 0) & (c == 0))
    def _():
        col = lax.broadcasted_iota(jnp.int32, (1, blk), 1).astype(F32)
        for h in range(RET_HEADS):
            kdec[h] = jnp.exp(_decay_rate(decb_ref, h, (1, blk)) * col)
            cdec[h] = jnp.exp(_decay_rate(decb_ref, h, (1, head)) * float(blk))

    @pl.when(c == 0)
    def _():
        state[...] = jnp.zeros_like(state)

    for h in range(RET_HEADS):
        sl = slice(h * head, (h + 1) * head)
        st = state[h]
        sb_ref[0, 0, h] = _pack(st)
        kd = (_unpack(kt_ref[0, h]).astype(F32) * kdec[h]).astype(BF16)
        state[h] = st * cdec[h] + _dot(kd, _unpack(v_ref[0, :, sl]))


def _ret_bwd_states(kt, v, dec_b, batch):
    _, t_packed, d = v.shape
    n_blk = t_packed * PACK // batch // RET_BLOCK
    head = d // RET_HEADS
    blk_bytes = _nbytes((RET_BLOCK, d), BF16)
    state_bytes = _nbytes((RET_HEADS, head, head), F32)
    return pl.pallas_call(
        _ret_bwd_state_kernel,
        grid=(batch, n_blk),
        in_specs=[
            pl.BlockSpec(memory_space=pltpu.SMEM),
            pl.BlockSpec((1, RET_HEADS, head // PACK, RET_BLOCK), lambda b, c: (b * n_blk + (n_blk - 1 - c), 0, 0, 0)),
            pl.BlockSpec((1, RET_BLOCK // PACK, d), lambda b, c: (0, b * n_blk + (n_blk - 1 - c), 0)),
        ],
        out_specs=pl.BlockSpec((1, 1, RET_HEADS, head // PACK, head), lambda b, c: (b, n_blk - 1 - c, 0, 0, 0)),
        out_shape=jax.ShapeDtypeStruct((batch, n_blk, RET_HEADS, head // PACK, head), U32),
        scratch_shapes=[
            pltpu.VMEM((RET_HEADS, head, head), F32),
            pltpu.VMEM((RET_HEADS, 1, RET_BLOCK), F32),
            pltpu.VMEM((RET_HEADS, 1, head), F32),
        ],
        compiler_params=_params(("arbitrary", "arbitrary"), 6 * blk_bytes + 4 * state_bytes),
        name="ret_bwd_states",
    )(dec_b, kt, v)


def _project_slice(act_ref, w_ref, acc_ref, i):
    width = act_ref.shape[2]
    w_rows = slice(i * width // PACK, (i + 1) * width // PACK)
    for _, cs in _tiles(1, acc_ref.shape[1], 1, RESIDUAL_COL_TILE):
        part = _dot(act_ref[i], _unpack(w_ref[w_rows, cs]))
        acc_ref[:, cs] = part if i == 0 else acc_ref[:, cs] + part


def _ret_out_kernel(decf_ref, decb_ref, q_ref, kt_ref, v_ref, g_ref, sb_ref, wr_ref, o_ref,
                    state, dmat, qdf, qdb, kdf, cdf, act, acc):
    c = pl.program_id(1)
    blk = q_ref.shape[1] * PACK
    head = q_ref.shape[2] // RET_HEADS

    @pl.when((pl.program_id(0) == 0) & (c == 0))
    def _():
        row = lax.broadcasted_iota(jnp.int32, (blk, head), 0).astype(F32)
        col = lax.broadcasted_iota(jnp.int32, (1, blk), 1).astype(F32)
        ri = lax.broadcasted_iota(jnp.int32, (blk, blk), 0)
        ci = lax.broadcasted_iota(jnp.int32, (blk, blk), 1)
        dist = (ri - ci).astype(F32)
        for h in range(RET_HEADS):
            qdf[h] = jnp.exp(_decay_rate(decf_ref, h, (blk, head)) * (row + 1.0))
            qdb[h] = jnp.exp(_decay_rate(decb_ref, h, (blk, head)) * (float(blk) - row))
            kdf[h] = jnp.exp(_decay_rate(decf_ref, h, (1, blk)) * (float(blk) - 1.0 - col))
            cdf[h] = jnp.exp(_decay_rate(decf_ref, h, (1, head)) * float(blk))
            lf2 = _decay_rate(decf_ref, h, (blk, blk))
            lb2 = _decay_rate(decb_ref, h, (blk, blk))
            dmat[h] = jnp.where(ri >= ci, jnp.exp(lf2 * jnp.maximum(dist, 0.0)), jnp.exp(lb2 * jnp.maximum(-dist, 0.0)))

    @pl.when(c == 0)
    def _():
        state[...] = jnp.zeros_like(state)

    for h in range(RET_HEADS):
        sl = slice(h * head, (h + 1) * head)
        q = _unpack(q_ref[0, :, sl])
        kt = _unpack(kt_ref[0, h])
        v = _unpack(v_ref[0, :, sl])
        st = state[h]
        p = (_dot(q, kt) * dmat[h]).astype(BF16)
        o = _dot(p, v) + _dot(q, st.astype(BF16)) * qdf[h] + _dot(q, _unpack(sb_ref[0, 0, h])) * qdb[h]
        cen = o - jnp.mean(o, axis=-1, keepdims=True)
        var = jnp.mean(cen * cen, axis=-1, keepdims=True)
        act[h] = (_unpack(g_ref[0, :, sl]).astype(F32) * (cen * lax.rsqrt(var + EPS))).astype(BF16)
        kd = (kt.astype(F32) * kdf[h]).astype(BF16)
        state[h] = st * cdf[h] + _dot(kd, v)
        if h >= RET_PROJECT_LAG:
            _project_slice(act, wr_ref, acc, h - RET_PROJECT_LAG)
    for h in range(RET_HEADS - RET_PROJECT_LAG, RET_HEADS):
        _project_slice(act, wr_ref, acc, h)
    o_ref[...] = _pack(acc[...])


def _ret_out(q, kt, v, g, sb, dec_f, dec_b, w_ret_out, batch):
    _, t_packed, d = q.shape
    n_blk = t_packed * PACK // batch // RET_BLOCK
    head = d // RET_HEADS
    chunk = lambda: pl.BlockSpec((1, RET_BLOCK // PACK, d), lambda b, c: (0, b * n_blk + c, 0))
    blk_bytes = _nbytes((RET_BLOCK, d), BF16)
    table_bytes = _nbytes((RET_HEADS, RET_BLOCK, head), F32)
    return pl.pallas_call(
        _ret_out_kernel,
        grid=(batch, n_blk),
        in_specs=[
            pl.BlockSpec(memory_space=pltpu.SMEM),
            pl.BlockSpec(memory_space=pltpu.SMEM),
            chunk(),
            pl.BlockSpec((1, RET_HEADS, head // PACK, RET_BLOCK), lambda b, c: (b * n_blk + c, 0, 0, 0)),
            chunk(), chunk(),
            pl.BlockSpec((1, 1, RET_HEADS, head // PACK, head), lambda b, c: (b, c, 0, 0, 0)),
            _resident((d // PACK, d)),
        ],
        out_specs=pl.BlockSpec((RET_BLOCK // PACK, d), lambda b, c: (b * n_blk + c, 0)),
        out_shape=jax.ShapeDtypeStruct((t_packed, d), U32),
        scratch_shapes=[
            pltpu.VMEM((RET_HEADS, head, head), F32),
            pltpu.VMEM((RET_HEADS, RET_BLOCK, RET_BLOCK), F32),
            pltpu.VMEM((RET_HEADS, RET_BLOCK, head), F32),
            pltpu.VMEM((RET_HEADS, RET_BLOCK, head), F32),
            pltpu.VMEM((RET_HEADS, 1, RET_BLOCK), F32),
            pltpu.VMEM((RET_HEADS, 1, head), F32),
            pltpu.VMEM((RET_HEADS, RET_BLOCK, head), BF16),
            pltpu.VMEM((RET_BLOCK, d), F32),
        ],
        compiler_params=_params(("arbitrary", "arbitrary"),
                                12 * blk_bytes + 6 * table_bytes + _nbytes((d, d), BF16) + 4 * _nbytes((RET_BLOCK, d), F32)),
        name="ret_out",
    )(dec_f, dec_b, q, kt, v, g, sb, w_ret_out)


def _mix_out_kernel(u_ref, vs_ref, ws_ref, bs_ref, ret_ref, gr_ref, gs_ref, h_ref, wsg_ref, wo_ref, o_ref, act, acc):
    rows, d = h_ref.shape
    gdim = d // SGU_GROUPS
    for g in range(SGU_GROUPS):
        cs = slice(g * gdim, (g + 1) * gdim)
        for rs, _ in _tiles(rows, 1, SGU_CHUNK, 1):
            mixed = _dot(ws_ref[g], _unpack(vs_ref[0, _packed(rs), cs])) + bs_ref[g]
            act[g, rs, :] = (_unpack(u_ref[0, _packed(rs), cs]).astype(F32) * mixed).astype(BF16)
        if g >= SGU_PROJECT_LAG:
            _project_slice(act, wsg_ref, acc, g - SGU_PROJECT_LAG)
    for g in range(SGU_GROUPS - SGU_PROJECT_LAG, SGU_GROUPS):
        _project_slice(act, wsg_ref, acc, g)
    merged = _unpack(gr_ref[0]).astype(F32) * _unpack(ret_ref[...]).astype(F32) + _unpack(gs_ref[0]).astype(F32) * acc[...]
    _residual_proj(h_ref, merged.astype(BF16), wo_ref, o_ref)


def _mix_out(u, vs, w_s, b_s, ret_proj, gates, h, w_sgu_out, w_out):
    t, d = h.shape
    gdim = d // SGU_GROUPS
    assert t % OUT_ROWS == 0 and OUT_ROWS % SGU_CHUNK == 0
    rows = lambda: pl.BlockSpec((OUT_ROWS, d), lambda i: (i, 0))
    packed_rows = lambda: pl.BlockSpec((OUT_ROWS // PACK, d), lambda i: (i, 0))
    seg_rows = lambda s: pl.BlockSpec((1, OUT_ROWS // PACK, d), lambda i: (s, i, 0))
    weight = lambda: _resident((d // PACK, d))
    vmem = (2 * _nbytes((d, d), BF16) + 10 * _nbytes((OUT_ROWS, d), BF16) + 4 * _nbytes((OUT_ROWS, d), F32)
            + _nbytes((OUT_ROWS, d), BF16) + 4 * _nbytes((OUT_ROWS, d), F32))
    return pl.pallas_call(
        _mix_out_kernel,
        grid=(t // OUT_ROWS,),
        in_specs=[
            seg_rows(0), seg_rows(0),
            pl.BlockSpec((SGU_GROUPS, SGU_CHUNK, SGU_CHUNK), lambda i: (0, 0, 0)),
            pl.BlockSpec((SGU_GROUPS, SGU_CHUNK, 1), lambda i: (0, 0, 0)),
            packed_rows(), seg_rows(0), seg_rows(1), rows(), weight(), weight(),
        ],
        out_specs=rows(),
        out_shape=jax.ShapeDtypeStruct((t, d), F32),
        scratch_shapes=[pltpu.VMEM((SGU_GROUPS, OUT_ROWS, gdim), BF16), pltpu.VMEM((OUT_ROWS, d), F32)],
        compiler_params=_params(("parallel",), vmem),
        name="mix_out",
    )(u, vs, w_s, b_s, ret_proj, gates, gates, h, w_sgu_out, w_out)


def _mem_kv_kernel(m_ref, nw_ref, w_ref, o_ref, n_scr):
    @pl.when(pl.program_id(1) == 0)
    def _():
        n_scr[...] = _rms(m_ref[...], nw_ref[...]).astype(BF16)

    o_ref[...] = _pack(_dot(n_scr[...], _unpack(w_ref[...])))


def _mem_kv(mem, norm_w, w_kv):
    t, d = mem.shape
    rows = min(KV_ROWS, t)
    n_out = w_kv.shape[1]
    assert t % rows == 0 and n_out % KV_COLS == 0
    vmem = (2 * _nbytes((rows, d), F32) + _nbytes((rows, d), BF16) + 2 * _nbytes((d, KV_COLS), BF16)
            + 2 * _nbytes((rows, KV_COLS), BF16) + 2 * _nbytes((rows, d), F32))
    return pl.pallas_call(
        _mem_kv_kernel,
        grid=(t // rows, n_out // KV_COLS),
        in_specs=[
            pl.BlockSpec((rows, d), lambda i, j: (i, 0)),
            pl.BlockSpec((1, d), lambda i, j: (0, 0)),
            pl.BlockSpec((d // PACK, KV_COLS), lambda i, j: (0, j)),
        ],
        out_specs=pl.BlockSpec((rows // PACK, KV_COLS), lambda i, j: (i, j)),
        out_shape=jax.ShapeDtypeStruct((t // PACK, n_out), U32),
        scratch_shapes=[pltpu.VMEM((rows, d), BF16)],
        compiler_params=_params(("parallel", "arbitrary"), vmem),
        name="mem_kv",
    )(mem, norm_w, w_kv)


def _xattn_kernel(h_ref, nw_ref, kv_ref, wq_ref, wo_ref, next_nw_ref, o_ref, n_out_ref, q_scr, a_scr, acc):
    rows, d = h_ref.shape
    hd = d // XATTN_HEADS
    scale = float(hd) ** -0.5
    for rs, _ in _tiles(rows, 1, rows // 2, 1):
        q_scr[rs, :] = _dot(_rms(h_ref[rs, :], nw_ref[...]).astype(BF16), _unpack(wq_ref[...])).astype(BF16)

    def probs(h):
        sl = slice(h * hd, (h + 1) * hd)
        s = _dot_nt(q_scr[:, sl], _unpack(kv_ref[:, sl])) * scale
        e = jnp.exp(s - jnp.max(s, axis=-1, keepdims=True))
        return (e / jnp.sum(e, axis=-1, keepdims=True)).astype(BF16)

    def emit(rs, normed):
        n_out_ref[_packed(rs), :] = _pack(normed)

    last = XATTN_HEADS - 1
    p = probs(0)
    for h in range(XATTN_HEADS):
        p_next = probs(h + 1) if h < last else None
        a_scr[h] = _dot(p, _unpack(kv_ref[:, d + h * hd:d + (h + 1) * hd])).astype(BF16)
        if h < last:
            _project_slice(a_scr, wo_ref, acc, h)
        else:
            _residual_proj_norm(h_ref, lambda rs: a_scr[last, rs, :], wo_ref, o_ref, None, next_nw_ref[...], emit,
                                w_rows=slice(last * hd // PACK, (last + 1) * hd // PACK), base_ref=acc)
        p = p_next


def _xattn(h, kv, norm_w, w_q, w_o, next_norm_w, batch):
    t, d = h.shape
    n_mem = kv.shape[0] * PACK // batch
    row_blocks = t // batch // XATTN_ROWS
    assert t % (batch * XATTN_ROWS) == 0
    rows = lambda: pl.BlockSpec((XATTN_ROWS, d), lambda b, i: (b * row_blocks + i, 0))
    vec = lambda: pl.BlockSpec((1, d), lambda b, i: (0, 0))
    weight = lambda: _resident((d // PACK, d))
    vmem = (2 * _nbytes((d, d), BF16) + 4 * _nbytes((XATTN_ROWS, d), F32) + 4 * _nbytes((XATTN_ROWS, d), BF16)
            + 2 * _nbytes((n_mem, 2 * d), BF16) + 5 * _nbytes((XATTN_ROWS, d), F32))
    return pl.pallas_call(
        _xattn_kernel,
        grid=(batch, row_blocks),
        in_specs=[rows(), vec(), pl.BlockSpec((n_mem // PACK, 2 * d), lambda b, i: (b, 0)), weight(), weight(), vec()],
        out_specs=(rows(), pl.BlockSpec((XATTN_ROWS // PACK, d), lambda b, i: (b * row_blocks + i, 0))),
        out_shape=(jax.ShapeDtypeStruct((t, d), F32), jax.ShapeDtypeStruct((t // PACK, d), U32)),
        scratch_shapes=[pltpu.VMEM((XATTN_ROWS, d), BF16),
                        pltpu.VMEM((XATTN_HEADS, XATTN_ROWS, d // XATTN_HEADS), BF16),
                        pltpu.VMEM((XATTN_ROWS, d), F32)],
        compiler_params=_params(("parallel", "parallel"), vmem),
        name="xattn",
    )(h, norm_w, kv, w_q, w_o, next_norm_w)


def _rope_tables(seq, half):
    freqs = ROPE_BASE ** (-jnp.linspace(0.0, 1.0, half, dtype=F32))
    ang = jnp.arange(seq, dtype=F32)[:, None] * freqs[None, :]
    return jnp.cos(ang), jnp.sin(ang)


def _trunk(x, mem, w):
    batch, seq, d = x.shape
    t = batch * seq
    x = x.reshape(t, d)
    half = d // RET_HEADS // 2
    cos, sin = _rope_tables(seq, half)
    pos_blocks = seq // IN_ROWS
    assert seq % IN_ROWS == 0
    pos_spec = lambda: pl.BlockSpec((IN_ROWS, half), lambda s, i: (i % pos_blocks, 0))

    hidden = _ffn_up(x, w["ffn1_norm"], w["ffn1_w_gu"], t)
    h, n = _ffn_down(x, hidden, w["ffn1_w_down"], w["mix_norm"], final_norm=False)

    w_in = w["w_in"]
    vec_spec = lambda: pl.BlockSpec((1, d), lambda s, i: (0, 0))
    q = _in_proj(n, w_in, SEG_Q, 1, "rotary", (cos, sin), (pos_spec(), pos_spec()))
    kt = _in_proj(n, w_in, SEG_K, 1, "rotary_transposed", (cos, sin), (pos_spec(), pos_spec()),
                  k_scale=float(2 * half) ** -0.5)
    v = _in_proj(n, w_in, SEG_V, 1, "plain")
    g = _in_proj(n, w_in, SEG_G, 1, "silu_gain", (w["ret_gn_w"],), (vec_spec(),))
    u = _in_proj(n, w_in, SEG_U, 1, "gelu")
    vs = _in_proj(n, w_in, SEG_VS, 1, "gelu_layernorm", (w["sgu_norm_w"],), (vec_spec(),), rows=IN_LN_ROWS)
    gates = _in_proj(n, w_in, SEG_GATE_R, 2, "sigmoid_bias", (w["gate_bias"],),
                     (pl.BlockSpec((2, d), lambda s, i: (0, 0)),))

    sb = _ret_bwd_states(kt, v, w["ret_decay_bwd"], batch)
    ret_proj = _ret_out(q, kt, v, g, sb, w["ret_decay_fwd"], w["ret_decay_bwd"], w["w_ret_out"], batch)
    h = _mix_out(u, vs, w["sgu_w_s"], w["sgu_b_s"], ret_proj, gates, h, w["w_sgu_out"], w["w_out"])

    kv = _mem_kv(mem.reshape(-1, d), w["xattn_norm_mem"], w["xattn_w_kv"])
    h, n = _xattn(h, kv, w["xattn_norm_q"], w["xattn_w_q"], w["xattn_w_o"], w["ffn2_norm"], batch)

    hidden = _ffn_up(n, None, w["ffn2_w_gu"], t)
    y = _ffn_down(h, hidden, w["ffn2_w_down"], w["final_norm"], final_norm=True)
    return y.reshape(batch, seq, d)


def _prepare(ffn1_norm, ffn1_w_gu, ffn1_w_down, mix_norm, w_in, gate_bias, ret_decay_fwd, ret_decay_bwd, ret_gn_w, w_ret_out, sgu_norm_w, sgu_w_s, sgu_b_s, w_sgu_out, w_out, xattn_norm_q, xattn_norm_mem, xattn_w_q, xattn_w_kv, xattn_w_o, ffn2_norm, ffn2_w_gu, ffn2_w_down, final_norm):
    assert ffn1_norm.shape[0] == 1, "single-layer trunk"
    pw = lambda w: _pack_weight(w[0])
    return {
        "ffn1_norm": ffn1_norm, "ffn1_w_gu": _pack_gate_up(ffn1_w_gu[0]), "ffn1_w_down": pw(ffn1_w_down),
        "mix_norm": mix_norm, "w_in": pw(w_in), "gate_bias": gate_bias[0],
        "ret_decay_fwd": ret_decay_fwd[0], "ret_decay_bwd": ret_decay_bwd[0], "ret_gn_w": ret_gn_w,
        "w_ret_out": pw(w_ret_out), "sgu_norm_w": sgu_norm_w,
        "sgu_w_s": sgu_w_s[0].astype(BF16), "sgu_b_s": sgu_b_s[0][:, :, None],
        "w_sgu_out": pw(w_sgu_out), "w_out": pw(w_out),
        "xattn_norm_q": xattn_norm_q, "xattn_norm_mem": xattn_norm_mem,
        "xattn_w_q": pw(xattn_w_q), "xattn_w_kv": pw(xattn_w_kv), "xattn_w_o": pw(xattn_w_o),
        "ffn2_norm": ffn2_norm, "ffn2_w_gu": _pack_gate_up(ffn2_w_gu[0]), "ffn2_w_down": pw(ffn2_w_down),
        "final_norm": final_norm[None, :],
    }


def kernel(x_prompt, x_sample, mem_prompt, mem_sample, ffn1_norm, ffn1_w_gu, ffn1_w_down, mix_norm, w_in, gate_bias, ret_decay_fwd, ret_decay_bwd, ret_gn_w, w_ret_out, sgu_norm_w, sgu_w_s, sgu_b_s, w_sgu_out, w_out, xattn_norm_q, xattn_norm_mem, xattn_w_q, xattn_w_kv, xattn_w_o, ffn2_norm, ffn2_w_gu, ffn2_w_down, final_norm):
    w = _prepare(ffn1_norm, ffn1_w_gu, ffn1_w_down, mix_norm, w_in, gate_bias, ret_decay_fwd, ret_decay_bwd, ret_gn_w, w_ret_out, sgu_norm_w, sgu_w_s, sgu_b_s, w_sgu_out, w_out, xattn_norm_q, xattn_norm_mem, xattn_w_q, xattn_w_kv, xattn_w_o, ffn2_norm, ffn2_w_gu, ffn2_w_down, final_norm)
    return _trunk(x_prompt, mem_prompt, w), _trunk(x_sample, mem_sample, w)
```

```python
import functools
import math

import jax
import jax.numpy as jnp
from jax import lax
from jax.experimental import pallas as pl
from jax.experimental.pallas import tpu as pltpu

F32 = jnp.float32
BF16 = jnp.bfloat16
U32 = jnp.uint32
PACK = 2

EPS = 1e-6
FFN_RES_SCALE = 0.5
ROPE_BASE = 10000.0
RET_HEADS = 8
SGU_GROUPS = 8
SGU_CHUNK = 128
XATTN_HEADS = 4
SEG_Q, SEG_K, SEG_V, SEG_G, SEG_U, SEG_VS, SEG_GATE_R, SEG_GATE_S = range(8)

V7X_VMEM_BYTES = 64 * 1024 * 1024
V7X_MXU_COLS = 256

RET_BLOCK = 256
RET_BWD_CHUNKS_PER_STEP = 4
RET_PROJECT_LAG = 2
SGU_PROJECT_LAG = 1

PACK_ROWS = 512
PACK_VMEM_RESERVATION = V7X_VMEM_BYTES - 8 * 1024 * 1024
FFN_UP_ROWS = 1024
FFN_UP_COLS = 1408
GATE_UP_LANES = 128
FFN_DOWN_ROWS = 512
IN_ROWS = 2048
IN_LN_ROWS = 1024
LN_ROW_TILE = 256
OUT_ROWS = 256
KV_ROWS = 512
KV_COLS = 1024
XATTN_ROWS = 512
RESIDUAL_COL_TILE = 512
MATMUL_ROW_TILE = 512


def _nbytes(shape, dtype):
    return math.prod(shape) * jnp.dtype(dtype).itemsize


def _params(semantics, vmem_bytes):
    assert vmem_bytes <= V7X_VMEM_BYTES, vmem_bytes
    return pltpu.CompilerParams(dimension_semantics=semantics, vmem_limit_bytes=int(vmem_bytes))


def _resident(shape):
    return pl.BlockSpec(shape, lambda *_: (0,) * len(shape), pipeline_mode=pl.Buffered(1))


def _unpack(words):
    return pltpu.bitcast(words, BF16)


def _pack(vals):
    return pltpu.bitcast(vals.astype(BF16), U32)


def _packed(rows):
    return slice(rows.start // PACK, rows.stop // PACK)


def _tiles(n_rows, n_cols, row_tile, col_tile):
    for r in range(n_rows // row_tile):
        for c in range(n_cols // col_tile):
            yield slice(r * row_tile, (r + 1) * row_tile), slice(c * col_tile, (c + 1) * col_tile)


def _rms(x, w):
    return x * lax.rsqrt(jnp.mean(x * x, axis=-1, keepdims=True) + EPS) * w


def _dot(a, b):
    return jnp.dot(a, b, preferred_element_type=F32)


def _dot_nt(a, b):
    return lax.dot_general(a, b, (((1,), (1,)), ((), ())), preferred_element_type=F32)


def _sigmoid(x):
    return 0.5 * jnp.tanh(0.5 * x) + 0.5


def _gelu_tanh(x):
    c = math.sqrt(2.0 / math.pi)
    return x * (0.5 * jnp.tanh(x * (c + (c * 0.044715) * (x * x))) + 0.5)


def _residual_proj(x_ref, a, w_ref, o_ref):
    for _, cs in _tiles(1, o_ref.shape[1], 1, RESIDUAL_COL_TILE):
        o_ref[:, cs] = x_ref[:, cs] + _dot(a, _unpack(w_ref[:, cs]))


def _residual_proj_norm(x_ref, lhs, w_ref, o_ref, scale, norm_w, emit, w_rows=slice(None), base_ref=None):
    rows, d = o_ref.shape
    for rs, _ in _tiles(rows, 1, rows // 2, 1):
        a = lhs(rs)
        ss = jnp.zeros((a.shape[0], 1), F32)
        for _, cs in _tiles(1, d, 1, RESIDUAL_COL_TILE):
            y = _dot(a, _unpack(w_ref[w_rows, cs]))
            if base_ref is not None:
                y = base_ref[rs, cs] + y
            o = x_ref[rs, cs] + (y if scale is None else scale * y)
            o_ref[rs, cs] = o
            ss += jnp.sum(o * o, axis=-1, keepdims=True)
        emit(rs, o_ref[rs, :] * lax.rsqrt(ss * (1.0 / d) + EPS) * norm_w)


def _pack_kernel(w_ref, o_ref):
    o_ref[...] = _pack(w_ref[...])


def _pack_weight(w):
    k, n = w.shape
    rows = min(PACK_ROWS, k)
    cols = next(c for c in (2048, 1408, 1024, 512, 256, 128) if n % c == 0)
    assert k % rows == 0 and rows % (8 * PACK) == 0
    return pl.pallas_call(
        _pack_kernel,
        grid=(k // rows, n // cols),
        in_specs=[pl.BlockSpec((rows, cols), lambda i, j: (i, j))],
        out_specs=pl.BlockSpec((rows // PACK, cols), lambda i, j: (i, j)),
        out_shape=jax.ShapeDtypeStruct((k // PACK, n), U32),
        compiler_params=_params(("parallel", "parallel"), PACK_VMEM_RESERVATION),
        name="pack_weight",
    )(w)


def _pack_gate_up_kernel(g_ref, u_ref, o_ref):
    o_ref[:, :GATE_UP_LANES] = _pack(g_ref[...])
    o_ref[:, GATE_UP_LANES:] = _pack(u_ref[...])


def _pack_gate_up(w_gu):
    k, n = w_gu.shape
    n_blocks = n // 2 // GATE_UP_LANES
    assert n % (2 * GATE_UP_LANES) == 0
    return pl.pallas_call(
        _pack_gate_up_kernel,
        grid=(n_blocks,),
        in_specs=[pl.BlockSpec((k, GATE_UP_LANES), lambda j: (0, j)),
                  pl.BlockSpec((k, GATE_UP_LANES), lambda j: (0, j + n_blocks))],
        out_specs=pl.BlockSpec((k // PACK, 2 * GATE_UP_LANES), lambda j: (0, j)),
        out_shape=jax.ShapeDtypeStruct((k // PACK, n), U32),
        compiler_params=_params(("parallel",), PACK_VMEM_RESERVATION),
        name="pack_gate_up",
    )(w_gu, w_gu)


def _ffn_up_kernel(*refs, norm_input):
    if norm_input:
        x_ref, nw_ref, w_ref, o_ref, n_scr = refs

        @pl.when(pl.program_id(1) == 0)
        def _():
            n_scr[...] = _rms(x_ref[...], nw_ref[...]).astype(BF16)

        lhs = lambda rs: n_scr[rs, :]
    else:
        n_ref, w_ref, o_ref = refs
        lhs = lambda rs: _unpack(n_ref[_packed(rs), :])
    for rs, cs in _tiles(o_ref.shape[0] * PACK, w_ref.shape[1], MATMUL_ROW_TILE, 2 * GATE_UP_LANES):
        z = _dot(lhs(rs), _unpack(w_ref[:, cs]))
        g, u = z[:, :GATE_UP_LANES], z[:, GATE_UP_LANES:]
        o_ref[_packed(rs), cs.start // 2:cs.stop // 2] = _pack(g * _sigmoid(g) * u)


def _ffn_up(x_or_n, norm_w, w_gu, t):
    d = x_or_n.shape[1]
    d_ff = w_gu.shape[1] // 2
    assert t % FFN_UP_ROWS == 0 and d_ff % FFN_UP_COLS == 0 and FFN_UP_COLS % GATE_UP_LANES == 0
    norm_input = norm_w is not None
    w_spec = pl.BlockSpec((d // PACK, 2 * FFN_UP_COLS), lambda i, j: (0, j))
    if norm_input:
        in_specs = [pl.BlockSpec((FFN_UP_ROWS, d), lambda i, j: (i, 0)), pl.BlockSpec((1, d), lambda i, j: (0, 0)), w_spec]
        args = (x_or_n, norm_w, w_gu)
        scratch = [pltpu.VMEM((FFN_UP_ROWS, d), BF16)]
    else:
        in_specs = [pl.BlockSpec((FFN_UP_ROWS // PACK, d), lambda i, j: (i, 0)), w_spec]
        args = (x_or_n, w_gu)
        scratch = []
    vmem = (2 * _nbytes((FFN_UP_ROWS, d), F32 if norm_input else BF16) + _nbytes((FFN_UP_ROWS, d), BF16)
            + 4 * _nbytes((d, FFN_UP_COLS), BF16) + 2 * _nbytes((FFN_UP_ROWS, FFN_UP_COLS), BF16)
            + 8 * _nbytes((MATMUL_ROW_TILE, 2 * GATE_UP_LANES), F32))
    return pl.pallas_call(
        functools.partial(_ffn_up_kernel, norm_input=norm_input),
        grid=(t // FFN_UP_ROWS, d_ff // FFN_UP_COLS),
        in_specs=in_specs,
        out_specs=pl.BlockSpec((FFN_UP_ROWS // PACK, FFN_UP_COLS), lambda i, j: (i, j)),
        out_shape=jax.ShapeDtypeStruct((t // PACK, d_ff), U32),
        scratch_shapes=scratch,
        compiler_params=_params(("parallel", "arbitrary"), vmem),
        name="ffn_up_norm" if norm_input else "ffn_up",
    )(*args)


def _ffn_down_kernel(x_ref, hid_ref, wd_ref, nw_ref, o_ref, *n_out, final_norm):
    def emit(rs, normed):
        if final_norm:
            o_ref[rs, :] = normed
        else:
            n_out[0][_packed(rs), :] = _pack(normed)

    _residual_proj_norm(x_ref, lambda rs: _unpack(hid_ref[_packed(rs), :]), wd_ref, o_ref, FFN_RES_SCALE,
                        nw_ref[...], emit)


def _ffn_down(x, hidden, w_down, norm_w, final_norm):
    t, d = x.shape
    d_ff = hidden.shape[1]
    assert t % FFN_DOWN_ROWS == 0
    row_spec = lambda: pl.BlockSpec((FFN_DOWN_ROWS, d), lambda i: (i, 0))
    packed_spec = lambda cols: pl.BlockSpec((FFN_DOWN_ROWS // PACK, cols), lambda i: (i, 0))
    vmem = (4 * _nbytes((FFN_DOWN_ROWS, d), F32) + 2 * _nbytes((FFN_DOWN_ROWS, d_ff), BF16)
            + _nbytes((d_ff, d), BF16) + 2 * _nbytes((FFN_DOWN_ROWS, d), BF16)
            + 4 * _nbytes((FFN_DOWN_ROWS, RESIDUAL_COL_TILE), F32))
    out_f32 = jax.ShapeDtypeStruct((t, d), F32)
    return pl.pallas_call(
        functools.partial(_ffn_down_kernel, final_norm=final_norm),
        grid=(t // FFN_DOWN_ROWS,),
        in_specs=[row_spec(), packed_spec(d_ff), _resident((d_ff // PACK, d)), pl.BlockSpec((1, d), lambda i: (0, 0))],
        out_specs=row_spec() if final_norm else (row_spec(), packed_spec(d)),
        out_shape=out_f32 if final_norm else (out_f32, jax.ShapeDtypeStruct((t // PACK, d), U32)),
        compiler_params=_params(("parallel",), vmem),
        name="ffn_down_final" if final_norm else "ffn_down",
    )(x, hidden, w_down, norm_w)


def _in_proj_kernel(n_ref, w_ref, *refs, kind, k_scale):
    seg = pl.program_id(0)
    rows, d = n_ref.shape[0] * PACK, n_ref.shape[1]
    o_ref = refs[1] if kind == "gelu_layernorm" else refs[-1]

    def z_tiles():
        for rs, cs in _tiles(rows, d, MATMUL_ROW_TILE, V7X_MXU_COLS):
            yield rs, cs, _dot(_unpack(n_ref[_packed(rs), :]), _unpack(w_ref[:, cs]))

    def store(rs, cs, val):
        o_ref[0, _packed(rs), cs] = _pack(val)

    if kind in ("rotary", "rotary_transposed"):
        cos_ref, sin_ref, _ = refs
        half = V7X_MXU_COLS // 2
        for rs, cs, z in z_tiles():
            x1, x2 = z[:, :half], z[:, half:]
            cos, sin = cos_ref[rs, :], sin_ref[rs, :]
            if kind == "rotary":
                store(rs, slice(cs.start, cs.start + half), x1 * cos - x2 * sin)
                store(rs, slice(cs.start + half, cs.stop), x1 * sin + x2 * cos)
            else:
                rot = jnp.concatenate([x1 * cos - x2 * sin, x1 * sin + x2 * cos], axis=1) * k_scale
                rot_t = _pack(rot.T)
                for j in range(MATMUL_ROW_TILE // RET_BLOCK):
                    o_ref[rs.start // RET_BLOCK + j, cs.start // V7X_MXU_COLS] = rot_t[:, j * RET_BLOCK:(j + 1) * RET_BLOCK]
    elif kind == "plain":
        for rs, cs, z in z_tiles():
            store(rs, cs, z)
    elif kind == "silu_gain":
        gain_ref, _ = refs
        for rs, cs, z in z_tiles():
            store(rs, cs, z * _sigmoid(z) * gain_ref[:, cs])
    elif kind == "gelu":
        for rs, cs, z in z_tiles():
            store(rs, cs, _gelu_tanh(z))
    elif kind == "gelu_layernorm":
        lnw_ref, _, a_scr = refs
        for rs, _ in _tiles(rows, 1, LN_ROW_TILE, 1):
            for _, cs in _tiles(1, d, 1, V7X_MXU_COLS):
                a_scr[rs, cs] = _gelu_tanh(_dot(_unpack(n_ref[_packed(rs), :]), _unpack(w_ref[:, cs])))
            a = a_scr[rs, :]
            cen = a - jnp.mean(a, axis=-1, keepdims=True)
            var = jnp.mean(cen * cen, axis=-1, keepdims=True)
            o_ref[0, _packed(rs), :] = _pack(cen * lax.rsqrt(var + EPS) * lnw_ref[...])
    elif kind == "sigmoid_bias":
        gb_ref, _ = refs
        for rs, cs, z in z_tiles():
            store(rs, cs, _sigmoid(z + gb_ref[pl.ds(seg, 1), cs]))
    else:
        raise ValueError(kind)


def _in_proj(n, w_in, first_seg, n_seg, kind, extra=(), extra_specs=(), rows=IN_ROWS, k_scale=None):
    t, d = n.shape[0] * PACK, n.shape[1]
    assert t % rows == 0 and w_in.shape[1] % d == 0
    scratch = [pltpu.VMEM((rows, d), F32)] if kind == "gelu_layernorm" else []
    vmem = (4 * _nbytes((rows, d), BF16) + 2 * _nbytes((d, d), BF16) + 4 * _nbytes((rows, d // RET_HEADS // 2), F32)
            + (2 * _nbytes((rows, d), F32) if scratch else 0) + 16 * _nbytes((MATMUL_ROW_TILE, V7X_MXU_COLS), F32))
    if kind == "rotary_transposed":
        assert n_seg == 1
        head = d // RET_HEADS
        out_spec = pl.BlockSpec((rows // RET_BLOCK, RET_HEADS, head // PACK, RET_BLOCK), lambda s, i: (i, 0, 0, 0))
        out_shape = jax.ShapeDtypeStruct((t // RET_BLOCK, RET_HEADS, head // PACK, RET_BLOCK), U32)
    else:
        out_spec = pl.BlockSpec((1, rows // PACK, d), lambda s, i: (s, i, 0))
        out_shape = jax.ShapeDtypeStruct((n_seg, t // PACK, d), U32)
    return pl.pallas_call(
        functools.partial(_in_proj_kernel, kind=kind, k_scale=k_scale),
        grid=(n_seg, t // rows),
        in_specs=[pl.BlockSpec((rows // PACK, d), lambda s, i: (i, 0)),
                  pl.BlockSpec((d // PACK, d), lambda s, i: (0, first_seg + s))] + list(extra_specs),
        out_specs=out_spec,
        out_shape=out_shape,
        scratch_shapes=scratch,
        compiler_params=_params(("parallel", "parallel"), vmem),
        name="in_proj_" + kind,
    )(n, w_in, *extra)


def _decay_rate(dec_ref, h, shape):
    return -jnp.exp(jnp.full(shape, dec_ref[h], F32))


def _ret_bwd_state_kernel(decb_ref, kt_ref, v_ref, sb_ref, state, kdec, cdec):
    c = pl.program_id(1)
    blk = kt_ref.shape[3]
    head = v_ref.shape[2] // RET_HEADS

    @pl.when((pl.program_id(0) == 0) & (c == 0))
    def _():
        col = lax.broadcasted_iota(jnp.int32, (1, blk), 1).astype(F32)
        for h in range(RET_HEADS):
            kdec[h] = jnp.exp(_decay_rate(decb_ref, h, (1, blk)) * col)
            cdec[h] = jnp.exp(_decay_rate(decb_ref, h, (1, head)) * float(blk))

    @pl.when(c == 0)
    def _():
        state[...] = jnp.zeros_like(state)

    for j in reversed(range(kt_ref.shape[0])):
        rows = slice(j * blk // PACK, (j + 1) * blk // PACK)
        for h in range(RET_HEADS):
            sl = slice(h * head, (h + 1) * head)
            st = state[h]
            sb_ref[0, j, h] = _pack(st)
            kd = (_unpack(kt_ref[j, h]).astype(F32) * kdec[h]).astype(BF16)
            state[h] = st * cdec[h] + _dot(kd, _unpack(v_ref[0, rows, sl]))


def _ret_bwd_states(kt, v, dec_b, batch):
    _, t_packed, d = v.shape
    n_blk = t_packed * PACK // batch // RET_BLOCK
    head = d // RET_HEADS
    per_step = RET_BWD_CHUNKS_PER_STEP
    n_steps = n_blk // per_step
    assert n_blk % per_step == 0
    blk_bytes = per_step * _nbytes((RET_BLOCK, d), BF16)
    state_bytes = _nbytes((RET_HEADS, head, head), F32)
    return pl.pallas_call(
        _ret_bwd_state_kernel,
        grid=(batch, n_steps),
        in_specs=[
            pl.BlockSpec(memory_space=pltpu.SMEM),
            pl.BlockSpec((per_step, RET_HEADS, head // PACK, RET_BLOCK),
                         lambda b, c: (b * n_steps + (n_steps - 1 - c), 0, 0, 0)),
            pl.BlockSpec((1, per_step * RET_BLOCK // PACK, d), lambda b, c: (0, b * n_steps + (n_steps - 1 - c), 0)),
        ],
        out_specs=pl.BlockSpec((1, per_step, RET_HEADS, head // PACK, head), lambda b, c: (b, n_steps - 1 - c, 0, 0, 0)),
        out_shape=jax.ShapeDtypeStruct((batch, n_blk, RET_HEADS, head // PACK, head), U32),
        scratch_shapes=[
            pltpu.VMEM((RET_HEADS, head, head), F32),
            pltpu.VMEM((RET_HEADS, 1, RET_BLOCK), F32),
            pltpu.VMEM((RET_HEADS, 1, head), F32),
        ],
        compiler_params=_params(("arbitrary", "arbitrary"), 6 * blk_bytes + 4 * state_bytes),
        name="ret_bwd_states",
    )(dec_b, kt, v)


def _project_slice(act_ref, w_ref, acc_ref, i):
    width = act_ref.shape[2]
    w_rows = slice(i * width // PACK, (i + 1) * width // PACK)
    for _, cs in _tiles(1, acc_ref.shape[1], 1, RESIDUAL_COL_TILE):
        part = _dot(act_ref[i], _unpack(w_ref[w_rows, cs]))
        acc_ref[:, cs] = part if i == 0 else acc_ref[:, cs] + part


def _ret_out_kernel(decf_ref, decb_ref, q_ref, kt_ref, v_ref, g_ref, sb_ref, wr_ref, o_ref,
                    state, dmat, qdf, qdb, kdf, cdf, act, acc):
    c = pl.program_id(1)
    blk = q_ref.shape[1] * PACK
    head = q_ref.shape[2] // RET_HEADS

    @pl.when((pl.program_id(0) == 0) & (c == 0))
    def _():
        row = lax.broadcasted_iota(jnp.int32, (blk, head), 0).astype(F32)
        col = lax.broadcasted_iota(jnp.int32, (1, blk), 1).astype(F32)
        ri = lax.broadcasted_iota(jnp.int32, (blk, blk), 0)
        ci = lax.broadcasted_iota(jnp.int32, (blk, blk), 1)
        dist = (ri - ci).astype(F32)
        for h in range(RET_HEADS):
            qdf[h] = jnp.exp(_decay_rate(decf_ref, h, (blk, head)) * (row + 1.0))
            qdb[h] = jnp.exp(_decay_rate(decb_ref, h, (blk, head)) * (float(blk) - row))
            kdf[h] = jnp.exp(_decay_rate(decf_ref, h, (1, blk)) * (float(blk) - 1.0 - col))
            cdf[h] = jnp.exp(_decay_rate(decf_ref, h, (1, head)) * float(blk))
            lf2 = _decay_rate(decf_ref, h, (blk, blk))
            lb2 = _decay_rate(decb_ref, h, (blk, blk))
            dmat[h] = jnp.where(ri >= ci, jnp.exp(lf2 * jnp.maximum(dist, 0.0)), jnp.exp(lb2 * jnp.maximum(-dist, 0.0)))

    @pl.when(c == 0)
    def _():
        state[...] = jnp.zeros_like(state)

    for h in range(RET_HEADS):
        sl = slice(h * head, (h + 1) * head)
        q = _unpack(q_ref[0, :, sl])
        kt = _unpack(kt_ref[0, h])
        v = _unpack(v_ref[0, :, sl])
        st = state[h]
        p = (_dot(q, kt) * dmat[h]).astype(BF16)
        o = _dot(p, v) + _dot(q, st.astype(BF16)) * qdf[h] + _dot(q, _unpack(sb_ref[0, 0, h])) * qdb[h]
        cen = o - jnp.mean(o, axis=-1, keepdims=True)
        var = jnp.mean(cen * cen, axis=-1, keepdims=True)
        act[h] = (_unpack(g_ref[0, :, sl]).astype(F32) * (cen * lax.rsqrt(var + EPS))).astype(BF16)
        kd = (kt.astype(F32) * kdf[h]).astype(BF16)
        state[h] = st * cdf[h] + _dot(kd, v)
        if h >= RET_PROJECT_LAG:
            _project_slice(act, wr_ref, acc, h - RET_PROJECT_LAG)
    for h in range(RET_HEADS - RET_PROJECT_LAG, RET_HEADS):
        _project_slice(act, wr_ref, acc, h)
    o_ref[...] = _pack(acc[...])


def _ret_out(q, kt, v, g, sb, dec_f, dec_b, w_ret_out, batch):
    _, t_packed, d = q.shape
    n_blk = t_packed * PACK // batch // RET_BLOCK
    head = d // RET_HEADS
    chunk = lambda: pl.BlockSpec((1, RET_BLOCK // PACK, d), lambda b, c: (0, b * n_blk + c, 0))
    blk_bytes = _nbytes((RET_BLOCK, d), BF16)
    table_bytes = _nbytes((RET_HEADS, RET_BLOCK, head), F32)
    return pl.pallas_call(
        _ret_out_kernel,
        grid=(batch, n_blk),
        in_specs=[
            pl.BlockSpec(memory_space=pltpu.SMEM),
            pl.BlockSpec(memory_space=pltpu.SMEM),
            chunk(),
            pl.BlockSpec((1, RET_HEADS, head // PACK, RET_BLOCK), lambda b, c: (b * n_blk + c, 0, 0, 0)),
            chunk(), chunk(),
            pl.BlockSpec((1, 1, RET_HEADS, head // PACK, head), lambda b, c: (b, c, 0, 0, 0)),
            _resident((d // PACK, d)),
        ],
        out_specs=pl.BlockSpec((RET_BLOCK // PACK, d), lambda b, c: (b * n_blk + c, 0)),
        out_shape=jax.ShapeDtypeStruct((t_packed, d), U32),
        scratch_shapes=[
            pltpu.VMEM((RET_HEADS, head, head), F32),
            pltpu.VMEM((RET_HEADS, RET_BLOCK, RET_BLOCK), F32),
            pltpu.VMEM((RET_HEADS, RET_BLOCK, head), F32),
            pltpu.VMEM((RET_HEADS, RET_BLOCK, head), F32),
            pltpu.VMEM((RET_HEADS, 1, RET_BLOCK), F32),
            pltpu.VMEM((RET_HEADS, 1, head), F32),
            pltpu.VMEM((RET_HEADS, RET_BLOCK, head), BF16),
            pltpu.VMEM((RET_BLOCK, d), F32),
        ],
        compiler_params=_params(("arbitrary", "arbitrary"),
                                12 * blk_bytes + 6 * table_bytes + _nbytes((d, d), BF16) + 4 * _nbytes((RET_BLOCK, d), F32)),
        name="ret_out",
    )(dec_f, dec_b, q, kt, v, g, sb, w_ret_out)


def _mix_out_kernel(u_ref, vs_ref, ws_ref, bs_ref, ret_ref, gr_ref, gs_ref, h_ref, wsg_ref, wo_ref, o_ref, act, acc):
    rows, d = h_ref.shape
    gdim = d // SGU_GROUPS
    for g in range(SGU_GROUPS):
        cs = slice(g * gdim, (g + 1) * gdim)
        for rs, _ in _tiles(rows, 1, SGU_CHUNK, 1):
            mixed = _dot(ws_ref[g], _unpack(vs_ref[0, _packed(rs), cs])) + bs_ref[g]
            act[g, rs, :] = (_unpack(u_ref[0, _packed(rs), cs]).astype(F32) * mixed).astype(BF16)
        if g >= SGU_PROJECT_LAG:
            _project_slice(act, wsg_ref, acc, g - SGU_PROJECT_LAG)
    for g in range(SGU_GROUPS - SGU_PROJECT_LAG, SGU_GROUPS):
        _project_slice(act, wsg_ref, acc, g)
    merged = _unpack(gr_ref[0]).astype(F32) * _unpack(ret_ref[...]).astype(F32) + _unpack(gs_ref[0]).astype(F32) * acc[...]
    _residual_proj(h_ref, merged.astype(BF16), wo_ref, o_ref)


def _mix_out(u, vs, w_s, b_s, ret_proj, gates, h, w_sgu_out, w_out):
    t, d = h.shape
    gdim = d // SGU_GROUPS
    assert t % OUT_ROWS == 0 and OUT_ROWS % SGU_CHUNK == 0
    rows = lambda: pl.BlockSpec((OUT_ROWS, d), lambda i: (i, 0))
    packed_rows = lambda: pl.BlockSpec((OUT_ROWS // PACK, d), lambda i: (i, 0))
    seg_rows = lambda s: pl.BlockSpec((1, OUT_ROWS // PACK, d), lambda i: (s, i, 0))
    weight = lambda: _resident((d // PACK, d))
    vmem = (2 * _nbytes((d, d), BF16) + 10 * _nbytes((OUT_ROWS, d), BF16) + 4 * _nbytes((OUT_ROWS, d), F32)
            + _nbytes((OUT_ROWS, d), BF16) + 4 * _nbytes((OUT_ROWS, d), F32))
    return pl.pallas_call(
        _mix_out_kernel,
        grid=(t // OUT_ROWS,),
        in_specs=[
            seg_rows(0), seg_rows(0),
            pl.BlockSpec((SGU_GROUPS, SGU_CHUNK, SGU_CHUNK), lambda i: (0, 0, 0)),
            pl.BlockSpec((SGU_GROUPS, SGU_CHUNK, 1), lambda i: (0, 0, 0)),
            packed_rows(), seg_rows(0), seg_rows(1), rows(), weight(), weight(),
        ],
        out_specs=rows(),
        out_shape=jax.ShapeDtypeStruct((t, d), F32),
        scratch_shapes=[pltpu.VMEM((SGU_GROUPS, OUT_ROWS, gdim), BF16), pltpu.VMEM((OUT_ROWS, d), F32)],
        compiler_params=_params(("parallel",), vmem),
        name="mix_out",
    )(u, vs, w_s, b_s, ret_proj, gates, gates, h, w_sgu_out, w_out)


def _mem_kv_kernel(m_ref, nw_ref, w_ref, o_ref, n_scr):
    @pl.when(pl.program_id(1) == 0)
    def _():
        n_scr[...] = _rms(m_ref[...], nw_ref[...]).astype(BF16)

    o_ref[...] = _pack(_dot(n_scr[...], _unpack(w_ref[...])))


def _mem_kv(mem, norm_w, w_kv):
    t, d = mem.shape
    rows = min(KV_ROWS, t)
    n_out = w_kv.shape[1]
    assert t % rows == 0 and n_out % KV_COLS == 0
    vmem = (2 * _nbytes((rows, d), F32) + _nbytes((rows, d), BF16) + 2 * _nbytes((d, KV_COLS), BF16)
            + 2 * _nbytes((rows, KV_COLS), BF16) + 2 * _nbytes((rows, d), F32))
    return pl.pallas_call(
        _mem_kv_kernel,
        grid=(t // rows, n_out // KV_COLS),
        in_specs=[
            pl.BlockSpec((rows, d), lambda i, j: (i, 0)),
            pl.BlockSpec((1, d), lambda i, j: (0, 0)),
            pl.BlockSpec((d // PACK, KV_COLS), lambda i, j: (0, j)),
        ],
        out_specs=pl.BlockSpec((rows // PACK, KV_COLS), lambda i, j: (i, j)),
        out_shape=jax.ShapeDtypeStruct((t // PACK, n_out), U32),
        scratch_shapes=[pltpu.VMEM((rows, d), BF16)],
        compiler_params=_params(("parallel", "arbitrary"), vmem),
        name="mem_kv",
    )(mem, norm_w, w_kv)


def _xattn_kernel(h_ref, nw_ref, kv_ref, wq_ref, wo_ref, next_nw_ref, o_ref, n_out_ref, q_scr, a_scr, acc):
    rows, d = h_ref.shape
    hd = d // XATTN_HEADS
    scale = float(hd) ** -0.5
    for rs, _ in _tiles(rows, 1, rows // 2, 1):
        q_scr[rs, :] = _dot(_rms(h_ref[rs, :], nw_ref[...]).astype(BF16), _unpack(wq_ref[...])).astype(BF16)

    def probs(h):
        sl = slice(h * hd, (h + 1) * hd)
        s = _dot_nt(q_scr[:, sl], _unpack(kv_ref[:, sl])) * scale
        e = jnp.exp(s - jnp.max(s, axis=-1, keepdims=True))
        return (e / jnp.sum(e, axis=-1, keepdims=True)).astype(BF16)

    def emit(rs, normed):
        n_out_ref[_packed(rs), :] = _pack(normed)

    last = XATTN_HEADS - 1
    p = probs(0)
    for h in range(XATTN_HEADS):
        p_next = probs(h + 1) if h < last else None
        a_scr[h] = _dot(p, _unpack(kv_ref[:, d + h * hd:d + (h + 1) * hd])).astype(BF16)
        if h < last:
            _project_slice(a_scr, wo_ref, acc, h)
        else:
            _residual_proj_norm(h_ref, lambda rs: a_scr[last, rs, :], wo_ref, o_ref, None, next_nw_ref[...], emit,
                                w_rows=slice(last * hd // PACK, (last + 1) * hd // PACK), base_ref=acc)
        p = p_next


def _xattn(h, kv, norm_w, w_q, w_o, next_norm_w, batch):
    t, d = h.shape
    n_mem = kv.shape[0] * PACK // batch
    row_blocks = t // batch // XATTN_ROWS
    assert t % (batch * XATTN_ROWS) == 0
    rows = lambda: pl.BlockSpec((XATTN_ROWS, d), lambda b, i: (b * row_blocks + i, 0))
    vec = lambda: pl.BlockSpec((1, d), lambda b, i: (0, 0))
    weight = lambda: _resident((d // PACK, d))
    vmem = (2 * _nbytes((d, d), BF16) + 4 * _nbytes((XATTN_ROWS, d), F32) + 4 * _nbytes((XATTN_ROWS, d), BF16)
            + 2 * _nbytes((n_mem, 2 * d), BF16) + 5 * _nbytes((XATTN_ROWS, d), F32))
    return pl.pallas_call(
        _xattn_kernel,
        grid=(batch, row_blocks),
        in_specs=[rows(), vec(), pl.BlockSpec((n_mem // PACK, 2 * d), lambda b, i: (b, 0)), weight(), weight(), vec()],
        out_specs=(rows(), pl.BlockSpec((XATTN_ROWS // PACK, d), lambda b, i: (b * row_blocks + i, 0))),
        out_shape=(jax.ShapeDtypeStruct((t, d), F32), jax.ShapeDtypeStruct((t // PACK, d), U32)),
        scratch_shapes=[pltpu.VMEM((XATTN_ROWS, d), BF16),
                        pltpu.VMEM((XATTN_HEADS, XATTN_ROWS, d // XATTN_HEADS), BF16),
                        pltpu.VMEM((XATTN_ROWS, d), F32)],
        compiler_params=_params(("parallel", "parallel"), vmem),
        name="xattn",
    )(h, norm_w, kv, w_q, w_o, next_norm_w)


def _rope_tables(seq, half):
    freqs = ROPE_BASE ** (-jnp.linspace(0.0, 1.0, half, dtype=F32))
    ang = jnp.arange(seq, dtype=F32)[:, None] * freqs[None, :]
    return jnp.cos(ang), jnp.sin(ang)


def _trunk(x, mem, w):
    batch, seq, d = x.shape
    t = batch * seq
    x = x.reshape(t, d)
    half = d // RET_HEADS // 2
    cos, sin = _rope_tables(seq, half)
    pos_blocks = seq // IN_ROWS
    assert seq % IN_ROWS == 0
    pos_spec = lambda: pl.BlockSpec((IN_ROWS, half), lambda s, i: (i % pos_blocks, 0))

    hidden = _ffn_up(x, w["ffn1_norm"], w["ffn1_w_gu"], t)
    h, n = _ffn_down(x, hidden, w["ffn1_w_down"], w["mix_norm"], final_norm=False)

    w_in = w["w_in"]
    vec_spec = lambda: pl.BlockSpec((1, d), lambda s, i: (0, 0))
    q = _in_proj(n, w_in, SEG_Q, 1, "rotary", (cos, sin), (pos_spec(), pos_spec()))
    kt = _in_proj(n, w_in, SEG_K, 1, "rotary_transposed", (cos, sin), (pos_spec(), pos_spec()),
                  k_scale=float(2 * half) ** -0.5)
    v = _in_proj(n, w_in, SEG_V, 1, "plain")
    g = _in_proj(n, w_in, SEG_G, 1, "silu_gain", (w["ret_gn_w"],), (vec_spec(),))
    u = _in_proj(n, w_in, SEG_U, 1, "gelu")
    vs = _in_proj(n, w_in, SEG_VS, 1, "gelu_layernorm", (w["sgu_norm_w"],), (vec_spec(),), rows=IN_LN_ROWS)
    gates = _in_proj(n, w_in, SEG_GATE_R, 2, "sigmoid_bias", (w["gate_bias"],),
                     (pl.BlockSpec((2, d), lambda s, i: (0, 0)),))

    sb = _ret_bwd_states(kt, v, w["ret_decay_bwd"], batch)
    ret_proj = _ret_out(q, kt, v, g, sb, w["ret_decay_fwd"], w["ret_decay_bwd"], w["w_ret_out"], batch)
    h = _mix_out(u, vs, w["sgu_w_s"], w["sgu_b_s"], ret_proj, gates, h, w["w_sgu_out"], w["w_out"])

    kv = _mem_kv(mem.reshape(-1, d), w["xattn_norm_mem"], w["xattn_w_kv"])
    h, n = _xattn(h, kv, w["xattn_norm_q"], w["xattn_w_q"], w["xattn_w_o"], w["ffn2_norm"], batch)

    hidden = _ffn_up(n, None, w["ffn2_w_gu"], t)
    y = _ffn_down(h, hidden, w["ffn2_w_down"], w["final_norm"], final_norm=True)
    return y.reshape(batch, seq, d)


def _prepare(ffn1_norm, ffn1_w_gu, ffn1_w_down, mix_norm, w_in, gate_bias, ret_decay_fwd, ret_decay_bwd, ret_gn_w, w_ret_out, sgu_norm_w, sgu_w_s, sgu_b_s, w_sgu_out, w_out, xattn_norm_q, xattn_norm_mem, xattn_w_q, xattn_w_kv, xattn_w_o, ffn2_norm, ffn2_w_gu, ffn2_w_down, final_norm):
    assert ffn1_norm.shape[0] == 1, "single-layer trunk"
    pw = lambda w: _pack_weight(w[0])
    return {
        "ffn1_norm": ffn1_norm, "ffn1_w_gu": _pack_gate_up(ffn1_w_gu[0]), "ffn1_w_down": pw(ffn1_w_down),
        "mix_norm": mix_norm, "w_in": pw(w_in), "gate_bias": gate_bias[0],
        "ret_decay_fwd": ret_decay_fwd[0], "ret_decay_bwd": ret_decay_bwd[0], "ret_gn_w": ret_gn_w,
        "w_ret_out": pw(w_ret_out), "sgu_norm_w": sgu_norm_w,
        "sgu_w_s": sgu_w_s[0].astype(BF16), "sgu_b_s": sgu_b_s[0][:, :, None],
        "w_sgu_out": pw(w_sgu_out), "w_out": pw(w_out),
        "xattn_norm_q": xattn_norm_q, "xattn_norm_mem": xattn_norm_mem,
        "xattn_w_q": pw(xattn_w_q), "xattn_w_kv": pw(xattn_w_kv), "xattn_w_o": pw(xattn_w_o),
        "ffn2_norm": ffn2_norm, "ffn2_w_gu": _pack_gate_up(ffn2_w_gu[0]), "ffn2_w_down": pw(ffn2_w_down),
        "final_norm": final_norm[None, :],
    }


def kernel(x_prompt, x_sample, mem_prompt, mem_sample, ffn1_norm, ffn1_w_gu, ffn1_w_down, mix_norm, w_in, gate_bias, ret_decay_fwd, ret_decay_bwd, ret_gn_w, w_ret_out, sgu_norm_w, sgu_w_s, sgu_b_s, w_sgu_out, w_out, xattn_norm_q, xattn_norm_mem, xattn_w_q, xattn_w_kv, xattn_w_o, ffn2_norm, ffn2_w_gu, ffn2_w_down, final_norm):
    w = _prepare(ffn1_norm, ffn1_w_gu, ffn1_w_down, mix_norm, w_in, gate_bias, ret_decay_fwd, ret_decay_bwd, ret_gn_w, w_ret_out, sgu_norm_w, sgu_w_s, sgu_b_s, w_sgu_out, w_out, xattn_norm_q, xattn_norm_mem, xattn_w_q, xattn_w_kv, xattn_w_o, ffn2_norm, ffn2_w_gu, ffn2_w_down, final_norm)
    return _trunk(x_prompt, mem_prompt, w), _trunk(x_sample, mem_sample, w)
```

```python
import functools
import math

import jax
import jax.numpy as jnp
from jax import lax
from jax.experimental import pallas as pl
from jax.experimental.pallas import tpu as pltpu

F32 = jnp.float32
BF16 = jnp.bfloat16
U32 = jnp.uint32
PACK = 2

EPS = 1e-6
FFN_RES_SCALE = 0.5
ROPE_BASE = 10000.0
RET_HEADS = 8
SGU_GROUPS = 8
SGU_CHUNK = 128
XATTN_HEADS = 4
SEG_Q, SEG_K, SEG_V, SEG_G, SEG_U, SEG_VS, SEG_GATE_R, SEG_GATE_S = range(8)

V7X_VMEM_BYTES = 64 * 1024 * 1024
V7X_MXU_COLS = 256

RET_BLOCK = 256
RET_BWD_CHUNKS_PER_STEP = 4
RET_PROJECT_LAG = 2
SGU_PROJECT_LAG = 1

PACK_ROWS = 512
PACK_VMEM_RESERVATION = V7X_VMEM_BYTES - 8 * 1024 * 1024
FFN_UP_ROWS = 1024
FFN_UP_COLS = 1408
GATE_UP_LANES = 128
FFN_DOWN_ROWS = 512
IN_ROWS = 2048
IN_LN_ROWS = 1024
LN_ROW_TILE = 256
OUT_ROWS = 256
KV_ROWS = 512
KV_COLS = 1024
XATTN_ROWS = 512
RESIDUAL_COL_TILE = 512
MATMUL_ROW_TILE = 512


def _nbytes(shape, dtype):
    return math.prod(shape) * jnp.dtype(dtype).itemsize


def _params(semantics, vmem_bytes):
    assert vmem_bytes <= V7X_VMEM_BYTES, vmem_bytes
    return pltpu.CompilerParams(dimension_semantics=semantics, vmem_limit_bytes=int(vmem_bytes))


def _resident(shape):
    return pl.BlockSpec(shape, lambda *_: (0,) * len(shape), pipeline_mode=pl.Buffered(1))


def _unpack(words):
    return pltpu.bitcast(words, BF16)


def _pack(vals):
    return pltpu.bitcast(vals.astype(BF16), U32)


def _packed(rows):
    return slice(rows.start // PACK, rows.stop // PACK)


def _tiles(n_rows, n_cols, row_tile, col_tile):
    for r in range(n_rows // row_tile):
        for c in range(n_cols // col_tile):
            yield slice(r * row_tile, (r + 1) * row_tile), slice(c * col_tile, (c + 1) * col_tile)


def _rms(x, w):
    return x * lax.rsqrt(jnp.mean(x * x, axis=-1, keepdims=True) + EPS) * w


def _dot(a, b):
    return jnp.dot(a, b, preferred_element_type=F32)


def _dot_nt(a, b):
    return lax.dot_general(a, b, (((1,), (1,)), ((), ())), preferred_element_type=F32)


def _sigmoid(x):
    return 0.5 * jnp.tanh(0.5 * x) + 0.5


def _gelu_tanh(x):
    c = math.sqrt(2.0 / math.pi)
    return x * (0.5 * jnp.tanh(x * (c + (c * 0.044715) * (x * x))) + 0.5)


def _residual_proj(x_ref, a, w_ref, o_ref):
    for _, cs in _tiles(1, o_ref.shape[1], 1, RESIDUAL_COL_TILE):
        o_ref[:, cs] = x_ref[:, cs] + _dot(a, _unpack(w_ref[:, cs]))


def _residual_proj_norm(x_ref, lhs, w_ref, o_ref, scale, norm_w, emit, w_rows=slice(None), base_ref=None):
    rows, d = o_ref.shape
    for rs, _ in _tiles(rows, 1, rows // 2, 1):
        a = lhs(rs)
        ss = jnp.zeros((a.shape[0], 1), F32)
        for _, cs in _tiles(1, d, 1, RESIDUAL_COL_TILE):
            y = _dot(a, _unpack(w_ref[w_rows, cs]))
            if base_ref is not None:
                y = base_ref[rs, cs] + y
            o = x_ref[rs, cs] + (y if scale is None else scale * y)
            o_ref[rs, cs] = o
            ss += jnp.sum(o * o, axis=-1, keepdims=True)
        emit(rs, o_ref[rs, :] * lax.rsqrt(ss * (1.0 / d) + EPS) * norm_w)


def _pack_kernel(w_ref, o_ref):
    o_ref[...] = _pack(w_ref[...])


def _pack_weight(w):
    k, n = w.shape
    rows = min(PACK_ROWS, k)
    cols = next(c for c in (2048, 1408, 1024, 512, 256, 128) if n % c == 0)
    assert k % rows == 0 and rows % (8 * PACK) == 0
    return pl.pallas_call(
        _pack_kernel,
        grid=(k // rows, n // cols),
        in_specs=[pl.BlockSpec((rows, cols), lambda i, j: (i, j))],
        out_specs=pl.BlockSpec((rows // PACK, cols), lambda i, j: (i, j)),
        out_shape=jax.ShapeDtypeStruct((k // PACK, n), U32),
        compiler_params=_params(("parallel", "parallel"), PACK_VMEM_RESERVATION),
        name="pack_weight",
    )(w)


def _pack_gate_up_kernel(g_ref, u_ref, o_ref):
    for _, cs in _tiles(1, g_ref.shape[1], 1, GATE_UP_LANES):
        o_ref[:, 2 * cs.start:2 * cs.start + GATE_UP_LANES] = _pack(g_ref[:, cs])
        o_ref[:, 2 * cs.start + GATE_UP_LANES:2 * cs.stop] = _pack(u_ref[:, cs])


def _pack_gate_up(w_gu):
    k, n = w_gu.shape
    rows = min(PACK_ROWS, k)
    n_col_steps = n // 2 // FFN_UP_COLS
    assert n % (2 * FFN_UP_COLS) == 0 and k % rows == 0
    return pl.pallas_call(
        _pack_gate_up_kernel,
        grid=(k // rows, n_col_steps),
        in_specs=[pl.BlockSpec((rows, FFN_UP_COLS), lambda i, j: (i, j)),
                  pl.BlockSpec((rows, FFN_UP_COLS), lambda i, j: (i, j + n_col_steps))],
        out_specs=pl.BlockSpec((rows // PACK, 2 * FFN_UP_COLS), lambda i, j: (i, j)),
        out_shape=jax.ShapeDtypeStruct((k // PACK, n), U32),
        compiler_params=_params(("parallel", "parallel"), PACK_VMEM_RESERVATION),
        name="pack_gate_up",
    )(w_gu, w_gu)


def _ffn_up_kernel(*refs, norm_input):
    if norm_input:
        x_ref, nw_ref, w_ref, o_ref, n_scr = refs

        @pl.when(pl.program_id(1) == 0)
        def _():
            n_scr[...] = _rms(x_ref[...], nw_ref[...]).astype(BF16)

        lhs = lambda rs: n_scr[rs, :]
    else:
        n_ref, w_ref, o_ref = refs
        lhs = lambda rs: _unpack(n_ref[_packed(rs), :])
    for rs, cs in _tiles(o_ref.shape[0] * PACK, w_ref.shape[1], MATMUL_ROW_TILE, 2 * GATE_UP_LANES):
        z = _dot(lhs(rs), _unpack(w_ref[:, cs]))
        g, u = z[:, :GATE_UP_LANES], z[:, GATE_UP_LANES:]
        o_ref[_packed(rs), cs.start // 2:cs.stop // 2] = _pack(g * _sigmoid(g) * u)


def _ffn_up(x_or_n, norm_w, w_gu, t):
    d = x_or_n.shape[1]
    d_ff = w_gu.shape[1] // 2
    assert t % FFN_UP_ROWS == 0 and d_ff % FFN_UP_COLS == 0 and FFN_UP_COLS % GATE_UP_LANES == 0
    norm_input = norm_w is not None
    w_spec = pl.BlockSpec((d // PACK, 2 * FFN_UP_COLS), lambda i, j: (0, j))
    if norm_input:
        in_specs = [pl.BlockSpec((FFN_UP_ROWS, d), lambda i, j: (i, 0)), pl.BlockSpec((1, d), lambda i, j: (0, 0)), w_spec]
        args = (x_or_n, norm_w, w_gu)
        scratch = [pltpu.VMEM((FFN_UP_ROWS, d), BF16)]
    else:
        in_specs = [pl.BlockSpec((FFN_UP_ROWS // PACK, d), lambda i, j: (i, 0)), w_spec]
        args = (x_or_n, w_gu)
        scratch = []
    vmem = (2 * _nbytes((FFN_UP_ROWS, d), F32 if norm_input else BF16) + _nbytes((FFN_UP_ROWS, d), BF16)
            + 4 * _nbytes((d, FFN_UP_COLS), BF16) + 2 * _nbytes((FFN_UP_ROWS, FFN_UP_COLS), BF16)
            + 8 * _nbytes((MATMUL_ROW_TILE, 2 * GATE_UP_LANES), F32))
    return pl.pallas_call(
        functools.partial(_ffn_up_kernel, norm_input=norm_input),
        grid=(t // FFN_UP_ROWS, d_ff // FFN_UP_COLS),
        in_specs=in_specs,
        out_specs=pl.BlockSpec((FFN_UP_ROWS // PACK, FFN_UP_COLS), lambda i, j: (i, j)),
        out_shape=jax.ShapeDtypeStruct((t // PACK, d_ff), U32),
        scratch_shapes=scratch,
        compiler_params=_params(("parallel", "arbitrary"), vmem),
        name="ffn_up_norm" if norm_input else "ffn_up",
    )(*args)


def _ffn_down_kernel(x_ref, hid_ref, wd_ref, nw_ref, o_ref, *n_out, final_norm):
    def emit(rs, normed):
        if final_norm:
            o_ref[rs, :] = normed
        else:
            n_out[0][_packed(rs), :] = _pack(normed)

    _residual_proj_norm(x_ref, lambda rs: _unpack(hid_ref[_packed(rs), :]), wd_ref, o_ref, FFN_RES_SCALE,
                        nw_ref[...], emit)


def _ffn_down(x, hidden, w_down, norm_w, final_norm):
    t, d = x.shape
    d_ff = hidden.shape[1]
    assert t % FFN_DOWN_ROWS == 0
    row_spec = lambda: pl.BlockSpec((FFN_DOWN_ROWS, d), lambda i: (i, 0))
    packed_spec = lambda cols: pl.BlockSpec((FFN_DOWN_ROWS // PACK, cols), lambda i: (i, 0))
    vmem = (4 * _nbytes((FFN_DOWN_ROWS, d), F32) + 2 * _nbytes((FFN_DOWN_ROWS, d_ff), BF16)
            + _nbytes((d_ff, d), BF16) + 2 * _nbytes((FFN_DOWN_ROWS, d), BF16)
            + 4 * _nbytes((FFN_DOWN_ROWS, RESIDUAL_COL_TILE), F32))
    out_f32 = jax.ShapeDtypeStruct((t, d), F32)
    return pl.pallas_call(
        functools.partial(_ffn_down_kernel, final_norm=final_norm),
        grid=(t // FFN_DOWN_ROWS,),
        in_specs=[row_spec(), packed_spec(d_ff), _resident((d_ff // PACK, d)), pl.BlockSpec((1, d), lambda i: (0, 0))],
        out_specs=row_spec() if final_norm else (row_spec(), packed_spec(d)),
        out_shape=out_f32 if final_norm else (out_f32, jax.ShapeDtypeStruct((t // PACK, d), U32)),
        compiler_params=_params(("parallel",), vmem),
        name="ffn_down_final" if final_norm else "ffn_down",
    )(x, hidden, w_down, norm_w)


def _in_proj_kernel(n_ref, w_ref, *refs, kind, k_scale):
    seg = pl.program_id(0)
    rows, d = n_ref.shape[0] * PACK, n_ref.shape[1]
    o_ref = refs[1] if kind == "gelu_layernorm" else refs[-1]

    def z_tiles():
        for rs, cs in _tiles(rows, d, MATMUL_ROW_TILE, V7X_MXU_COLS):
            yield rs, cs, _dot(_unpack(n_ref[_packed(rs), :]), _unpack(w_ref[:, cs]))

    def store(rs, cs, val):
        o_ref[0, _packed(rs), cs] = _pack(val)

    if kind in ("rotary", "rotary_transposed"):
        cos_ref, sin_ref, _ = refs
        half = V7X_MXU_COLS // 2
        for rs, cs, z in z_tiles():
            x1, x2 = z[:, :half], z[:, half:]
            cos, sin = cos_ref[rs, :], sin_ref[rs, :]
            if kind == "rotary":
                store(rs, slice(cs.start, cs.start + half), x1 * cos - x2 * sin)
                store(rs, slice(cs.start + half, cs.stop), x1 * sin + x2 * cos)
            else:
                rot = jnp.concatenate([x1 * cos - x2 * sin, x1 * sin + x2 * cos], axis=1) * k_scale
                rot_t = _pack(rot.T)
                for j in range(MATMUL_ROW_TILE // RET_BLOCK):
                    o_ref[rs.start // RET_BLOCK + j, cs.start // V7X_MXU_COLS] = rot_t[:, j * RET_BLOCK:(j + 1) * RET_BLOCK]
    elif kind == "plain":
        for rs, cs, z in z_tiles():
            store(rs, cs, z)
    elif kind == "silu_gain":
        gain_ref, _ = refs
        for rs, cs, z in z_tiles():
            store(rs, cs, z * _sigmoid(z) * gain_ref[:, cs])
    elif kind == "gelu":
        for rs, cs, z in z_tiles():
            store(rs, cs, _gelu_tanh(z))
    elif kind == "gelu_layernorm":
        lnw_ref, _, a_scr = refs
        for rs, _ in _tiles(rows, 1, LN_ROW_TILE, 1):
            for _, cs in _tiles(1, d, 1, V7X_MXU_COLS):
                a_scr[rs, cs] = _gelu_tanh(_dot(_unpack(n_ref[_packed(rs), :]), _unpack(w_ref[:, cs])))
            a = a_scr[rs, :]
            cen = a - jnp.mean(a, axis=-1, keepdims=True)
            var = jnp.mean(cen * cen, axis=-1, keepdims=True)
            o_ref[0, _packed(rs), :] = _pack(cen * lax.rsqrt(var + EPS) * lnw_ref[...])
    elif kind == "sigmoid_bias":
        gb_ref, _ = refs
        for rs, cs, z in z_tiles():
            store(rs, cs, _sigmoid(z + gb_ref[pl.ds(seg, 1), cs]))
    else:
        raise ValueError(kind)


def _in_proj(n, w_in, first_seg, n_seg, kind, extra=(), extra_specs=(), rows=IN_ROWS, k_scale=None):
    t, d = n.shape[0] * PACK, n.shape[1]
    assert t % rows == 0 and w_in.shape[1] % d == 0
    scratch = [pltpu.VMEM((rows, d), F32)] if kind == "gelu_layernorm" else []
    vmem = (4 * _nbytes((rows, d), BF16) + 2 * _nbytes((d, d), BF16) + 4 * _nbytes((rows, d // RET_HEADS // 2), F32)
            + (2 * _nbytes((rows, d), F32) if scratch else 0) + 16 * _nbytes((MATMUL_ROW_TILE, V7X_MXU_COLS), F32))
    if kind == "rotary_transposed":
        assert n_seg == 1
        head = d // RET_HEADS
        out_spec = pl.BlockSpec((rows // RET_BLOCK, RET_HEADS, head // PACK, RET_BLOCK), lambda s, i: (i, 0, 0, 0))
        out_shape = jax.ShapeDtypeStruct((t // RET_BLOCK, RET_HEADS, head // PACK, RET_BLOCK), U32)
    else:
        out_spec = pl.BlockSpec((1, rows // PACK, d), lambda s, i: (s, i, 0))
        out_shape = jax.ShapeDtypeStruct((n_seg, t // PACK, d), U32)
    return pl.pallas_call(
        functools.partial(_in_proj_kernel, kind=kind, k_scale=k_scale),
        grid=(n_seg, t // rows),
        in_specs=[pl.BlockSpec((rows // PACK, d), lambda s, i: (i, 0)),
                  pl.BlockSpec((d // PACK, d), lambda s, i: (0, first_seg + s))] + list(extra_specs),
        out_specs=out_spec,
        out_shape=out_shape,
        scratch_shapes=scratch,
        compiler_params=_params(("parallel", "parallel"), vmem),
        name="in_proj_" + kind,
    )(n, w_in, *extra)


def _decay_rate(dec_ref, h, shape):
    return -jnp.exp(jnp.full(shape, dec_ref[h], F32))


def _ret_bwd_state_kernel(decb_ref, kt_ref, v_ref, sb_ref, state, kdec, cdec):
    c = pl.program_id(1)
    blk = kt_ref.shape[3]
    head = v_ref.shape[2] // RET_HEADS

    @pl.when((pl.program_id(0) == 0) & (c == 0))
    def _():
        col = lax.broadcasted_iota(jnp.int32, (1, blk), 1).astype(F32)
        for h in range(RET_HEADS):
            kdec[h] = jnp.exp(_decay_rate(decb_ref, h, (1, blk)) * col)
            cdec[h] = jnp.exp(_decay_rate(decb_ref, h, (1, head)) * float(blk))

    @pl.when(c == 0)
    def _():
        state[...] = jnp.zeros_like(state)

    for j in reversed(range(kt_ref.shape[0])):
        rows = slice(j * blk // PACK, (j + 1) * blk // PACK)
        for h in range(RET_HEADS):
            sl = slice(h * head, (h + 1) * head)
            st = state[h]
            sb_ref[0, j, h] = _pack(st)
            kd = (_unpack(kt_ref[j, h]).astype(F32) * kdec[h]).astype(BF16)
            state[h] = st * cdec[h] + _dot(kd, _unpack(v_ref[0, rows, sl]))


def _ret_bwd_states(kt, v, dec_b, batch):
    _, t_packed, d = v.shape
    n_blk = t_packed * PACK // batch // RET_BLOCK
    head = d // RET_HEADS
    per_step = RET_BWD_CHUNKS_PER_STEP
    n_steps = n_blk // per_step
    assert n_blk % per_step == 0
    blk_bytes = per_step * _nbytes((RET_BLOCK, d), BF16)
    state_bytes = _nbytes((RET_HEADS, head, head), F32)
    return pl.pallas_call(
        _ret_bwd_state_kernel,
        grid=(batch, n_steps),
        in_specs=[
            pl.BlockSpec(memory_space=pltpu.SMEM),
            pl.BlockSpec((per_step, RET_HEADS, head // PACK, RET_BLOCK),
                         lambda b, c: (b * n_steps + (n_steps - 1 - c), 0, 0, 0)),
            pl.BlockSpec((1, per_step * RET_BLOCK // PACK, d), lambda b, c: (0, b * n_steps + (n_steps - 1 - c), 0)),
        ],
        out_specs=pl.BlockSpec((1, per_step, RET_HEADS, head // PACK, head), lambda b, c: (b, n_steps - 1 - c, 0, 0, 0)),
        out_shape=jax.ShapeDtypeStruct((batch, n_blk, RET_HEADS, head // PACK, head), U32),
        scratch_shapes=[
            pltpu.VMEM((RET_HEADS, head, head), F32),
            pltpu.VMEM((RET_HEADS, 1, RET_BLOCK), F32),
            pltpu.VMEM((RET_HEADS, 1, head), F32),
        ],
        compiler_params=_params(("arbitrary", "arbitrary"), 6 * blk_bytes + 4 * state_bytes),
        name="ret_bwd_states",
    )(dec_b, kt, v)


def _project_slice(act_ref, w_ref, acc_ref, i):
    width = act_ref.shape[2]
    w_rows = slice(i * width // PACK, (i + 1) * width // PACK)
    for _, cs in _tiles(1, acc_ref.shape[1], 1, RESIDUAL_COL_TILE):
        part = _dot(act_ref[i], _unpack(w_ref[w_rows, cs]))
        acc_ref[:, cs] = part if i == 0 else acc_ref[:, cs] + part


def _ret_out_kernel(decf_ref, decb_ref, q_ref, kt_ref, v_ref, g_ref, sb_ref, wr_ref, o_ref,
                    state, dmat, qdf, qdb, kdf, cdf, act, acc):
    c = pl.program_id(1)
    blk = q_ref.shape[1] * PACK
    head = q_ref.shape[2] // RET_HEADS

    @pl.when((pl.program_id(0) == 0) & (c == 0))
    def _():
        row = lax.broadcasted_iota(jnp.int32, (blk, head), 0).astype(F32)
        col = lax.broadcasted_iota(jnp.int32, (1, blk), 1).astype(F32)
        ri = lax.broadcasted_iota(jnp.int32, (blk, blk), 0)
        ci = lax.broadcasted_iota(jnp.int32, (blk, blk), 1)
        dist = (ri - ci).astype(F32)
        for h in range(RET_HEADS):
            qdf[h] = jnp.exp(_decay_rate(decf_ref, h, (blk, head)) * (row + 1.0))
            qdb[h] = jnp.exp(_decay_rate(decb_ref, h, (blk, head)) * (float(blk) - row))
            kdf[h] = jnp.exp(_decay_rate(decf_ref, h, (1, blk)) * (float(blk) - 1.0 - col))
            cdf[h] = jnp.exp(_decay_rate(decf_ref, h, (1, head)) * float(blk))
            lf2 = _decay_rate(decf_ref, h, (blk, blk))
            lb2 = _decay_rate(decb_ref, h, (blk, blk))
            dmat[h] = jnp.where(ri >= ci, jnp.exp(lf2 * jnp.maximum(dist, 0.0)), jnp.exp(lb2 * jnp.maximum(-dist, 0.0)))

    @pl.when(c == 0)
    def _():
        state[...] = jnp.zeros_like(state)

    for h in range(RET_HEADS):
        sl = slice(h * head, (h + 1) * head)
        q = _unpack(q_ref[0, :, sl])
        kt = _unpack(kt_ref[0, h])
        v = _unpack(v_ref[0, :, sl])
        st = state[h]
        scores = _dot(q, kt)
        cross = _dot(q, st.astype(BF16)) * qdf[h] + _dot(q, _unpack(sb_ref[0, 0, h])) * qdb[h]
        state[h] = st * cdf[h] + _dot((kt.astype(F32) * kdf[h]).astype(BF16), v)
        o = _dot((scores * dmat[h]).astype(BF16), v) + cross
        cen = o - jnp.mean(o, axis=-1, keepdims=True)
        var = jnp.mean(cen * cen, axis=-1, keepdims=True)
        act[h] = (_unpack(g_ref[0, :, sl]).astype(F32) * (cen * lax.rsqrt(var + EPS))).astype(BF16)
        if h >= RET_PROJECT_LAG:
            _project_slice(act, wr_ref, acc, h - RET_PROJECT_LAG)
    for h in range(RET_HEADS - RET_PROJECT_LAG, RET_HEADS):
        _project_slice(act, wr_ref, acc, h)
    o_ref[...] = _pack(acc[...])


def _ret_out(q, kt, v, g, sb, dec_f, dec_b, w_ret_out, batch):
    _, t_packed, d = q.shape
    n_blk = t_packed * PACK // batch // RET_BLOCK
    head = d // RET_HEADS
    chunk = lambda: pl.BlockSpec((1, RET_BLOCK // PACK, d), lambda b, c: (0, b * n_blk + c, 0))
    blk_bytes = _nbytes((RET_BLOCK, d), BF16)
    table_bytes = _nbytes((RET_HEADS, RET_BLOCK, head), F32)
    return pl.pallas_call(
        _ret_out_kernel,
        grid=(batch, n_blk),
        in_specs=[
            pl.BlockSpec(memory_space=pltpu.SMEM),
            pl.BlockSpec(memory_space=pltpu.SMEM),
            chunk(),
            pl.BlockSpec((1, RET_HEADS, head // PACK, RET_BLOCK), lambda b, c: (b * n_blk + c, 0, 0, 0)),
            chunk(), chunk(),
            pl.BlockSpec((1, 1, RET_HEADS, head // PACK, head), lambda b, c: (b, c, 0, 0, 0)),
            _resident((d // PACK, d)),
        ],
        out_specs=pl.BlockSpec((RET_BLOCK // PACK, d), lambda b, c: (b * n_blk + c, 0)),
        out_shape=jax.ShapeDtypeStruct((t_packed, d), U32),
        scratch_shapes=[
            pltpu.VMEM((RET_HEADS, head, head), F32),
            pltpu.VMEM((RET_HEADS, RET_BLOCK, RET_BLOCK), F32),
            pltpu.VMEM((RET_HEADS, RET_BLOCK, head), F32),
            pltpu.VMEM((RET_HEADS, RET_BLOCK, head), F32),
            pltpu.VMEM((RET_HEADS, 1, RET_BLOCK), F32),
            pltpu.VMEM((RET_HEADS, 1, head), F32),
            pltpu.VMEM((RET_HEADS, RET_BLOCK, head), BF16),
            pltpu.VMEM((RET_BLOCK, d), F32),
        ],
        compiler_params=_params(("arbitrary", "arbitrary"),
                                12 * blk_bytes + 6 * table_bytes + _nbytes((d, d), BF16) + 4 * _nbytes((RET_BLOCK, d), F32)),
        name="ret_out",
    )(dec_f, dec_b, q, kt, v, g, sb, w_ret_out)


def _mix_out_kernel(u_ref, vs_ref, ws_ref, bs_ref, ret_ref, gr_ref, gs_ref, h_ref, wsg_ref, wo_ref, o_ref, act, acc):
    rows, d = h_ref.shape
    gdim = d // SGU_GROUPS
    for g in range(SGU_GROUPS):
        cs = slice(g * gdim, (g + 1) * gdim)
        for rs, _ in _tiles(rows, 1, SGU_CHUNK, 1):
            mixed = _dot(ws_ref[g], _unpack(vs_ref[0, _packed(rs), cs])) + bs_ref[g]
            act[g, rs, :] = (_unpack(u_ref[0, _packed(rs), cs]).astype(F32) * mixed).astype(BF16)
        if g >= SGU_PROJECT_LAG:
            _project_slice(act, wsg_ref, acc, g - SGU_PROJECT_LAG)
    for g in range(SGU_GROUPS - SGU_PROJECT_LAG, SGU_GROUPS):
        _project_slice(act, wsg_ref, acc, g)
    merged = _unpack(gr_ref[0]).astype(F32) * _unpack(ret_ref[...]).astype(F32) + _unpack(gs_ref[0]).astype(F32) * acc[...]
    _residual_proj(h_ref, merged.astype(BF16), wo_ref, o_ref)


def _mix_out(u, vs, w_s, b_s, ret_proj, gates, h, w_sgu_out, w_out):
    t, d = h.shape
    gdim = d // SGU_GROUPS
    assert t % OUT_ROWS == 0 and OUT_ROWS % SGU_CHUNK == 0
    rows = lambda: pl.BlockSpec((OUT_ROWS, d), lambda i: (i, 0))
    packed_rows = lambda: pl.BlockSpec((OUT_ROWS // PACK, d), lambda i: (i, 0))
    seg_rows = lambda s: pl.BlockSpec((1, OUT_ROWS // PACK, d), lambda i: (s, i, 0))
    weight = lambda: _resident((d // PACK, d))
    vmem = (2 * _nbytes((d, d), BF16) + 10 * _nbytes((OUT_ROWS, d), BF16) + 4 * _nbytes((OUT_ROWS, d), F32)
            + _nbytes((OUT_ROWS, d), BF16) + 4 * _nbytes((OUT_ROWS, d), F32))
    return pl.pallas_call(
        _mix_out_kernel,
        grid=(t // OUT_ROWS,),
        in_specs=[
            seg_rows(0), seg_rows(0),
            pl.BlockSpec((SGU_GROUPS, SGU_CHUNK, SGU_CHUNK), lambda i: (0, 0, 0)),
            pl.BlockSpec((SGU_GROUPS, SGU_CHUNK, 1), lambda i: (0, 0, 0)),
            packed_rows(), seg_rows(0), seg_rows(1), rows(), weight(), weight(),
        ],
        out_specs=rows(),
        out_shape=jax.ShapeDtypeStruct((t, d), F32),
        scratch_shapes=[pltpu.VMEM((SGU_GROUPS, OUT_ROWS, gdim), BF16), pltpu.VMEM((OUT_ROWS, d), F32)],
        compiler_params=_params(("parallel",), vmem),
        name="mix_out",
    )(u, vs, w_s, b_s, ret_proj, gates, gates, h, w_sgu_out, w_out)


def _mem_kv_kernel(m_ref, nw_ref, w_ref, o_ref, n_scr):
    @pl.when(pl.program_id(1) == 0)
    def _():
        n_scr[...] = _rms(m_ref[...], nw_ref[...]).astype(BF16)

    o_ref[...] = _pack(_dot(n_scr[...], _unpack(w_ref[...])))


def _mem_kv(mem, norm_w, w_kv):
    t, d = mem.shape
    rows = min(KV_ROWS, t)
    n_out = w_kv.shape[1]
    assert t % rows == 0 and n_out % KV_COLS == 0
    vmem = (2 * _nbytes((rows, d), F32) + _nbytes((rows, d), BF16) + 2 * _nbytes((d, KV_COLS), BF16)
            + 2 * _nbytes((rows, KV_COLS), BF16) + 2 * _nbytes((rows, d), F32))
    return pl.pallas_call(
        _mem_kv_kernel,
        grid=(t // rows, n_out // KV_COLS),
        in_specs=[
            pl.BlockSpec((rows, d), lambda i, j: (i, 0)),
            pl.BlockSpec((1, d), lambda i, j: (0, 0)),
            pl.BlockSpec((d // PACK, KV_COLS), lambda i, j: (0, j)),
        ],
        out_specs=pl.BlockSpec((rows // PACK, KV_COLS), lambda i, j: (i, j)),
        out_shape=jax.ShapeDtypeStruct((t // PACK, n_out), U32),
        scratch_shapes=[pltpu.VMEM((rows, d), BF16)],
        compiler_params=_params(("parallel", "arbitrary"), vmem),
        name="mem_kv",
    )(mem, norm_w, w_kv)


def _xattn_kernel(h_ref, nw_ref, kv_ref, wq_ref, wo_ref, next_nw_ref, o_ref, n_out_ref, q_scr, a_scr, acc):
    rows, d = h_ref.shape
    hd = d // XATTN_HEADS
    scale = float(hd) ** -0.5
    for rs, _ in _tiles(rows, 1, rows // 2, 1):
        q_scr[rs, :] = _dot(_rms(h_ref[rs, :], nw_ref[...]).astype(BF16), _unpack(wq_ref[...])).astype(BF16)

    def probs(h):
        sl = slice(h * hd, (h + 1) * hd)
        s = _dot_nt(q_scr[:, sl], _unpack(kv_ref[:, sl])) * scale
        e = jnp.exp(s - jnp.max(s, axis=-1, keepdims=True))
        return (e / jnp.sum(e, axis=-1, keepdims=True)).astype(BF16)

    def emit(rs, normed):
        n_out_ref[_packed(rs), :] = _pack(normed)

    last = XATTN_HEADS - 1
    p = probs(0)
    for h in range(XATTN_HEADS):
        p_next = probs(h + 1) if h < last else None
        a_scr[h] = _dot(p, _unpack(kv_ref[:, d + h * hd:d + (h + 1) * hd])).astype(BF16)
        if h < last:
            _project_slice(a_scr, wo_ref, acc, h)
        else:
            _residual_proj_norm(h_ref, lambda rs: a_scr[last, rs, :], wo_ref, o_ref, None, next_nw_ref[...], emit,
                                w_rows=slice(last * hd // PACK, (last + 1) * hd // PACK), base_ref=acc)
        p = p_next


def _xattn(h, kv, norm_w, w_q, w_o, next_norm_w, batch):
    t, d = h.shape
    n_mem = kv.shape[0] * PACK // batch
    row_blocks = t // batch // XATTN_ROWS
    assert t % (batch * XATTN_ROWS) == 0
    rows = lambda: pl.BlockSpec((XATTN_ROWS, d), lambda b, i: (b * row_blocks + i, 0))
    vec = lambda: pl.BlockSpec((1, d), lambda b, i: (0, 0))
    weight = lambda: _resident((d // PACK, d))
    vmem = (2 * _nbytes((d, d), BF16) + 4 * _nbytes((XATTN_ROWS, d), F32) + 4 * _nbytes((XATTN_ROWS, d), BF16)
            + 2 * _nbytes((n_mem, 2 * d), BF16) + 5 * _nbytes((XATTN_ROWS, d), F32))
    return pl.pallas_call(
        _xattn_kernel,
        grid=(batch, row_blocks),
        in_specs=[rows(), vec(), pl.BlockSpec((n_mem // PACK, 2 * d), lambda b, i: (b, 0)), weight(), weight(), vec()],
        out_specs=(rows(), pl.BlockSpec((XATTN_ROWS // PACK, d), lambda b, i: (b * row_blocks + i, 0))),
        out_shape=(jax.ShapeDtypeStruct((t, d), F32), jax.ShapeDtypeStruct((t // PACK, d), U32)),
        scratch_shapes=[pltpu.VMEM((XATTN_ROWS, d), BF16),
                        pltpu.VMEM((XATTN_HEADS, XATTN_ROWS, d // XATTN_HEADS), BF16),
                        pltpu.VMEM((XATTN_ROWS, d), F32)],
        compiler_params=_params(("parallel", "parallel"), vmem),
        name="xattn",
    )(h, norm_w, kv, w_q, w_o, next_norm_w)


def _rope_tables(seq, half):
    freqs = ROPE_BASE ** (-jnp.linspace(0.0, 1.0, half, dtype=F32))
    ang = jnp.arange(seq, dtype=F32)[:, None] * freqs[None, :]
    return jnp.cos(ang), jnp.sin(ang)


def _trunk(x, mem, w):
    batch, seq, d = x.shape
    t = batch * seq
    x = x.reshape(t, d)
    half = d // RET_HEADS // 2
    cos, sin = _rope_tables(seq, half)
    pos_blocks = seq // IN_ROWS
    assert seq % IN_ROWS == 0
    pos_spec = lambda: pl.BlockSpec((IN_ROWS, half), lambda s, i: (i % pos_blocks, 0))

    hidden = _ffn_up(x, w["ffn1_norm"], w["ffn1_w_gu"], t)
    h, n = _ffn_down(x, hidden, w["ffn1_w_down"], w["mix_norm"], final_norm=False)

    w_in = w["w_in"]
    vec_spec = lambda: pl.BlockSpec((1, d), lambda s, i: (0, 0))
    q = _in_proj(n, w_in, SEG_Q, 1, "rotary", (cos, sin), (pos_spec(), pos_spec()))
    kt = _in_proj(n, w_in, SEG_K, 1, "rotary_transposed", (cos, sin), (pos_spec(), pos_spec()),
                  k_scale=float(2 * half) ** -0.5)
    v = _in_proj(n, w_in, SEG_V, 1, "plain")
    g = _in_proj(n, w_in, SEG_G, 1, "silu_gain", (w["ret_gn_w"],), (vec_spec(),))
    u = _in_proj(n, w_in, SEG_U, 1, "gelu")
    vs = _in_proj(n, w_in, SEG_VS, 1, "gelu_layernorm", (w["sgu_norm_w"],), (vec_spec(),), rows=IN_LN_ROWS)
    gates = _in_proj(n, w_in, SEG_GATE_R, 2, "sigmoid_bias", (w["gate_bias"],),
                     (pl.BlockSpec((2, d), lambda s, i: (0, 0)),))

    sb = _ret_bwd_states(kt, v, w["ret_decay_bwd"], batch)
    ret_proj = _ret_out(q, kt, v, g, sb, w["ret_decay_fwd"], w["ret_decay_bwd"], w["w_ret_out"], batch)
    h = _mix_out(u, vs, w["sgu_w_s"], w["sgu_b_s"], ret_proj, gates, h, w["w_sgu_out"], w["w_out"])

    kv = _mem_kv(mem.reshape(-1, d), w["xattn_norm_mem"], w["xattn_w_kv"])
    h, n = _xattn(h, kv, w["xattn_norm_q"], w["xattn_w_q"], w["xattn_w_o"], w["ffn2_norm"], batch)

    hidden = _ffn_up(n, None, w["ffn2_w_gu"], t)
    y = _ffn_down(h, hidden, w["ffn2_w_down"], w["final_norm"], final_norm=True)
    return y.reshape(batch, seq, d)


def _prepare(ffn1_norm, ffn1_w_gu, ffn1_w_down, mix_norm, w_in, gate_bias, ret_decay_fwd, ret_decay_bwd, ret_gn_w, w_ret_out, sgu_norm_w, sgu_w_s, sgu_b_s, w_sgu_out, w_out, xattn_norm_q, xattn_norm_mem, xattn_w_q, xattn_w_kv, xattn_w_o, ffn2_norm, ffn2_w_gu, ffn2_w_down, final_norm):
    assert ffn1_norm.shape[0] == 1, "single-layer trunk"
    pw = lambda w: _pack_weight(w[0])
    return {
        "ffn1_norm": ffn1_norm, "ffn1_w_gu": _pack_gate_up(ffn1_w_gu[0]), "ffn1_w_down": pw(ffn1_w_down),
        "mix_norm": mix_norm, "w_in": pw(w_in), "gate_bias": gate_bias[0],
        "ret_decay_fwd": ret_decay_fwd[0], "ret_decay_bwd": ret_decay_bwd[0], "ret_gn_w": ret_gn_w,
        "w_ret_out": pw(w_ret_out), "sgu_norm_w": sgu_norm_w,
        "sgu_w_s": sgu_w_s[0].astype(BF16), "sgu_b_s": sgu_b_s[0][:, :, None],
        "w_sgu_out": pw(w_sgu_out), "w_out": pw(w_out),
        "xattn_norm_q": xattn_norm_q, "xattn_norm_mem": xattn_norm_mem,
        "xattn_w_q": pw(xattn_w_q), "xattn_w_kv": pw(xattn_w_kv), "xattn_w_o": pw(xattn_w_o),
        "ffn2_norm": ffn2_norm, "ffn2_w_gu": _pack_gate_up(ffn2_w_gu[0]), "ffn2_w_down": pw(ffn2_w_down),
        "final_norm": final_norm[None, :],
    }


def kernel(x_prompt, x_sample, mem_prompt, mem_sample, ffn1_norm, ffn1_w_gu, ffn1_w_down, mix_norm, w_in, gate_bias, ret_decay_fwd, ret_decay_bwd, ret_gn_w, w_ret_out, sgu_norm_w, sgu_w_s, sgu_b_s, w_sgu_out, w_out, xattn_norm_q, xattn_norm_mem, xattn_w_q, xattn_w_kv, xattn_w_o, ffn2_norm, ffn2_w_gu, ffn2_w_down, final_norm):
    w = _prepare(ffn1_norm, ffn1_w_gu, ffn1_w_down, mix_norm, w_in, gate_bias, ret_decay_fwd, ret_decay_bwd, ret_gn_w, w_ret_out, sgu_norm_w, sgu_w_s, sgu_b_s, w_sgu_out, w_out, xattn_norm_q, xattn_norm_mem, xattn_w_q, xattn_w_kv, xattn_w_o, ffn2_norm, ffn2_w_gu, ffn2_w_down, final_norm)
    return _trunk(x_prompt, mem_prompt, w), _trunk(x_sample, mem_sample, w)
```

```python
import functools
import math

import jax
import jax.numpy as jnp
from jax import lax
from jax.experimental import pallas as pl
from jax.experimental.pallas import tpu as pltpu

F32 = jnp.float32
BF16 = jnp.bfloat16
U32 = jnp.uint32
PACK = 2

EPS = 1e-6
FFN_RES_SCALE = 0.5
ROPE_BASE = 10000.0
RET_HEADS = 8
SGU_GROUPS = 8
SGU_CHUNK = 128
XATTN_HEADS = 4
SEG_Q, SEG_K, SEG_V, SEG_G, SEG_U, SEG_VS, SEG_GATE_R, SEG_GATE_S = range(8)

V7X_VMEM_BYTES = 64 * 1024 * 1024
V7X_SUBLANES = 8
V7X_LANES = 128
V7X_MXU_COLS = 256

RET_BLOCK = 256
RET_BWD_CHUNKS_PER_STEP = 4
RET_PROJECT_LAG = 2
SGU_PROJECT_LAG = 1

PACK_ROWS = 512
PACK_MAX_COLS = 2048
PACK_VMEM_RESERVATION = V7X_VMEM_BYTES - 8 * 1024 * 1024
FFN_UP_ROWS = 1024
FFN_UP_COLS = 1408
GATE_UP_LANES = V7X_LANES
FFN_DOWN_ROWS = 512
IN_ROWS = 2048
IN_LN_ROWS = 1024
LN_ROW_TILE = 256
OUT_ROWS = 256
KV_ROWS = 512
KV_COLS = 1024
XATTN_ROWS = 512
RESIDUAL_COL_TILE = 512
MATMUL_ROW_TILE = 512


def _nbytes(shape, dtype):
    return math.prod(shape) * jnp.dtype(dtype).itemsize


def _params(semantics, vmem_bytes):
    assert vmem_bytes <= V7X_VMEM_BYTES, vmem_bytes
    return pltpu.CompilerParams(dimension_semantics=semantics, vmem_limit_bytes=int(vmem_bytes))


def _resident(shape):
    return pl.BlockSpec(shape, lambda *_: (0,) * len(shape), pipeline_mode=pl.Buffered(1))


def _unpack(words):
    return pltpu.bitcast(words, BF16)


def _pack(vals):
    return pltpu.bitcast(vals.astype(BF16), U32)


def _packed(rows):
    return slice(rows.start // PACK, rows.stop // PACK)


def _tiles(n_rows, n_cols, row_tile, col_tile):
    for r in range(n_rows // row_tile):
        for c in range(n_cols // col_tile):
            yield slice(r * row_tile, (r + 1) * row_tile), slice(c * col_tile, (c + 1) * col_tile)


def _rms(x, w):
    return x * lax.rsqrt(jnp.mean(x * x, axis=-1, keepdims=True) + EPS) * w


def _dot(a, b):
    return jnp.dot(a, b, preferred_element_type=F32)


def _dot_nt(a, b):
    return lax.dot_general(a, b, (((1,), (1,)), ((), ())), preferred_element_type=F32)


def _sigmoid(x):
    return 0.5 * jnp.tanh(0.5 * x) + 0.5


def _gelu_tanh(x):
    c = math.sqrt(2.0 / math.pi)
    return x * (0.5 * jnp.tanh(x * (c + (c * 0.044715) * (x * x))) + 0.5)


def _residual_proj(x_ref, a, w_ref, o_ref):
    for _, cs in _tiles(1, o_ref.shape[1], 1, RESIDUAL_COL_TILE):
        o_ref[:, cs] = x_ref[:, cs] + _dot(a, _unpack(w_ref[:, cs]))


def _residual_proj_norm(x_ref, lhs, w_ref, o_ref, scale, norm_w, emit, w_rows=slice(None), base_ref=None):
    rows, d = o_ref.shape
    for rs, _ in _tiles(rows, 1, rows // 2, 1):
        a = lhs(rs)
        ss = jnp.zeros((a.shape[0], 1), F32)
        for _, cs in _tiles(1, d, 1, RESIDUAL_COL_TILE):
            y = _dot(a, _unpack(w_ref[w_rows, cs]))
            if base_ref is not None:
                y = base_ref[rs, cs] + y
            o = x_ref[rs, cs] + (y if scale is None else scale * y)
            o_ref[rs, cs] = o
            ss += jnp.sum(o * o, axis=-1, keepdims=True)
        emit(rs, o_ref[rs, :] * lax.rsqrt(ss * (1.0 / d) + EPS) * norm_w)


def _pack_kernel(w_ref, o_ref):
    o_ref[...] = _pack(w_ref[...])


def _pack_weight(w):
    k, n = w.shape
    rows = min(PACK_ROWS, k)
    cols = max(c for c in range(V7X_LANES, PACK_MAX_COLS + 1, V7X_LANES) if n % c == 0)
    assert k % rows == 0 and rows % (V7X_SUBLANES * PACK) == 0
    return pl.pallas_call(
        _pack_kernel,
        grid=(k // rows, n // cols),
        in_specs=[pl.BlockSpec((rows, cols), lambda i, j: (i, j))],
        out_specs=pl.BlockSpec((rows // PACK, cols), lambda i, j: (i, j)),
        out_shape=jax.ShapeDtypeStruct((k // PACK, n), U32),
        compiler_params=_params(("parallel", "parallel"), PACK_VMEM_RESERVATION),
        name="pack_weight",
    )(w)


def _pack_gate_up_kernel(g_ref, u_ref, o_ref):
    for _, cs in _tiles(1, g_ref.shape[1], 1, GATE_UP_LANES):
        o_ref[:, 2 * cs.start:2 * cs.start + GATE_UP_LANES] = _pack(g_ref[:, cs])
        o_ref[:, 2 * cs.start + GATE_UP_LANES:2 * cs.stop] = _pack(u_ref[:, cs])


def _pack_gate_up(w_gu):
    k, n = w_gu.shape
    rows = min(PACK_ROWS, k)
    n_col_steps = n // 2 // FFN_UP_COLS
    assert n % (2 * FFN_UP_COLS) == 0 and k % rows == 0
    return pl.pallas_call(
        _pack_gate_up_kernel,
        grid=(k // rows, n_col_steps),
        in_specs=[pl.BlockSpec((rows, FFN_UP_COLS), lambda i, j: (i, j)),
                  pl.BlockSpec((rows, FFN_UP_COLS), lambda i, j: (i, j + n_col_steps))],
        out_specs=pl.BlockSpec((rows // PACK, 2 * FFN_UP_COLS), lambda i, j: (i, j)),
        out_shape=jax.ShapeDtypeStruct((k // PACK, n), U32),
        compiler_params=_params(("parallel", "parallel"), PACK_VMEM_RESERVATION),
        name="pack_gate_up",
    )(w_gu, w_gu)


def _ffn_up_kernel(*refs, norm_input):
    if norm_input:
        x_ref, nw_ref, w_ref, o_ref, n_scr = refs

        @pl.when(pl.program_id(1) == 0)
        def _():
            n_scr[...] = _rms(x_ref[...], nw_ref[...]).astype(BF16)

        lhs = lambda rs: n_scr[rs, :]
    else:
        n_ref, w_ref, o_ref = refs
        lhs = lambda rs: _unpack(n_ref[_packed(rs), :])
    for rs, cs in _tiles(o_ref.shape[0] * PACK, w_ref.shape[1], MATMUL_ROW_TILE, 2 * GATE_UP_LANES):
        z = _dot(lhs(rs), _unpack(w_ref[:, cs]))
        g, u = z[:, :GATE_UP_LANES], z[:, GATE_UP_LANES:]
        o_ref[_packed(rs), cs.start // 2:cs.stop // 2] = _pack(g * _sigmoid(g) * u)


def _ffn_up(x_or_n, norm_w, w_gu, t):
    d = x_or_n.shape[1]
    d_ff = w_gu.shape[1] // 2
    assert t % FFN_UP_ROWS == 0 and d_ff % FFN_UP_COLS == 0 and FFN_UP_COLS % GATE_UP_LANES == 0
    norm_input = norm_w is not None
    w_spec = pl.BlockSpec((d // PACK, 2 * FFN_UP_COLS), lambda i, j: (0, j))
    if norm_input:
        in_specs = [pl.BlockSpec((FFN_UP_ROWS, d), lambda i, j: (i, 0)), pl.BlockSpec((1, d), lambda i, j: (0, 0)), w_spec]
        args = (x_or_n, norm_w, w_gu)
        scratch = [pltpu.VMEM((FFN_UP_ROWS, d), BF16)]
    else:
        in_specs = [pl.BlockSpec((FFN_UP_ROWS // PACK, d), lambda i, j: (i, 0)), w_spec]
        args = (x_or_n, w_gu)
        scratch = []
    vmem = (2 * _nbytes((FFN_UP_ROWS, d), F32 if norm_input else BF16) + _nbytes((FFN_UP_ROWS, d), BF16)
            + 4 * _nbytes((d, FFN_UP_COLS), BF16) + 2 * _nbytes((FFN_UP_ROWS, FFN_UP_COLS), BF16)
            + 8 * _nbytes((MATMUL_ROW_TILE, 2 * GATE_UP_LANES), F32))
    return pl.pallas_call(
        functools.partial(_ffn_up_kernel, norm_input=norm_input),
        grid=(t // FFN_UP_ROWS, d_ff // FFN_UP_COLS),
        in_specs=in_specs,
        out_specs=pl.BlockSpec((FFN_UP_ROWS // PACK, FFN_UP_COLS), lambda i, j: (i, j)),
        out_shape=jax.ShapeDtypeStruct((t // PACK, d_ff), U32),
        scratch_shapes=scratch,
        compiler_params=_params(("parallel", "arbitrary"), vmem),
        name="ffn_up_norm" if norm_input else "ffn_up",
    )(*args)


def _ffn_down_kernel(x_ref, hid_ref, wd_ref, nw_ref, o_ref, *n_out, final_norm):
    def emit(rs, normed):
        if final_norm:
            o_ref[rs, :] = normed
        else:
            n_out[0][_packed(rs), :] = _pack(normed)

    _residual_proj_norm(x_ref, lambda rs: _unpack(hid_ref[_packed(rs), :]), wd_ref, o_ref, FFN_RES_SCALE,
                        nw_ref[...], emit)


def _ffn_down(x, hidden, w_down, norm_w, final_norm):
    t, d = x.shape
    d_ff = hidden.shape[1]
    assert t % FFN_DOWN_ROWS == 0
    row_spec = lambda: pl.BlockSpec((FFN_DOWN_ROWS, d), lambda i: (i, 0))
    packed_spec = lambda cols: pl.BlockSpec((FFN_DOWN_ROWS // PACK, cols), lambda i: (i, 0))
    vmem = (4 * _nbytes((FFN_DOWN_ROWS, d), F32) + 2 * _nbytes((FFN_DOWN_ROWS, d_ff), BF16)
            + _nbytes((d_ff, d), BF16) + 2 * _nbytes((FFN_DOWN_ROWS, d), BF16)
            + 4 * _nbytes((FFN_DOWN_ROWS, RESIDUAL_COL_TILE), F32))
    out_f32 = jax.ShapeDtypeStruct((t, d), F32)
    return pl.pallas_call(
        functools.partial(_ffn_down_kernel, final_norm=final_norm),
        grid=(t // FFN_DOWN_ROWS,),
        in_specs=[row_spec(), packed_spec(d_ff), _resident((d_ff // PACK, d)), pl.BlockSpec((1, d), lambda i: (0, 0))],
        out_specs=row_spec() if final_norm else (row_spec(), packed_spec(d)),
        out_shape=out_f32 if final_norm else (out_f32, jax.ShapeDtypeStruct((t // PACK, d), U32)),
        compiler_params=_params(("parallel",), vmem),
        name="ffn_down_final" if final_norm else "ffn_down",
    )(x, hidden, w_down, norm_w)


def _in_proj_kernel(n_ref, w_ref, *refs, kind, k_scale):
    seg = pl.program_id(0)
    rows, d = n_ref.shape[0] * PACK, n_ref.shape[1]
    o_ref = refs[1] if kind == "gelu_layernorm" else refs[-1]

    def z_tiles():
        for rs, cs in _tiles(rows, d, MATMUL_ROW_TILE, V7X_MXU_COLS):
            yield rs, cs, _dot(_unpack(n_ref[_packed(rs), :]), _unpack(w_ref[:, cs]))

    def store(rs, cs, val):
        o_ref[0, _packed(rs), cs] = _pack(val)

    if kind in ("rotary", "rotary_transposed"):
        cos_ref, sin_ref, _ = refs
        half = V7X_MXU_COLS // 2
        for rs, cs, z in z_tiles():
            x1, x2 = z[:, :half], z[:, half:]
            cos, sin = cos_ref[rs, :], sin_ref[rs, :]
            if kind == "rotary":
                store(rs, slice(cs.start, cs.start + half), x1 * cos - x2 * sin)
                store(rs, slice(cs.start + half, cs.stop), x1 * sin + x2 * cos)
            else:
                rot = jnp.concatenate([x1 * cos - x2 * sin, x1 * sin + x2 * cos], axis=1) * k_scale
                rot_t = _pack(rot.T)
                for j in range(MATMUL_ROW_TILE // RET_BLOCK):
                    o_ref[rs.start // RET_BLOCK + j, cs.start // V7X_MXU_COLS] = rot_t[:, j * RET_BLOCK:(j + 1) * RET_BLOCK]
    elif kind == "plain":
        for rs, cs, z in z_tiles():
            store(rs, cs, z)
    elif kind == "silu_gain":
        gain_ref, _ = refs
        for rs, cs, z in z_tiles():
            store(rs, cs, z * _sigmoid(z) * gain_ref[:, cs])
    elif kind == "gelu":
        for rs, cs, z in z_tiles():
            store(rs, cs, _gelu_tanh(z))
    elif kind == "gelu_layernorm":
        lnw_ref, _, a_scr = refs
        for rs, _ in _tiles(rows, 1, LN_ROW_TILE, 1):
            for _, cs in _tiles(1, d, 1, V7X_MXU_COLS):
                a_scr[rs, cs] = _gelu_tanh(_dot(_unpack(n_ref[_packed(rs), :]), _unpack(w_ref[:, cs])))
            a = a_scr[rs, :]
            cen = a - jnp.mean(a, axis=-1, keepdims=True)
            var = jnp.mean(cen * cen, axis=-1, keepdims=True)
            o_ref[0, _packed(rs), :] = _pack(cen * lax.rsqrt(var + EPS) * lnw_ref[...])
    elif kind == "sigmoid_bias":
        gb_ref, _ = refs
        for rs, cs, z in z_tiles():
            store(rs, cs, _sigmoid(z + gb_ref[pl.ds(seg, 1), cs]))
    else:
        raise ValueError(kind)


def _in_proj(n, w_in, first_seg, n_seg, kind, extra=(), extra_specs=(), rows=IN_ROWS, k_scale=None):
    t, d = n.shape[0] * PACK, n.shape[1]
    assert t % rows == 0 and w_in.shape[1] % d == 0
    scratch = [pltpu.VMEM((rows, d), F32)] if kind == "gelu_layernorm" else []
    vmem = (4 * _nbytes((rows, d), BF16) + 2 * _nbytes((d, d), BF16) + 4 * _nbytes((rows, d // RET_HEADS // 2), F32)
            + (2 * _nbytes((rows, d), F32) if scratch else 0) + 16 * _nbytes((MATMUL_ROW_TILE, V7X_MXU_COLS), F32))
    if kind in ("rotary", "rotary_transposed"):
        assert d // RET_HEADS == V7X_MXU_COLS and MATMUL_ROW_TILE % RET_BLOCK == 0
    if kind == "rotary_transposed":
        assert n_seg == 1
        head = d // RET_HEADS
        out_spec = pl.BlockSpec((rows // RET_BLOCK, RET_HEADS, head // PACK, RET_BLOCK), lambda s, i: (i, 0, 0, 0))
        out_shape = jax.ShapeDtypeStruct((t // RET_BLOCK, RET_HEADS, head // PACK, RET_BLOCK), U32)
    else:
        out_spec = pl.BlockSpec((1, rows // PACK, d), lambda s, i: (s, i, 0))
        out_shape = jax.ShapeDtypeStruct((n_seg, t // PACK, d), U32)
    return pl.pallas_call(
        functools.partial(_in_proj_kernel, kind=kind, k_scale=k_scale),
        grid=(n_seg, t // rows),
        in_specs=[pl.BlockSpec((rows // PACK, d), lambda s, i: (i, 0)),
                  pl.BlockSpec((d // PACK, d), lambda s, i: (0, first_seg + s))] + list(extra_specs),
        out_specs=out_spec,
        out_shape=out_shape,
        scratch_shapes=scratch,
        compiler_params=_params(("parallel", "parallel"), vmem),
        name="in_proj_" + kind,
    )(n, w_in, *extra)


def _decay_rate(dec_ref, h, shape):
    return -jnp.exp(jnp.full(shape, dec_ref[h], F32))


def _ret_bwd_state_kernel(decb_ref, kt_ref, v_ref, sb_ref, state, kdec, cdec):
    c = pl.program_id(1)
    blk = kt_ref.shape[3]
    head = v_ref.shape[2] // RET_HEADS

    @pl.when((pl.program_id(0) == 0) & (c == 0))
    def _():
        col = lax.broadcasted_iota(jnp.int32, (1, blk), 1).astype(F32)
        for h in range(RET_HEADS):
            kdec[h] = jnp.exp(_decay_rate(decb_ref, h, (1, blk)) * col)
            cdec[h] = jnp.exp(_decay_rate(decb_ref, h, (1, head)) * float(blk))

    @pl.when(c == 0)
    def _():
        state[...] = jnp.zeros_like(state)

    for j in reversed(range(kt_ref.shape[0])):
        rows = slice(j * blk // PACK, (j + 1) * blk // PACK)
        for h in range(RET_HEADS):
            sl = slice(h * head, (h + 1) * head)
            st = state[h]
            sb_ref[0, j, h] = _pack(st)
            kd = (_unpack(kt_ref[j, h]).astype(F32) * kdec[h]).astype(BF16)
            state[h] = st * cdec[h] + _dot(kd, _unpack(v_ref[0, rows, sl]))


def _ret_bwd_states(kt, v, dec_b, batch):
    _, t_packed, d = v.shape
    n_blk = t_packed * PACK // batch // RET_BLOCK
    head = d // RET_HEADS
    per_step = RET_BWD_CHUNKS_PER_STEP
    n_steps = n_blk // per_step
    assert n_blk % per_step == 0
    blk_bytes = per_step * _nbytes((RET_BLOCK, d), BF16)
    state_bytes = _nbytes((RET_HEADS, head, head), F32)
    return pl.pallas_call(
        _ret_bwd_state_kernel,
        grid=(batch, n_steps),
        in_specs=[
            pl.BlockSpec(memory_space=pltpu.SMEM),
            pl.BlockSpec((per_step, RET_HEADS, head // PACK, RET_BLOCK),
                         lambda b, c: (b * n_steps + (n_steps - 1 - c), 0, 0, 0)),
            pl.BlockSpec((1, per_step * RET_BLOCK // PACK, d), lambda b, c: (0, b * n_steps + (n_steps - 1 - c), 0)),
        ],
        out_specs=pl.BlockSpec((1, per_step, RET_HEADS, head // PACK, head), lambda b, c: (b, n_steps - 1 - c, 0, 0, 0)),
        out_shape=jax.ShapeDtypeStruct((batch, n_blk, RET_HEADS, head // PACK, head), U32),
        scratch_shapes=[
            pltpu.VMEM((RET_HEADS, head, head), F32),
            pltpu.VMEM((RET_HEADS, 1, RET_BLOCK), F32),
            pltpu.VMEM((RET_HEADS, 1, head), F32),
        ],
        compiler_params=_params(("arbitrary", "arbitrary"), 6 * blk_bytes + 4 * state_bytes),
        name="ret_bwd_states",
    )(dec_b, kt, v)


def _project_slice(act_ref, w_ref, acc_ref, i):
    width = act_ref.shape[2]
    w_rows = slice(i * width // PACK, (i + 1) * width // PACK)
    for _, cs in _tiles(1, acc_ref.shape[1], 1, RESIDUAL_COL_TILE):
        part = _dot(act_ref[i], _unpack(w_ref[w_rows, cs]))
        acc_ref[:, cs] = part if i == 0 else acc_ref[:, cs] + part


def _ret_out_kernel(decf_ref, decb_ref, q_ref, kt_ref, v_ref, g_ref, sb_ref, wr_ref, o_ref,
                    state, dmat, qdf, qdb, kdf, cdf, act, acc):
    c = pl.program_id(1)
    blk = q_ref.shape[1] * PACK
    head = q_ref.shape[2] // RET_HEADS

    @pl.when((pl.program_id(0) == 0) & (c == 0))
    def _():
        row = lax.broadcasted_iota(jnp.int32, (blk, head), 0).astype(F32)
        col = lax.broadcasted_iota(jnp.int32, (1, blk), 1).astype(F32)
        ri = lax.broadcasted_iota(jnp.int32, (blk, blk), 0)
        ci = lax.broadcasted_iota(jnp.int32, (blk, blk), 1)
        dist = (ri - ci).astype(F32)
        for h in range(RET_HEADS):
            qdf[h] = jnp.exp(_decay_rate(decf_ref, h, (blk, head)) * (row + 1.0))
            qdb[h] = jnp.exp(_decay_rate(decb_ref, h, (blk, head)) * (float(blk) - row))
            kdf[h] = jnp.exp(_decay_rate(decf_ref, h, (1, blk)) * (float(blk) - 1.0 - col))
            cdf[h] = jnp.exp(_decay_rate(decf_ref, h, (1, head)) * float(blk))
            lf2 = _decay_rate(decf_ref, h, (blk, blk))
            lb2 = _decay_rate(decb_ref, h, (blk, blk))
            dmat[h] = jnp.where(ri >= ci, jnp.exp(lf2 * jnp.maximum(dist, 0.0)), jnp.exp(lb2 * jnp.maximum(-dist, 0.0)))

    @pl.when(c == 0)
    def _():
        state[...] = jnp.zeros_like(state)

    for h in range(RET_HEADS):
        sl = slice(h * head, (h + 1) * head)
        q = _unpack(q_ref[0, :, sl])
        kt = _unpack(kt_ref[0, h])
        v = _unpack(v_ref[0, :, sl])
        st = state[h]
        scores = _dot(q, kt)
        cross = _dot(q, st.astype(BF16)) * qdf[h] + _dot(q, _unpack(sb_ref[0, 0, h])) * qdb[h]
        state[h] = st * cdf[h] + _dot((kt.astype(F32) * kdf[h]).astype(BF16), v)
        o = _dot((scores * dmat[h]).astype(BF16), v) + cross
        cen = o - jnp.mean(o, axis=-1, keepdims=True)
        var = jnp.mean(cen * cen, axis=-1, keepdims=True)
        act[h] = (_unpack(g_ref[0, :, sl]).astype(F32) * (cen * lax.rsqrt(var + EPS))).astype(BF16)
        if h >= RET_PROJECT_LAG:
            _project_slice(act, wr_ref, acc, h - RET_PROJECT_LAG)
    for h in range(RET_HEADS - RET_PROJECT_LAG, RET_HEADS):
        _project_slice(act, wr_ref, acc, h)
    o_ref[...] = _pack(acc[...])


def _ret_out(q, kt, v, g, sb, dec_f, dec_b, w_ret_out, batch):
    _, t_packed, d = q.shape
    n_blk = t_packed * PACK // batch // RET_BLOCK
    head = d // RET_HEADS
    chunk = lambda: pl.BlockSpec((1, RET_BLOCK // PACK, d), lambda b, c: (0, b * n_blk + c, 0))
    blk_bytes = _nbytes((RET_BLOCK, d), BF16)
    table_bytes = _nbytes((RET_HEADS, RET_BLOCK, head), F32)
    return pl.pallas_call(
        _ret_out_kernel,
        grid=(batch, n_blk),
        in_specs=[
            pl.BlockSpec(memory_space=pltpu.SMEM),
            pl.BlockSpec(memory_space=pltpu.SMEM),
            chunk(),
            pl.BlockSpec((1, RET_HEADS, head // PACK, RET_BLOCK), lambda b, c: (b * n_blk + c, 0, 0, 0)),
            chunk(), chunk(),
            pl.BlockSpec((1, 1, RET_HEADS, head // PACK, head), lambda b, c: (b, c, 0, 0, 0)),
            _resident((d // PACK, d)),
        ],
        out_specs=pl.BlockSpec((RET_BLOCK // PACK, d), lambda b, c: (b * n_blk + c, 0)),
        out_shape=jax.ShapeDtypeStruct((t_packed, d), U32),
        scratch_shapes=[
            pltpu.VMEM((RET_HEADS, head, head), F32),
            pltpu.VMEM((RET_HEADS, RET_BLOCK, RET_BLOCK), F32),
            pltpu.VMEM((RET_HEADS, RET_BLOCK, head), F32),
            pltpu.VMEM((RET_HEADS, RET_BLOCK, head), F32),
            pltpu.VMEM((RET_HEADS, 1, RET_BLOCK), F32),
            pltpu.VMEM((RET_HEADS, 1, head), F32),
            pltpu.VMEM((RET_HEADS, RET_BLOCK, head), BF16),
            pltpu.VMEM((RET_BLOCK, d), F32),
        ],
        compiler_params=_params(("arbitrary", "arbitrary"),
                                12 * blk_bytes + 6 * table_bytes + _nbytes((d, d), BF16) + 4 * _nbytes((RET_BLOCK, d), F32)),
        name="ret_out",
    )(dec_f, dec_b, q, kt, v, g, sb, w_ret_out)


def _mix_out_kernel(u_ref, vs_ref, ws_ref, bs_ref, ret_ref, gr_ref, gs_ref, h_ref, wsg_ref, wo_ref, o_ref, act, acc):
    rows, d = h_ref.shape
    gdim = d // SGU_GROUPS
    for g in range(SGU_GROUPS):
        cs = slice(g * gdim, (g + 1) * gdim)
        for rs, _ in _tiles(rows, 1, SGU_CHUNK, 1):
            mixed = _dot(ws_ref[g], _unpack(vs_ref[0, _packed(rs), cs])) + bs_ref[g]
            act[g, rs, :] = (_unpack(u_ref[0, _packed(rs), cs]).astype(F32) * mixed).astype(BF16)
        if g >= SGU_PROJECT_LAG:
            _project_slice(act, wsg_ref, acc, g - SGU_PROJECT_LAG)
    for g in range(SGU_GROUPS - SGU_PROJECT_LAG, SGU_GROUPS):
        _project_slice(act, wsg_ref, acc, g)
    for k in range(SGU_GROUPS):
        ks = slice(k * gdim, (k + 1) * gdim)
        merged = (_unpack(gr_ref[0, :, ks]).astype(F32) * _unpack(ret_ref[:, ks]).astype(F32)
                  + _unpack(gs_ref[0, :, ks]).astype(F32) * acc[:, ks]).astype(BF16)
        w_rows = slice(k * gdim // PACK, (k + 1) * gdim // PACK)
        for _, cs in _tiles(1, d, 1, RESIDUAL_COL_TILE):
            o_ref[:, cs] = (h_ref[:, cs] if k == 0 else o_ref[:, cs]) + _dot(merged, _unpack(wo_ref[w_rows, cs]))


def _mix_out(u, vs, w_s, b_s, ret_proj, gates, h, w_sgu_out, w_out):
    t, d = h.shape
    gdim = d // SGU_GROUPS
    assert t % OUT_ROWS == 0 and OUT_ROWS % SGU_CHUNK == 0
    rows = lambda: pl.BlockSpec((OUT_ROWS, d), lambda i: (i, 0))
    packed_rows = lambda: pl.BlockSpec((OUT_ROWS // PACK, d), lambda i: (i, 0))
    seg_rows = lambda s: pl.BlockSpec((1, OUT_ROWS // PACK, d), lambda i: (s, i, 0))
    weight = lambda: _resident((d // PACK, d))
    vmem = (2 * _nbytes((d, d), BF16) + 10 * _nbytes((OUT_ROWS, d), BF16) + 4 * _nbytes((OUT_ROWS, d), F32)
            + _nbytes((OUT_ROWS, d), BF16) + 4 * _nbytes((OUT_ROWS, d), F32))
    return pl.pallas_call(
        _mix_out_kernel,
        grid=(t // OUT_ROWS,),
        in_specs=[
            seg_rows(0), seg_rows(0),
            pl.BlockSpec((SGU_GROUPS, SGU_CHUNK, SGU_CHUNK), lambda i: (0, 0, 0)),
            pl.BlockSpec((SGU_GROUPS, SGU_CHUNK, 1), lambda i: (0, 0, 0)),
            packed_rows(), seg_rows(0), seg_rows(1), rows(), weight(), weight(),
        ],
        out_specs=rows(),
        out_shape=jax.ShapeDtypeStruct((t, d), F32),
        scratch_shapes=[pltpu.VMEM((SGU_GROUPS, OUT_ROWS, gdim), BF16), pltpu.VMEM((OUT_ROWS, d), F32)],
        compiler_params=_params(("parallel",), vmem),
        name="mix_out",
    )(u, vs, w_s, b_s, ret_proj, gates, gates, h, w_sgu_out, w_out)


def _mem_kv_kernel(m_ref, nw_ref, w_ref, o_ref, n_scr):
    @pl.when(pl.program_id(1) == 0)
    def _():
        n_scr[...] = _rms(m_ref[...], nw_ref[...]).astype(BF16)

    o_ref[...] = _pack(_dot(n_scr[...], _unpack(w_ref[...])))


def _mem_kv(mem, norm_w, w_kv):
    t, d = mem.shape
    rows = min(KV_ROWS, t)
    n_out = w_kv.shape[1]
    assert t % rows == 0 and n_out % KV_COLS == 0
    vmem = (2 * _nbytes((rows, d), F32) + _nbytes((rows, d), BF16) + 2 * _nbytes((d, KV_COLS), BF16)
            + 2 * _nbytes((rows, KV_COLS), BF16) + 2 * _nbytes((rows, d), F32))
    return pl.pallas_call(
        _mem_kv_kernel,
        grid=(t // rows, n_out // KV_COLS),
        in_specs=[
            pl.BlockSpec((rows, d), lambda i, j: (i, 0)),
            pl.BlockSpec((1, d), lambda i, j: (0, 0)),
            pl.BlockSpec((d // PACK, KV_COLS), lambda i, j: (0, j)),
        ],
        out_specs=pl.BlockSpec((rows // PACK, KV_COLS), lambda i, j: (i, j)),
        out_shape=jax.ShapeDtypeStruct((t // PACK, n_out), U32),
        scratch_shapes=[pltpu.VMEM((rows, d), BF16)],
        compiler_params=_params(("parallel", "arbitrary"), vmem),
        name="mem_kv",
    )(mem, norm_w, w_kv)


def _xattn_kernel(h_ref, nw_ref, kv_ref, wq_ref, wo_ref, next_nw_ref, o_ref, n_out_ref, n_scr, q_scr, a_scr, acc):
    rows, d = h_ref.shape
    hd = d // XATTN_HEADS
    scale = float(hd) ** -0.5
    head_cols = lambda h: slice(h * hd, (h + 1) * hd)

    def project_q(h, rs=slice(None)):
        q_scr[rs, head_cols(h)] = _dot(n_scr[rs, :], _unpack(wq_ref[:, head_cols(h)])).astype(BF16)

    def probs(h):
        s = _dot_nt(q_scr[:, head_cols(h)], _unpack(kv_ref[:, head_cols(h)])) * scale
        e = jnp.exp(s - jnp.max(s, axis=-1, keepdims=True))
        return (e / jnp.sum(e, axis=-1, keepdims=True)).astype(BF16)

    def emit(rs, normed):
        n_out_ref[_packed(rs), :] = _pack(normed)

    for rs, _ in _tiles(rows, 1, rows // 2, 1):
        n_scr[rs, :] = _rms(h_ref[rs, :], nw_ref[...]).astype(BF16)
        project_q(0, rs)
    last = XATTN_HEADS - 1
    p = probs(0)
    for h in range(XATTN_HEADS):
        if h < last:
            project_q(h + 1)
        p_next = probs(h + 1) if h < last else None
        a_scr[h] = _dot(p, _unpack(kv_ref[:, d + h * hd:d + (h + 1) * hd])).astype(BF16)
        if h < last:
            _project_slice(a_scr, wo_ref, acc, h)
        else:
            _residual_proj_norm(h_ref, lambda rs: a_scr[last, rs, :], wo_ref, o_ref, None, next_nw_ref[...], emit,
                                w_rows=slice(last * hd // PACK, (last + 1) * hd // PACK), base_ref=acc)
        p = p_next


def _xattn(h, kv, norm_w, w_q, w_o, next_norm_w, batch):
    t, d = h.shape
    n_mem = kv.shape[0] * PACK // batch
    row_blocks = t // batch // XATTN_ROWS
    assert t % (batch * XATTN_ROWS) == 0
    rows = lambda: pl.BlockSpec((XATTN_ROWS, d), lambda b, i: (b * row_blocks + i, 0))
    vec = lambda: pl.BlockSpec((1, d), lambda b, i: (0, 0))
    weight = lambda: _resident((d // PACK, d))
    vmem = (2 * _nbytes((d, d), BF16) + 5 * _nbytes((XATTN_ROWS, d), F32) + 5 * _nbytes((XATTN_ROWS, d), BF16)
            + 2 * _nbytes((n_mem, 2 * d), BF16) + 2 * _nbytes((XATTN_ROWS, d), F32))
    return pl.pallas_call(
        _xattn_kernel,
        grid=(batch, row_blocks),
        in_specs=[rows(), vec(), pl.BlockSpec((n_mem // PACK, 2 * d), lambda b, i: (b, 0)), weight(), weight(), vec()],
        out_specs=(rows(), pl.BlockSpec((XATTN_ROWS // PACK, d), lambda b, i: (b * row_blocks + i, 0))),
        out_shape=(jax.ShapeDtypeStruct((t, d), F32), jax.ShapeDtypeStruct((t // PACK, d), U32)),
        scratch_shapes=[pltpu.VMEM((XATTN_ROWS, d), BF16), pltpu.VMEM((XATTN_ROWS, d), BF16),
                        pltpu.VMEM((XATTN_HEADS, XATTN_ROWS, d // XATTN_HEADS), BF16),
                        pltpu.VMEM((XATTN_ROWS, d), F32)],
        compiler_params=_params(("parallel", "parallel"), vmem),
        name="xattn",
    )(h, norm_w, kv, w_q, w_o, next_norm_w)


def _rope_tables(seq, half):
    freqs = ROPE_BASE ** (-jnp.linspace(0.0, 1.0, half, dtype=F32))
    ang = jnp.arange(seq, dtype=F32)[:, None] * freqs[None, :]
    return jnp.cos(ang), jnp.sin(ang)


def _trunk(x, mem, w):
    batch, seq, d = x.shape
    t = batch * seq
    x = x.reshape(t, d)
    half = d // RET_HEADS // 2
    cos, sin = _rope_tables(seq, half)
    pos_blocks = seq // IN_ROWS
    assert seq % IN_ROWS == 0
    pos_spec = lambda: pl.BlockSpec((IN_ROWS, half), lambda s, i: (i % pos_blocks, 0))

    hidden = _ffn_up(x, w["ffn1_norm"], w["ffn1_w_gu"], t)
    h, n = _ffn_down(x, hidden, w["ffn1_w_down"], w["mix_norm"], final_norm=False)

    w_in = w["w_in"]
    vec_spec = lambda: pl.BlockSpec((1, d), lambda s, i: (0, 0))
    q = _in_proj(n, w_in, SEG_Q, 1, "rotary", (cos, sin), (pos_spec(), pos_spec()))
    kt = _in_proj(n, w_in, SEG_K, 1, "rotary_transposed", (cos, sin), (pos_spec(), pos_spec()),
                  k_scale=float(2 * half) ** -0.5)
    v = _in_proj(n, w_in, SEG_V, 1, "plain")
    g = _in_proj(n, w_in, SEG_G, 1, "silu_gain", (w["ret_gn_w"],), (vec_spec(),))
    u = _in_proj(n, w_in, SEG_U, 1, "gelu")
    vs = _in_proj(n, w_in, SEG_VS, 1, "gelu_layernorm", (w["sgu_norm_w"],), (vec_spec(),), rows=IN_LN_ROWS)
    gates = _in_proj(n, w_in, SEG_GATE_R, 2, "sigmoid_bias", (w["gate_bias"],),
                     (pl.BlockSpec((2, d), lambda s, i: (0, 0)),))

    sb = _ret_bwd_states(kt, v, w["ret_decay_bwd"], batch)
    ret_proj = _ret_out(q, kt, v, g, sb, w["ret_decay_fwd"], w["ret_decay_bwd"], w["w_ret_out"], batch)
    h = _mix_out(u, vs, w["sgu_w_s"], w["sgu_b_s"], ret_proj, gates, h, w["w_sgu_out"], w["w_out"])

    kv = _mem_kv(mem.reshape(-1, d), w["xattn_norm_mem"], w["xattn_w_kv"])
    h, n = _xattn(h, kv, w["xattn_norm_q"], w["xattn_w_q"], w["xattn_w_o"], w["ffn2_norm"], batch)

    hidden = _ffn_up(n, None, w["ffn2_w_gu"], t)
    y = _ffn_down(h, hidden, w["ffn2_w_down"], w["final_norm"], final_norm=True)
    return y.reshape(batch, seq, d)


def _prepare(ffn1_norm, ffn1_w_gu, ffn1_w_down, mix_norm, w_in, gate_bias, ret_decay_fwd, ret_decay_bwd, ret_gn_w, w_ret_out, sgu_norm_w, sgu_w_s, sgu_b_s, w_sgu_out, w_out, xattn_norm_q, xattn_norm_mem, xattn_w_q, xattn_w_kv, xattn_w_o, ffn2_norm, ffn2_w_gu, ffn2_w_down, final_norm):
    assert ffn1_norm.shape[0] == 1, "single-layer trunk"
    pw = lambda w: _pack_weight(w[0])
    return {
        "ffn1_norm": ffn1_norm, "ffn1_w_gu": _pack_gate_up(ffn1_w_gu[0]), "ffn1_w_down": pw(ffn1_w_down),
        "mix_norm": mix_norm, "w_in": pw(w_in), "gate_bias": gate_bias[0],
        "ret_decay_fwd": ret_decay_fwd[0], "ret_decay_bwd": ret_decay_bwd[0], "ret_gn_w": ret_gn_w,
        "w_ret_out": pw(w_ret_out), "sgu_norm_w": sgu_norm_w,
        "sgu_w_s": sgu_w_s[0].astype(BF16), "sgu_b_s": sgu_b_s[0][:, :, None],
        "w_sgu_out": pw(w_sgu_out), "w_out": pw(w_out),
        "xattn_norm_q": xattn_norm_q, "xattn_norm_mem": xattn_norm_mem,
        "xattn_w_q": pw(xattn_w_q), "xattn_w_kv": pw(xattn_w_kv), "xattn_w_o": pw(xattn_w_o),
        "ffn2_norm": ffn2_norm, "ffn2_w_gu": _pack_gate_up(ffn2_w_gu[0]), "ffn2_w_down": pw(ffn2_w_down),
        "final_norm": final_norm[None, :],
    }


def kernel(x_prompt, x_sample, mem_prompt, mem_sample, ffn1_norm, ffn1_w_gu, ffn1_w_down, mix_norm, w_in, gate_bias, ret_decay_fwd, ret_decay_bwd, ret_gn_w, w_ret_out, sgu_norm_w, sgu_w_s, sgu_b_s, w_sgu_out, w_out, xattn_norm_q, xattn_norm_mem, xattn_w_q, xattn_w_kv, xattn_w_o, ffn2_norm, ffn2_w_gu, ffn2_w_down, final_norm):
    w = _prepare(ffn1_norm, ffn1_w_gu, ffn1_w_down, mix_norm, w_in, gate_bias, ret_decay_fwd, ret_decay_bwd, ret_gn_w, w_ret_out, sgu_norm_w, sgu_w_s, sgu_b_s, w_sgu_out, w_out, xattn_norm_q, xattn_norm_mem, xattn_w_q, xattn_w_kv, xattn_w_o, ffn2_norm, ffn2_w_gu, ffn2_w_down, final_norm)
    return _trunk(x_prompt, mem_prompt, w), _trunk(x_sample, mem_sample, w)
```

```python
import functools
import math

import jax
import jax.numpy as jnp
from jax import lax
from jax.experimental import pallas as pl
from jax.experimental.pallas import tpu as pltpu

F32 = jnp.float32
BF16 = jnp.bfloat16
U32 = jnp.uint32
PACK = 2

EPS = 1e-6
FFN_RES_SCALE = 0.5
ROPE_BASE = 10000.0
RET_HEADS = 8
SGU_GROUPS = 8
SGU_CHUNK = 128
XATTN_HEADS = 4
SEG_Q, SEG_K, SEG_V, SEG_G, SEG_U, SEG_VS, SEG_GATE_R, SEG_GATE_S = range(8)

V7X_VMEM_BYTES = 64 * 1024 * 1024
V7X_SUBLANES = 8
V7X_LANES = 128
V7X_MXU_COLS = 256

RET_BLOCK = 256
RET_BWD_CHUNKS_PER_STEP = 4
RET_OUT_CHUNKS_PER_STEP = 2
RET_PROJECT_LAG = 2
SGU_PROJECT_LAG = 1

PACK_ROWS = 512
PACK_MAX_COLS = 2048
PACK_VMEM_RESERVATION = V7X_VMEM_BYTES - 8 * 1024 * 1024
FFN_UP_ROWS = 1024
FFN_UP_COLS = 1408
GATE_UP_LANES = V7X_LANES
FFN_DOWN_ROWS = 512
IN_ROWS = 2048
IN_LN_ROWS = 1024
LN_ROW_TILE = 256
OUT_ROWS = 256
KV_ROWS = 512
KV_COLS = 1024
XATTN_ROWS = 512
RESIDUAL_COL_TILE = 512
MATMUL_ROW_TILE = 512


def _nbytes(shape, dtype):
    return math.prod(shape) * jnp.dtype(dtype).itemsize


def _params(semantics, vmem_bytes):
    assert vmem_bytes <= V7X_VMEM_BYTES, vmem_bytes
    return pltpu.CompilerParams(dimension_semantics=semantics, vmem_limit_bytes=int(vmem_bytes))


def _resident(shape):
    return pl.BlockSpec(shape, lambda *_: (0,) * len(shape), pipeline_mode=pl.Buffered(1))


def _unpack(words):
    return pltpu.bitcast(words, BF16)


def _pack(vals):
    return pltpu.bitcast(vals.astype(BF16), U32)


def _packed(rows):
    return slice(rows.start // PACK, rows.stop // PACK)


def _tiles(n_rows, n_cols, row_tile, col_tile):
    for r in range(n_rows // row_tile):
        for c in range(n_cols // col_tile):
            yield slice(r * row_tile, (r + 1) * row_tile), slice(c * col_tile, (c + 1) * col_tile)


def _rms(x, w):
    return x * lax.rsqrt(jnp.mean(x * x, axis=-1, keepdims=True) + EPS) * w


def _dot(a, b):
    return jnp.dot(a, b, preferred_element_type=F32)


def _dot_nt(a, b):
    return lax.dot_general(a, b, (((1,), (1,)), ((), ())), preferred_element_type=F32)


def _sigmoid(x):
    return 0.5 * jnp.tanh(0.5 * x) + 0.5


def _gelu_tanh(x):
    c = math.sqrt(2.0 / math.pi)
    return x * (0.5 * jnp.tanh(x * (c + (c * 0.044715) * (x * x))) + 0.5)


def _residual_proj(x_ref, a, w_ref, o_ref):
    for _, cs in _tiles(1, o_ref.shape[1], 1, RESIDUAL_COL_TILE):
        o_ref[:, cs] = x_ref[:, cs] + _dot(a, _unpack(w_ref[:, cs]))


def _residual_proj_norm(x_ref, lhs, w_ref, o_ref, scale, norm_w, emit, w_rows=slice(None), base_ref=None):
    rows, d = o_ref.shape
    for rs, _ in _tiles(rows, 1, rows // 2, 1):
        a = lhs(rs)
        ss = jnp.zeros((a.shape[0], 1), F32)
        for _, cs in _tiles(1, d, 1, RESIDUAL_COL_TILE):
            y = _dot(a, _unpack(w_ref[w_rows, cs]))
            if base_ref is not None:
                y = base_ref[rs, cs] + y
            o = x_ref[rs, cs] + (y if scale is None else scale * y)
            o_ref[rs, cs] = o
            ss += jnp.sum(o * o, axis=-1, keepdims=True)
        emit(rs, o_ref[rs, :] * lax.rsqrt(ss * (1.0 / d) + EPS) * norm_w)


def _pack_kernel(w_ref, o_ref):
    o_ref[...] = _pack(w_ref[...])


def _pack_weight(w):
    k, n = w.shape
    rows = min(PACK_ROWS, k)
    cols = max(c for c in range(V7X_LANES, PACK_MAX_COLS + 1, V7X_LANES) if n % c == 0)
    assert k % rows == 0 and rows % (V7X_SUBLANES * PACK) == 0
    return pl.pallas_call(
        _pack_kernel,
        grid=(k // rows, n // cols),
        in_specs=[pl.BlockSpec((rows, cols), lambda i, j: (i, j))],
        out_specs=pl.BlockSpec((rows // PACK, cols), lambda i, j: (i, j)),
        out_shape=jax.ShapeDtypeStruct((k // PACK, n), U32),
        compiler_params=_params(("parallel", "parallel"), PACK_VMEM_RESERVATION),
        name="pack_weight",
    )(w)


def _pack_gate_up_kernel(g_ref, u_ref, o_ref):
    for _, cs in _tiles(1, g_ref.shape[1], 1, GATE_UP_LANES):
        o_ref[:, 2 * cs.start:2 * cs.start + GATE_UP_LANES] = _pack(g_ref[:, cs])
        o_ref[:, 2 * cs.start + GATE_UP_LANES:2 * cs.stop] = _pack(u_ref[:, cs])


def _pack_gate_up(w_gu):
    k, n = w_gu.shape
    rows = min(PACK_ROWS, k)
    n_col_steps = n // 2 // FFN_UP_COLS
    assert n % (2 * FFN_UP_COLS) == 0 and k % rows == 0
    return pl.pallas_call(
        _pack_gate_up_kernel,
        grid=(k // rows, n_col_steps),
        in_specs=[pl.BlockSpec((rows, FFN_UP_COLS), lambda i, j: (i, j)),
                  pl.BlockSpec((rows, FFN_UP_COLS), lambda i, j: (i, j + n_col_steps))],
        out_specs=pl.BlockSpec((rows // PACK, 2 * FFN_UP_COLS), lambda i, j: (i, j)),
        out_shape=jax.ShapeDtypeStruct((k // PACK, n), U32),
        compiler_params=_params(("parallel", "parallel"), PACK_VMEM_RESERVATION),
        name="pack_gate_up",
    )(w_gu, w_gu)


def _ffn_up_kernel(*refs, norm_input):
    if norm_input:
        x_ref, nw_ref, w_ref, o_ref, n_scr = refs

        @pl.when(pl.program_id(1) == 0)
        def _():
            n_scr[...] = _rms(x_ref[...], nw_ref[...]).astype(BF16)

        lhs = lambda rs: n_scr[rs, :]
    else:
        n_ref, w_ref, o_ref = refs
        lhs = lambda rs: _unpack(n_ref[_packed(rs), :])
    for rs, cs in _tiles(o_ref.shape[0] * PACK, w_ref.shape[1], MATMUL_ROW_TILE, 2 * GATE_UP_LANES):
        z = _dot(lhs(rs), _unpack(w_ref[:, cs]))
        g, u = z[:, :GATE_UP_LANES], z[:, GATE_UP_LANES:]
        o_ref[_packed(rs), cs.start // 2:cs.stop // 2] = _pack(g * _sigmoid(g) * u)


def _ffn_up(x_or_n, norm_w, w_gu, t):
    d = x_or_n.shape[1]
    d_ff = w_gu.shape[1] // 2
    assert t % FFN_UP_ROWS == 0 and d_ff % FFN_UP_COLS == 0 and FFN_UP_COLS % GATE_UP_LANES == 0
    norm_input = norm_w is not None
    w_spec = pl.BlockSpec((d // PACK, 2 * FFN_UP_COLS), lambda i, j: (0, j))
    if norm_input:
        in_specs = [pl.BlockSpec((FFN_UP_ROWS, d), lambda i, j: (i, 0)), pl.BlockSpec((1, d), lambda i, j: (0, 0)), w_spec]
        args = (x_or_n, norm_w, w_gu)
        scratch = [pltpu.VMEM((FFN_UP_ROWS, d), BF16)]
    else:
        in_specs = [pl.BlockSpec((FFN_UP_ROWS // PACK, d), lambda i, j: (i, 0)), w_spec]
        args = (x_or_n, w_gu)
        scratch = []
    vmem = (2 * _nbytes((FFN_UP_ROWS, d), F32 if norm_input else BF16) + _nbytes((FFN_UP_ROWS, d), BF16)
            + 4 * _nbytes((d, FFN_UP_COLS), BF16) + 2 * _nbytes((FFN_UP_ROWS, FFN_UP_COLS), BF16)
            + 8 * _nbytes((MATMUL_ROW_TILE, 2 * GATE_UP_LANES), F32))
    return pl.pallas_call(
        functools.partial(_ffn_up_kernel, norm_input=norm_input),
        grid=(t // FFN_UP_ROWS, d_ff // FFN_UP_COLS),
        in_specs=in_specs,
        out_specs=pl.BlockSpec((FFN_UP_ROWS // PACK, FFN_UP_COLS), lambda i, j: (i, j)),
        out_shape=jax.ShapeDtypeStruct((t // PACK, d_ff), U32),
        scratch_shapes=scratch,
        compiler_params=_params(("parallel", "arbitrary"), vmem),
        name="ffn_up_norm" if norm_input else "ffn_up",
    )(*args)


def _ffn_down_kernel(x_ref, hid_ref, wd_ref, nw_ref, o_ref, *n_out, final_norm):
    def emit(rs, normed):
        if final_norm:
            o_ref[rs, :] = normed
        else:
            n_out[0][_packed(rs), :] = _pack(normed)

    _residual_proj_norm(x_ref, lambda rs: _unpack(hid_ref[_packed(rs), :]), wd_ref, o_ref, FFN_RES_SCALE,
                        nw_ref[...], emit)


def _ffn_down(x, hidden, w_down, norm_w, final_norm):
    t, d = x.shape
    d_ff = hidden.shape[1]
    assert t % FFN_DOWN_ROWS == 0
    row_spec = lambda: pl.BlockSpec((FFN_DOWN_ROWS, d), lambda i: (i, 0))
    packed_spec = lambda cols: pl.BlockSpec((FFN_DOWN_ROWS // PACK, cols), lambda i: (i, 0))
    vmem = (4 * _nbytes((FFN_DOWN_ROWS, d), F32) + 2 * _nbytes((FFN_DOWN_ROWS, d_ff), BF16)
            + _nbytes((d_ff, d), BF16) + 2 * _nbytes((FFN_DOWN_ROWS, d), BF16)
            + 4 * _nbytes((FFN_DOWN_ROWS, RESIDUAL_COL_TILE), F32))
    out_f32 = jax.ShapeDtypeStruct((t, d), F32)
    return pl.pallas_call(
        functools.partial(_ffn_down_kernel, final_norm=final_norm),
        grid=(t // FFN_DOWN_ROWS,),
        in_specs=[row_spec(), packed_spec(d_ff), _resident((d_ff // PACK, d)), pl.BlockSpec((1, d), lambda i: (0, 0))],
        out_specs=row_spec() if final_norm else (row_spec(), packed_spec(d)),
        out_shape=out_f32 if final_norm else (out_f32, jax.ShapeDtypeStruct((t // PACK, d), U32)),
        compiler_params=_params(("parallel",), vmem),
        name="ffn_down_final" if final_norm else "ffn_down",
    )(x, hidden, w_down, norm_w)


def _in_proj_kernel(n_ref, w_ref, *refs, kind, k_scale):
    seg = pl.program_id(0)
    rows, d = n_ref.shape[0] * PACK, n_ref.shape[1]
    o_ref = refs[1] if kind == "gelu_layernorm" else refs[-1]

    def z_tiles():
        for rs, cs in _tiles(rows, d, MATMUL_ROW_TILE, V7X_MXU_COLS):
            yield rs, cs, _dot(_unpack(n_ref[_packed(rs), :]), _unpack(w_ref[:, cs]))

    def store(rs, cs, val):
        o_ref[0, _packed(rs), cs] = _pack(val)

    if kind in ("rotary", "rotary_transposed"):
        cos_ref, sin_ref, _ = refs
        half = V7X_MXU_COLS // 2
        for rs, cs, z in z_tiles():
            x1, x2 = z[:, :half], z[:, half:]
            cos, sin = cos_ref[rs, :], sin_ref[rs, :]
            if kind == "rotary":
                store(rs, slice(cs.start, cs.start + half), x1 * cos - x2 * sin)
                store(rs, slice(cs.start + half, cs.stop), x1 * sin + x2 * cos)
            else:
                rot = jnp.concatenate([x1 * cos - x2 * sin, x1 * sin + x2 * cos], axis=1) * k_scale
                rot_t = _pack(rot.T)
                for j in range(MATMUL_ROW_TILE // RET_BLOCK):
                    o_ref[rs.start // RET_BLOCK + j, cs.start // V7X_MXU_COLS] = rot_t[:, j * RET_BLOCK:(j + 1) * RET_BLOCK]
    elif kind == "plain":
        for rs, cs, z in z_tiles():
            store(rs, cs, z)
    elif kind == "silu_gain":
        gain_ref, _ = refs
        for rs, cs, z in z_tiles():
            store(rs, cs, z * _sigmoid(z) * gain_ref[:, cs])
    elif kind == "gelu":
        for rs, cs, z in z_tiles():
            store(rs, cs, _gelu_tanh(z))
    elif kind == "gelu_layernorm":
        lnw_ref, _, a_scr = refs
        for rs, _ in _tiles(rows, 1, LN_ROW_TILE, 1):
            for _, cs in _tiles(1, d, 1, V7X_MXU_COLS):
                a_scr[rs, cs] = _gelu_tanh(_dot(_unpack(n_ref[_packed(rs), :]), _unpack(w_ref[:, cs])))
            a = a_scr[rs, :]
            cen = a - jnp.mean(a, axis=-1, keepdims=True)
            var = jnp.mean(cen * cen, axis=-1, keepdims=True)
            o_ref[0, _packed(rs), :] = _pack(cen * lax.rsqrt(var + EPS) * lnw_ref[...])
    elif kind == "sigmoid_bias":
        gb_ref, _ = refs
        for rs, cs, z in z_tiles():
            store(rs, cs, _sigmoid(z + gb_ref[pl.ds(seg, 1), cs]))
    else:
        raise ValueError(kind)


def _in_proj(n, w_in, first_seg, n_seg, kind, extra=(), extra_specs=(), rows=IN_ROWS, k_scale=None):
    t, d = n.shape[0] * PACK, n.shape[1]
    assert t % rows == 0 and w_in.shape[1] % d == 0
    scratch = [pltpu.VMEM((rows, d), F32)] if kind == "gelu_layernorm" else []
    vmem = (4 * _nbytes((rows, d), BF16) + 2 * _nbytes((d, d), BF16) + 4 * _nbytes((rows, d // RET_HEADS // 2), F32)
            + (2 * _nbytes((rows, d), F32) if scratch else 0) + 16 * _nbytes((MATMUL_ROW_TILE, V7X_MXU_COLS), F32))
    if kind in ("rotary", "rotary_transposed"):
        assert d // RET_HEADS == V7X_MXU_COLS and MATMUL_ROW_TILE % RET_BLOCK == 0
    if kind == "rotary_transposed":
        assert n_seg == 1
        head = d // RET_HEADS
        out_spec = pl.BlockSpec((rows // RET_BLOCK, RET_HEADS, head // PACK, RET_BLOCK), lambda s, i: (i, 0, 0, 0))
        out_shape = jax.ShapeDtypeStruct((t // RET_BLOCK, RET_HEADS, head // PACK, RET_BLOCK), U32)
    else:
        out_spec = pl.BlockSpec((1, rows // PACK, d), lambda s, i: (s, i, 0))
        out_shape = jax.ShapeDtypeStruct((n_seg, t // PACK, d), U32)
    return pl.pallas_call(
        functools.partial(_in_proj_kernel, kind=kind, k_scale=k_scale),
        grid=(n_seg, t // rows),
        in_specs=[pl.BlockSpec((rows // PACK, d), lambda s, i: (i, 0)),
                  pl.BlockSpec((d // PACK, d), lambda s, i: (0, first_seg + s))] + list(extra_specs),
        out_specs=out_spec,
        out_shape=out_shape,
        scratch_shapes=scratch,
        compiler_params=_params(("parallel", "parallel"), vmem),
        name="in_proj_" + kind,
    )(n, w_in, *extra)


def _decay_rate(dec_ref, h, shape):
    return -jnp.exp(jnp.full(shape, dec_ref[h], F32))


def _ret_bwd_state_kernel(decb_ref, kt_ref, v_ref, sb_ref, state, kdec, cdec):
    c = pl.program_id(1)
    blk = kt_ref.shape[3]
    head = v_ref.shape[2] // RET_HEADS

    @pl.when((pl.program_id(0) == 0) & (c == 0))
    def _():
        col = lax.broadcasted_iota(jnp.int32, (1, blk), 1).astype(F32)
        for h in range(RET_HEADS):
            kdec[h] = jnp.exp(_decay_rate(decb_ref, h, (1, blk)) * col)
            cdec[h] = jnp.exp(_decay_rate(decb_ref, h, (1, head)) * float(blk))

    @pl.when(c == 0)
    def _():
        state[...] = jnp.zeros_like(state)

    for j in reversed(range(kt_ref.shape[0])):
        rows = slice(j * blk // PACK, (j + 1) * blk // PACK)
        for h in range(RET_HEADS):
            sl = slice(h * head, (h + 1) * head)
            st = state[h]
            sb_ref[0, j, h] = _pack(st)
            kd = (_unpack(kt_ref[j, h]).astype(F32) * kdec[h]).astype(BF16)
            state[h] = st * cdec[h] + _dot(kd, _unpack(v_ref[0, rows, sl]))


def _ret_bwd_states(kt, v, dec_b, batch):
    _, t_packed, d = v.shape
    n_blk = t_packed * PACK // batch // RET_BLOCK
    head = d // RET_HEADS
    per_step = RET_BWD_CHUNKS_PER_STEP
    n_steps = n_blk // per_step
    assert n_blk % per_step == 0
    blk_bytes = per_step * _nbytes((RET_BLOCK, d), BF16)
    state_bytes = _nbytes((RET_HEADS, head, head), F32)
    return pl.pallas_call(
        _ret_bwd_state_kernel,
        grid=(batch, n_steps),
        in_specs=[
            pl.BlockSpec(memory_space=pltpu.SMEM),
            pl.BlockSpec((per_step, RET_HEADS, head // PACK, RET_BLOCK),
                         lambda b, c: (b * n_steps + (n_steps - 1 - c), 0, 0, 0)),
            pl.BlockSpec((1, per_step * RET_BLOCK // PACK, d), lambda b, c: (0, b * n_steps + (n_steps - 1 - c), 0)),
        ],
        out_specs=pl.BlockSpec((1, per_step, RET_HEADS, head // PACK, head), lambda b, c: (b, n_steps - 1 - c, 0, 0, 0)),
        out_shape=jax.ShapeDtypeStruct((batch, n_blk, RET_HEADS, head // PACK, head), U32),
        scratch_shapes=[
            pltpu.VMEM((RET_HEADS, head, head), F32),
            pltpu.VMEM((RET_HEADS, 1, RET_BLOCK), F32),
            pltpu.VMEM((RET_HEADS, 1, head), F32),
        ],
        compiler_params=_params(("arbitrary", "arbitrary"), 6 * blk_bytes + 4 * state_bytes),
        name="ret_bwd_states",
    )(dec_b, kt, v)


def _project_slice(act_ref, w_ref, acc_ref, i):
    width = act_ref.shape[2]
    w_rows = slice(i * width // PACK, (i + 1) * width // PACK)
    for _, cs in _tiles(1, acc_ref.shape[1], 1, RESIDUAL_COL_TILE):
        part = _dot(act_ref[i], _unpack(w_ref[w_rows, cs]))
        acc_ref[:, cs] = part if i == 0 else acc_ref[:, cs] + part


def _ret_out_kernel(decf_ref, decb_ref, q_ref, kt_ref, v_ref, g_ref, sb_ref, wr_ref, o_ref,
                    state, dmat, qdf, qdb, kdf, cdf, act, acc):
    c = pl.program_id(1)
    blk = RET_BLOCK
    head = q_ref.shape[2] // RET_HEADS

    @pl.when((pl.program_id(0) == 0) & (c == 0))
    def _():
        row = lax.broadcasted_iota(jnp.int32, (blk, head), 0).astype(F32)
        col = lax.broadcasted_iota(jnp.int32, (1, blk), 1).astype(F32)
        ri = lax.broadcasted_iota(jnp.int32, (blk, blk), 0)
        ci = lax.broadcasted_iota(jnp.int32, (blk, blk), 1)
        dist = (ri - ci).astype(F32)
        for h in range(RET_HEADS):
            qdf[h] = jnp.exp(_decay_rate(decf_ref, h, (blk, head)) * (row + 1.0))
            qdb[h] = jnp.exp(_decay_rate(decb_ref, h, (blk, head)) * (float(blk) - row))
            kdf[h] = jnp.exp(_decay_rate(decf_ref, h, (1, blk)) * (float(blk) - 1.0 - col))
            cdf[h] = jnp.exp(_decay_rate(decf_ref, h, (1, head)) * float(blk))
            lf2 = _decay_rate(decf_ref, h, (blk, blk))
            lb2 = _decay_rate(decb_ref, h, (blk, blk))
            dmat[h] = jnp.where(ri >= ci, jnp.exp(lf2 * jnp.maximum(dist, 0.0)), jnp.exp(lb2 * jnp.maximum(-dist, 0.0)))

    @pl.when(c == 0)
    def _():
        state[...] = jnp.zeros_like(state)

    n_sub = kt_ref.shape[0]

    def project(item):
        j, h = divmod(item, RET_HEADS)
        _project_slice(act.at[j], wr_ref, acc.at[j], h)
        if h == RET_HEADS - 1:
            o_ref[j * blk // PACK:(j + 1) * blk // PACK, :] = _pack(acc[j])

    for item in range(n_sub * RET_HEADS):
        j, h = divmod(item, RET_HEADS)
        rows = slice(j * blk // PACK, (j + 1) * blk // PACK)
        sl = slice(h * head, (h + 1) * head)
        q = _unpack(q_ref[0, rows, sl])
        kt = _unpack(kt_ref[j, h])
        v = _unpack(v_ref[0, rows, sl])
        st = state[h]
        scores = _dot(q, kt)
        cross = _dot(q, st.astype(BF16)) * qdf[h] + _dot(q, _unpack(sb_ref[0, j, h])) * qdb[h]
        state[h] = st * cdf[h] + _dot((kt.astype(F32) * kdf[h]).astype(BF16), v)
        o = _dot((scores * dmat[h]).astype(BF16), v) + cross
        cen = o - jnp.mean(o, axis=-1, keepdims=True)
        var = jnp.mean(cen * cen, axis=-1, keepdims=True)
        act[j, h] = (_unpack(g_ref[0, rows, sl]).astype(F32) * (cen * lax.rsqrt(var + EPS))).astype(BF16)
        if item >= RET_PROJECT_LAG:
            project(item - RET_PROJECT_LAG)
    for item in range(n_sub * RET_HEADS - RET_PROJECT_LAG, n_sub * RET_HEADS):
        project(item)


def _ret_out(q, kt, v, g, sb, dec_f, dec_b, w_ret_out, batch):
    _, t_packed, d = q.shape
    n_blk = t_packed * PACK // batch // RET_BLOCK
    head = d // RET_HEADS
    per_step = RET_OUT_CHUNKS_PER_STEP
    n_steps = n_blk // per_step
    assert n_blk % per_step == 0
    rows = lambda: pl.BlockSpec((1, per_step * RET_BLOCK // PACK, d), lambda b, c: (0, b * n_steps + c, 0))
    blk_bytes = per_step * _nbytes((RET_BLOCK, d), BF16)
    table_bytes = _nbytes((RET_HEADS, RET_BLOCK, head), F32)
    return pl.pallas_call(
        _ret_out_kernel,
        grid=(batch, n_steps),
        in_specs=[
            pl.BlockSpec(memory_space=pltpu.SMEM),
            pl.BlockSpec(memory_space=pltpu.SMEM),
            rows(),
            pl.BlockSpec((per_step, RET_HEADS, head // PACK, RET_BLOCK), lambda b, c: (b * n_steps + c, 0, 0, 0)),
            rows(), rows(),
            pl.BlockSpec((1, per_step, RET_HEADS, head // PACK, head), lambda b, c: (b, c, 0, 0, 0)),
            _resident((d // PACK, d)),
        ],
        out_specs=pl.BlockSpec((per_step * RET_BLOCK // PACK, d), lambda b, c: (b * n_steps + c, 0)),
        out_shape=jax.ShapeDtypeStruct((t_packed, d), U32),
        scratch_shapes=[
            pltpu.VMEM((RET_HEADS, head, head), F32),
            pltpu.VMEM((RET_HEADS, RET_BLOCK, RET_BLOCK), F32),
            pltpu.VMEM((RET_HEADS, RET_BLOCK, head), F32),
            pltpu.VMEM((RET_HEADS, RET_BLOCK, head), F32),
            pltpu.VMEM((RET_HEADS, 1, RET_BLOCK), F32),
            pltpu.VMEM((RET_HEADS, 1, head), F32),
            pltpu.VMEM((per_step, RET_HEADS, RET_BLOCK, head), BF16),
            pltpu.VMEM((per_step, RET_BLOCK, d), F32),
        ],
        compiler_params=_params(("arbitrary", "arbitrary"),
                                12 * blk_bytes + 6 * table_bytes + _nbytes((d, d), BF16)
                                + (per_step + 3) * _nbytes((RET_BLOCK, d), F32)),
        name="ret_out",
    )(dec_f, dec_b, q, kt, v, g, sb, w_ret_out)


def _mix_out_kernel(u_ref, vs_ref, ws_ref, bs_ref, ret_ref, gr_ref, gs_ref, h_ref, wsg_ref, wo_ref, o_ref, act, acc):
    rows, d = h_ref.shape
    gdim = d // SGU_GROUPS
    for g in range(SGU_GROUPS):
        cs = slice(g * gdim, (g + 1) * gdim)
        for rs, _ in _tiles(rows, 1, SGU_CHUNK, 1):
            mixed = _dot(ws_ref[g], _unpack(vs_ref[0, _packed(rs), cs])) + bs_ref[g]
            act[g, rs, :] = (_unpack(u_ref[0, _packed(rs), cs]).astype(F32) * mixed).astype(BF16)
        if g >= SGU_PROJECT_LAG:
            _project_slice(act, wsg_ref, acc, g - SGU_PROJECT_LAG)
    for g in range(SGU_GROUPS - SGU_PROJECT_LAG, SGU_GROUPS):
        _project_slice(act, wsg_ref, acc, g)
    for k in range(SGU_GROUPS):
        ks = slice(k * gdim, (k + 1) * gdim)
        merged = (_unpack(gr_ref[0, :, ks]).astype(F32) * _unpack(ret_ref[:, ks]).astype(F32)
                  + _unpack(gs_ref[0, :, ks]).astype(F32) * acc[:, ks]).astype(BF16)
        w_rows = slice(k * gdim // PACK, (k + 1) * gdim // PACK)
        for _, cs in _tiles(1, d, 1, RESIDUAL_COL_TILE):
            o_ref[:, cs] = (h_ref[:, cs] if k == 0 else o_ref[:, cs]) + _dot(merged, _unpack(wo_ref[w_rows, cs]))


def _mix_out(u, vs, w_s, b_s, ret_proj, gates, h, w_sgu_out, w_out):
    t, d = h.shape
    gdim = d // SGU_GROUPS
    assert t % OUT_ROWS == 0 and OUT_ROWS % SGU_CHUNK == 0
    rows = lambda: pl.BlockSpec((OUT_ROWS, d), lambda i: (i, 0))
    packed_rows = lambda: pl.BlockSpec((OUT_ROWS // PACK, d), lambda i: (i, 0))
    seg_rows = lambda s: pl.BlockSpec((1, OUT_ROWS // PACK, d), lambda i: (s, i, 0))
    weight = lambda: _resident((d // PACK, d))
    vmem = (2 * _nbytes((d, d), BF16) + 10 * _nbytes((OUT_ROWS, d), BF16) + 4 * _nbytes((OUT_ROWS, d), F32)
            + _nbytes((OUT_ROWS, d), BF16) + 4 * _nbytes((OUT_ROWS, d), F32))
    return pl.pallas_call(
        _mix_out_kernel,
        grid=(t // OUT_ROWS,),
        in_specs=[
            seg_rows(0), seg_rows(0),
            pl.BlockSpec((SGU_GROUPS, SGU_CHUNK, SGU_CHUNK), lambda i: (0, 0, 0)),
            pl.BlockSpec((SGU_GROUPS, SGU_CHUNK, 1), lambda i: (0, 0, 0)),
            packed_rows(), seg_rows(0), seg_rows(1), rows(), weight(), weight(),
        ],
        out_specs=rows(),
        out_shape=jax.ShapeDtypeStruct((t, d), F32),
        scratch_shapes=[pltpu.VMEM((SGU_GROUPS, OUT_ROWS, gdim), BF16), pltpu.VMEM((OUT_ROWS, d), F32)],
        compiler_params=_params(("parallel",), vmem),
        name="mix_out",
    )(u, vs, w_s, b_s, ret_proj, gates, gates, h, w_sgu_out, w_out)


def _mem_kv_kernel(m_ref, nw_ref, w_ref, o_ref, n_scr):
    @pl.when(pl.program_id(1) == 0)
    def _():
        n_scr[...] = _rms(m_ref[...], nw_ref[...]).astype(BF16)

    o_ref[...] = _pack(_dot(n_scr[...], _unpack(w_ref[...])))


def _mem_kv(mem, norm_w, w_kv):
    t, d = mem.shape
    rows = min(KV_ROWS, t)
    n_out = w_kv.shape[1]
    assert t % rows == 0 and n_out % KV_COLS == 0
    vmem = (2 * _nbytes((rows, d), F32) + _nbytes((rows, d), BF16) + 2 * _nbytes((d, KV_COLS), BF16)
            + 2 * _nbytes((rows, KV_COLS), BF16) + 2 * _nbytes((rows, d), F32))
    return pl.pallas_call(
        _mem_kv_kernel,
        grid=(t // rows, n_out // KV_COLS),
        in_specs=[
            pl.BlockSpec((rows, d), lambda i, j: (i, 0)),
            pl.BlockSpec((1, d), lambda i, j: (0, 0)),
            pl.BlockSpec((d // PACK, KV_COLS), lambda i, j: (0, j)),
        ],
        out_specs=pl.BlockSpec((rows // PACK, KV_COLS), lambda i, j: (i, j)),
        out_shape=jax.ShapeDtypeStruct((t // PACK, n_out), U32),
        scratch_shapes=[pltpu.VMEM((rows, d), BF16)],
        compiler_params=_params(("parallel", "arbitrary"), vmem),
        name="mem_kv",
    )(mem, norm_w, w_kv)


def _xattn_kernel(h_ref, nw_ref, kv_ref, wq_ref, wo_ref, next_nw_ref, o_ref, n_out_ref, n_scr, q_scr, a_scr, acc):
    rows, d = h_ref.shape
    hd = d // XATTN_HEADS
    scale = float(hd) ** -0.5
    head_cols = lambda h: slice(h * hd, (h + 1) * hd)

    def project_q(h, rs=slice(None)):
        q_scr[rs, head_cols(h)] = _dot(n_scr[rs, :], _unpack(wq_ref[:, head_cols(h)])).astype(BF16)

    def probs(h):
        s = _dot_nt(q_scr[:, head_cols(h)], _unpack(kv_ref[:, head_cols(h)])) * scale
        e = jnp.exp(s - jnp.max(s, axis=-1, keepdims=True))
        return (e / jnp.sum(e, axis=-1, keepdims=True)).astype(BF16)

    def emit(rs, normed):
        n_out_ref[_packed(rs), :] = _pack(normed)

    for rs, _ in _tiles(rows, 1, rows // 2, 1):
        n_scr[rs, :] = _rms(h_ref[rs, :], nw_ref[...]).astype(BF16)
        project_q(0, rs)
    last = XATTN_HEADS - 1
    p = probs(0)
    for h in range(XATTN_HEADS):
        if h < last:
            project_q(h + 1)
        p_next = probs(h + 1) if h < last else None
        a_scr[h] = _dot(p, _unpack(kv_ref[:, d + h * hd:d + (h + 1) * hd])).astype(BF16)
        if h < last:
            _project_slice(a_scr, wo_ref, acc, h)
        else:
            _residual_proj_norm(h_ref, lambda rs: a_scr[last, rs, :], wo_ref, o_ref, None, next_nw_ref[...], emit,
                                w_rows=slice(last * hd // PACK, (last + 1) * hd // PACK), base_ref=acc)
        p = p_next


def _xattn(h, kv, norm_w, w_q, w_o, next_norm_w, batch):
    t, d = h.shape
    n_mem = kv.shape[0] * PACK // batch
    row_blocks = t // batch // XATTN_ROWS
    assert t % (batch * XATTN_ROWS) == 0
    rows = lambda: pl.BlockSpec((XATTN_ROWS, d), lambda b, i: (b * row_blocks + i, 0))
    vec = lambda: pl.BlockSpec((1, d), lambda b, i: (0, 0))
    weight = lambda: _resident((d // PACK, d))
    vmem = (2 * _nbytes((d, d), BF16) + 5 * _nbytes((XATTN_ROWS, d), F32) + 5 * _nbytes((XATTN_ROWS, d), BF16)
            + 2 * _nbytes((n_mem, 2 * d), BF16) + 2 * _nbytes((XATTN_ROWS, d), F32))
    return pl.pallas_call(
        _xattn_kernel,
        grid=(batch, row_blocks),
        in_specs=[rows(), vec(), pl.BlockSpec((n_mem // PACK, 2 * d), lambda b, i: (b, 0)), weight(), weight(), vec()],
        out_specs=(rows(), pl.BlockSpec((XATTN_ROWS // PACK, d), lambda b, i: (b * row_blocks + i, 0))),
        out_shape=(jax.ShapeDtypeStruct((t, d), F32), jax.ShapeDtypeStruct((t // PACK, d), U32)),
        scratch_shapes=[pltpu.VMEM((XATTN_ROWS, d), BF16), pltpu.VMEM((XATTN_ROWS, d), BF16),
                        pltpu.VMEM((XATTN_HEADS, XATTN_ROWS, d // XATTN_HEADS), BF16),
                        pltpu.VMEM((XATTN_ROWS, d), F32)],
        compiler_params=_params(("parallel", "parallel"), vmem),
        name="xattn",
    )(h, norm_w, kv, w_q, w_o, next_norm_w)


def _rope_tables(seq, half):
    freqs = ROPE_BASE ** (-jnp.linspace(0.0, 1.0, half, dtype=F32))
    ang = jnp.arange(seq, dtype=F32)[:, None] * freqs[None, :]
    return jnp.cos(ang), jnp.sin(ang)


def _trunk(x, mem, w):
    batch, seq, d = x.shape
    t = batch * seq
    x = x.reshape(t, d)
    half = d // RET_HEADS // 2
    cos, sin = _rope_tables(seq, half)
    pos_blocks = seq // IN_ROWS
    assert seq % IN_ROWS == 0
    pos_spec = lambda: pl.BlockSpec((IN_ROWS, half), lambda s, i: (i % pos_blocks, 0))

    hidden = _ffn_up(x, w["ffn1_norm"], w["ffn1_w_gu"], t)
    h, n = _ffn_down(x, hidden, w["ffn1_w_down"], w["mix_norm"], final_norm=False)

    w_in = w["w_in"]
    vec_spec = lambda: pl.BlockSpec((1, d), lambda s, i: (0, 0))
    q = _in_proj(n, w_in, SEG_Q, 1, "rotary", (cos, sin), (pos_spec(), pos_spec()))
    kt = _in_proj(n, w_in, SEG_K, 1, "rotary_transposed", (cos, sin), (pos_spec(), pos_spec()),
                  k_scale=float(2 * half) ** -0.5)
    v = _in_proj(n, w_in, SEG_V, 1, "plain")
    g = _in_proj(n, w_in, SEG_G, 1, "silu_gain", (w["ret_gn_w"],), (vec_spec(),))
    u = _in_proj(n, w_in, SEG_U, 1, "gelu")
    vs = _in_proj(n, w_in, SEG_VS, 1, "gelu_layernorm", (w["sgu_norm_w"],), (vec_spec(),), rows=IN_LN_ROWS)
    gates = _in_proj(n, w_in, SEG_GATE_R, 2, "sigmoid_bias", (w["gate_bias"],),
                     (pl.BlockSpec((2, d), lambda s, i: (0, 0)),))

    sb = _ret_bwd_states(kt, v, w["ret_decay_bwd"], batch)
    ret_proj = _ret_out(q, kt, v, g, sb, w["ret_decay_fwd"], w["ret_decay_bwd"], w["w_ret_out"], batch)
    h = _mix_out(u, vs, w["sgu_w_s"], w["sgu_b_s"], ret_proj, gates, h, w["w_sgu_out"], w["w_out"])

    kv = _mem_kv(mem.reshape(-1, d), w["xattn_norm_mem"], w["xattn_w_kv"])
    h, n = _xattn(h, kv, w["xattn_norm_q"], w["xattn_w_q"], w["xattn_w_o"], w["ffn2_norm"], batch)

    hidden = _ffn_up(n, None, w["ffn2_w_gu"], t)
    y = _ffn_down(h, hidden, w["ffn2_w_down"], w["final_norm"], final_norm=True)
    return y.reshape(batch, seq, d)


def _prepare(ffn1_norm, ffn1_w_gu, ffn1_w_down, mix_norm, w_in, gate_bias, ret_decay_fwd, ret_decay_bwd, ret_gn_w, w_ret_out, sgu_norm_w, sgu_w_s, sgu_b_s, w_sgu_out, w_out, xattn_norm_q, xattn_norm_mem, xattn_w_q, xattn_w_kv, xattn_w_o, ffn2_norm, ffn2_w_gu, ffn2_w_down, final_norm):
    assert ffn1_norm.shape[0] == 1, "single-layer trunk"
    pw = lambda w: _pack_weight(w[0])
    return {
        "ffn1_norm": ffn1_norm, "ffn1_w_gu": _pack_gate_up(ffn1_w_gu[0]), "ffn1_w_down": pw(ffn1_w_down),
        "mix_norm": mix_norm, "w_in": pw(w_in), "gate_bias": gate_bias[0],
        "ret_decay_fwd": ret_decay_fwd[0], "ret_decay_bwd": ret_decay_bwd[0], "ret_gn_w": ret_gn_w,
        "w_ret_out": pw(w_ret_out), "sgu_norm_w": sgu_norm_w,
        "sgu_w_s": sgu_w_s[0].astype(BF16), "sgu_b_s": sgu_b_s[0][:, :, None],
        "w_sgu_out": pw(w_sgu_out), "w_out": pw(w_out),
        "xattn_norm_q": xattn_norm_q, "xattn_norm_mem": xattn_norm_mem,
        "xattn_w_q": pw(xattn_w_q), "xattn_w_kv": pw(xattn_w_kv), "xattn_w_o": pw(xattn_w_o),
        "ffn2_norm": ffn2_norm, "ffn2_w_gu": _pack_gate_up(ffn2_w_gu[0]), "ffn2_w_down": pw(ffn2_w_down),
        "final_norm": final_norm[None, :],
    }


def kernel(x_prompt, x_sample, mem_prompt, mem_sample, ffn1_norm, ffn1_w_gu, ffn1_w_down, mix_norm, w_in, gate_bias, ret_decay_fwd, ret_decay_bwd, ret_gn_w, w_ret_out, sgu_norm_w, sgu_w_s, sgu_b_s, w_sgu_out, w_out, xattn_norm_q, xattn_norm_mem, xattn_w_q, xattn_w_kv, xattn_w_o, ffn2_norm, ffn2_w_gu, ffn2_w_down, final_norm):
    w = _prepare(ffn1_norm, ffn1_w_gu, ffn1_w_down, mix_norm, w_in, gate_bias, ret_decay_fwd, ret_decay_bwd, ret_gn_w, w_ret_out, sgu_norm_w, sgu_w_s, sgu_b_s, w_sgu_out, w_out, xattn_norm_q, xattn_norm_mem, xattn_w_q, xattn_w_kv, xattn_w_o, ffn2_norm, ffn2_w_gu, ffn2_w_down, final_norm)
    return _trunk(x_prompt, mem_prompt, w), _trunk(x_sample, mem_sample, w)
```

```python
import functools
import math

import jax
import jax.numpy as jnp
from jax import lax
from jax.experimental import pallas as pl
from jax.experimental.pallas import tpu as pltpu

F32 = jnp.float32
BF16 = jnp.bfloat16
U32 = jnp.uint32
PACK = 2

EPS = 1e-6
FFN_RES_SCALE = 0.5
ROPE_BASE = 10000.0
RET_HEADS = 8
SGU_GROUPS = 8
SGU_CHUNK = 128
XATTN_HEADS = 4
SEG_Q, SEG_K, SEG_V, SEG_G, SEG_U, SEG_VS, SEG_GATE_R, SEG_GATE_S = range(8)

V7X_VMEM_BYTES = 64 * 1024 * 1024
V7X_SUBLANES = 8
V7X_LANES = 128
V7X_MXU_COLS = 256

RET_BLOCK = 256
RET_BWD_CHUNKS_PER_STEP = 4
RET_OUT_CHUNKS_PER_STEP = 2
RET_PROJECT_LAG = 2
SGU_PROJECT_LAG = 1

PACK_ROWS = 512
PACK_MAX_COLS = 2048
PACK_VMEM_RESERVATION = V7X_VMEM_BYTES - 8 * 1024 * 1024
FFN_UP_ROWS = 1024
FFN_UP_COLS = 1408
GATE_UP_LANES = V7X_LANES
FFN_DOWN_ROWS = 512
IN_ROWS = 1024
LN_ROW_TILE = 256
OUT_ROWS = 256
KV_ROWS = 512
KV_COLS = 1024
XATTN_ROWS = 512
RESIDUAL_COL_TILE = 512
MATMUL_ROW_TILE = 512


def _nbytes(shape, dtype):
    return math.prod(shape) * jnp.dtype(dtype).itemsize


def _params(semantics, vmem_bytes):
    assert vmem_bytes <= V7X_VMEM_BYTES, vmem_bytes
    return pltpu.CompilerParams(dimension_semantics=semantics, vmem_limit_bytes=int(vmem_bytes))


def _resident(shape):
    return pl.BlockSpec(shape, lambda *_: (0,) * len(shape), pipeline_mode=pl.Buffered(1))


def _unpack(words):
    return pltpu.bitcast(words, BF16)


def _pack(vals):
    return pltpu.bitcast(vals.astype(BF16), U32)


def _packed(rows):
    return slice(rows.start // PACK, rows.stop // PACK)


def _tiles(n_rows, n_cols, row_tile, col_tile):
    for r in range(n_rows // row_tile):
        for c in range(n_cols // col_tile):
            yield slice(r * row_tile, (r + 1) * row_tile), slice(c * col_tile, (c + 1) * col_tile)


def _rms(x, w):
    return x * lax.rsqrt(jnp.mean(x * x, axis=-1, keepdims=True) + EPS) * w


def _dot(a, b):
    return jnp.dot(a, b, preferred_element_type=F32)


def _dot_nt(a, b):
    return lax.dot_general(a, b, (((1,), (1,)), ((), ())), preferred_element_type=F32)


def _sigmoid(x):
    return 0.5 * jnp.tanh(0.5 * x) + 0.5


def _gelu_tanh(x):
    c = math.sqrt(2.0 / math.pi)
    return x * (0.5 * jnp.tanh(x * (c + (c * 0.044715) * (x * x))) + 0.5)


def _residual_proj(x_ref, a, w_ref, o_ref):
    for _, cs in _tiles(1, o_ref.shape[1], 1, RESIDUAL_COL_TILE):
        o_ref[:, cs] = x_ref[:, cs] + _dot(a, _unpack(w_ref[:, cs]))


def _residual_proj_norm(x_ref, lhs, w_ref, o_ref, scale, norm_w, emit, w_rows=slice(None), base_ref=None):
    rows, d = o_ref.shape
    for rs, _ in _tiles(rows, 1, rows // 2, 1):
        a = lhs(rs)
        ss = jnp.zeros((a.shape[0], 1), F32)
        for _, cs in _tiles(1, d, 1, RESIDUAL_COL_TILE):
            y = _dot(a, _unpack(w_ref[w_rows, cs]))
            if base_ref is not None:
                y = base_ref[rs, cs] + y
            o = x_ref[rs, cs] + (y if scale is None else scale * y)
            o_ref[rs, cs] = o
            ss += jnp.sum(o * o, axis=-1, keepdims=True)
        emit(rs, o_ref[rs, :] * lax.rsqrt(ss * (1.0 / d) + EPS) * norm_w)


def _pack_kernel(w_ref, o_ref):
    o_ref[...] = _pack(w_ref[...])


def _pack_weight(w):
    k, n = w.shape
    rows = min(PACK_ROWS, k)
    cols = max(c for c in range(V7X_LANES, PACK_MAX_COLS + 1, V7X_LANES) if n % c == 0)
    assert k % rows == 0 and rows % (V7X_SUBLANES * PACK) == 0
    return pl.pallas_call(
        _pack_kernel,
        grid=(k // rows, n // cols),
        in_specs=[pl.BlockSpec((rows, cols), lambda i, j: (i, j))],
        out_specs=pl.BlockSpec((rows // PACK, cols), lambda i, j: (i, j)),
        out_shape=jax.ShapeDtypeStruct((k // PACK, n), U32),
        compiler_params=_params(("parallel", "parallel"), PACK_VMEM_RESERVATION),
        name="pack_weight",
    )(w)


def _pack_gate_up_kernel(g_ref, u_ref, o_ref):
    for _, cs in _tiles(1, g_ref.shape[1], 1, GATE_UP_LANES):
        o_ref[:, 2 * cs.start:2 * cs.start + GATE_UP_LANES] = _pack(g_ref[:, cs])
        o_ref[:, 2 * cs.start + GATE_UP_LANES:2 * cs.stop] = _pack(u_ref[:, cs])


def _pack_gate_up(w_gu):
    k, n = w_gu.shape
    rows = min(PACK_ROWS, k)
    n_col_steps = n // 2 // FFN_UP_COLS
    assert n % (2 * FFN_UP_COLS) == 0 and k % rows == 0
    return pl.pallas_call(
        _pack_gate_up_kernel,
        grid=(k // rows, n_col_steps),
        in_specs=[pl.BlockSpec((rows, FFN_UP_COLS), lambda i, j: (i, j)),
                  pl.BlockSpec((rows, FFN_UP_COLS), lambda i, j: (i, j + n_col_steps))],
        out_specs=pl.BlockSpec((rows // PACK, 2 * FFN_UP_COLS), lambda i, j: (i, j)),
        out_shape=jax.ShapeDtypeStruct((k // PACK, n), U32),
        compiler_params=_params(("parallel", "parallel"), PACK_VMEM_RESERVATION),
        name="pack_gate_up",
    )(w_gu, w_gu)


def _ffn_up_kernel(*refs, norm_input):
    if norm_input:
        x_ref, nw_ref, w_ref, o_ref, n_scr = refs

        @pl.when(pl.program_id(1) == 0)
        def _():
            n_scr[...] = _rms(x_ref[...], nw_ref[...]).astype(BF16)

        lhs = lambda rs: n_scr[rs, :]
    else:
        n_ref, w_ref, o_ref = refs
        lhs = lambda rs: _unpack(n_ref[_packed(rs), :])
    for rs, cs in _tiles(o_ref.shape[0] * PACK, w_ref.shape[1], MATMUL_ROW_TILE, 2 * GATE_UP_LANES):
        z = _dot(lhs(rs), _unpack(w_ref[:, cs]))
        g, u = z[:, :GATE_UP_LANES], z[:, GATE_UP_LANES:]
        o_ref[_packed(rs), cs.start // 2:cs.stop // 2] = _pack(g * _sigmoid(g) * u)


def _ffn_up(x_or_n, norm_w, w_gu, t):
    d = x_or_n.shape[1]
    d_ff = w_gu.shape[1] // 2
    assert t % FFN_UP_ROWS == 0 and d_ff % FFN_UP_COLS == 0 and FFN_UP_COLS % GATE_UP_LANES == 0
    norm_input = norm_w is not None
    w_spec = pl.BlockSpec((d // PACK, 2 * FFN_UP_COLS), lambda i, j: (0, j))
    if norm_input:
        in_specs = [pl.BlockSpec((FFN_UP_ROWS, d), lambda i, j: (i, 0)), pl.BlockSpec((1, d), lambda i, j: (0, 0)), w_spec]
        args = (x_or_n, norm_w, w_gu)
        scratch = [pltpu.VMEM((FFN_UP_ROWS, d), BF16)]
    else:
        in_specs = [pl.BlockSpec((FFN_UP_ROWS // PACK, d), lambda i, j: (i, 0)), w_spec]
        args = (x_or_n, w_gu)
        scratch = []
    vmem = (2 * _nbytes((FFN_UP_ROWS, d), F32 if norm_input else BF16) + _nbytes((FFN_UP_ROWS, d), BF16)
            + 4 * _nbytes((d, FFN_UP_COLS), BF16) + 2 * _nbytes((FFN_UP_ROWS, FFN_UP_COLS), BF16)
            + 8 * _nbytes((MATMUL_ROW_TILE, 2 * GATE_UP_LANES), F32))
    return pl.pallas_call(
        functools.partial(_ffn_up_kernel, norm_input=norm_input),
        grid=(t // FFN_UP_ROWS, d_ff // FFN_UP_COLS),
        in_specs=in_specs,
        out_specs=pl.BlockSpec((FFN_UP_ROWS // PACK, FFN_UP_COLS), lambda i, j: (i, j)),
        out_shape=jax.ShapeDtypeStruct((t // PACK, d_ff), U32),
        scratch_shapes=scratch,
        compiler_params=_params(("parallel", "arbitrary"), vmem),
        name="ffn_up_norm" if norm_input else "ffn_up",
    )(*args)


def _ffn_down_kernel(x_ref, hid_ref, wd_ref, nw_ref, o_ref, *n_out, final_norm):
    def emit(rs, normed):
        if final_norm:
            o_ref[rs, :] = normed
        else:
            n_out[0][_packed(rs), :] = _pack(normed)

    _residual_proj_norm(x_ref, lambda rs: _unpack(hid_ref[_packed(rs), :]), wd_ref, o_ref, FFN_RES_SCALE,
                        nw_ref[...], emit)


def _ffn_down(x, hidden, w_down, norm_w, final_norm):
    t, d = x.shape
    d_ff = hidden.shape[1]
    assert t % FFN_DOWN_ROWS == 0
    row_spec = lambda: pl.BlockSpec((FFN_DOWN_ROWS, d), lambda i: (i, 0))
    packed_spec = lambda cols: pl.BlockSpec((FFN_DOWN_ROWS // PACK, cols), lambda i: (i, 0))
    vmem = (4 * _nbytes((FFN_DOWN_ROWS, d), F32) + 2 * _nbytes((FFN_DOWN_ROWS, d_ff), BF16)
            + _nbytes((d_ff, d), BF16) + 2 * _nbytes((FFN_DOWN_ROWS, d), BF16)
            + 4 * _nbytes((FFN_DOWN_ROWS, RESIDUAL_COL_TILE), F32))
    out_f32 = jax.ShapeDtypeStruct((t, d), F32)
    return pl.pallas_call(
        functools.partial(_ffn_down_kernel, final_norm=final_norm),
        grid=(t // FFN_DOWN_ROWS,),
        in_specs=[row_spec(), packed_spec(d_ff), _resident((d_ff // PACK, d)), pl.BlockSpec((1, d), lambda i: (0, 0))],
        out_specs=row_spec() if final_norm else (row_spec(), packed_spec(d)),
        out_shape=out_f32 if final_norm else (out_f32, jax.ShapeDtypeStruct((t // PACK, d), U32)),
        compiler_params=_params(("parallel",), vmem),
        name="ffn_down_final" if final_norm else "ffn_down",
    )(x, hidden, w_down, norm_w)


def _in_proj_kernel(n_ref, w_ref, cos_ref, sin_ref, gain_ref, lnw_ref, gb_ref, o_ref, a_scr, *, k_scale):
    seg = pl.program_id(0)
    rows, d = n_ref.shape[0] * PACK, n_ref.shape[1]
    half = V7X_MXU_COLS // 2

    def z_tiles():
        for rs, cs in _tiles(rows, d, MATMUL_ROW_TILE, V7X_MXU_COLS):
            yield rs, cs, _dot(_unpack(n_ref[_packed(rs), :]), _unpack(w_ref[:, cs]))

    def store(rs, cs, val):
        o_ref[0, _packed(rs), cs] = _pack(val)

    def rotated(rs, z):
        x1, x2 = z[:, :half], z[:, half:]
        cos, sin = cos_ref[rs, :], sin_ref[rs, :]
        return x1 * cos - x2 * sin, x1 * sin + x2 * cos

    @pl.when(seg == SEG_Q)
    def _():
        for rs, cs, z in z_tiles():
            r1, r2 = rotated(rs, z)
            store(rs, slice(cs.start, cs.start + half), r1)
            store(rs, slice(cs.start + half, cs.stop), r2)

    @pl.when(seg == SEG_K)
    def _():
        for rs, cs, z in z_tiles():
            rot_t = _pack((jnp.concatenate(rotated(rs, z), axis=1) * k_scale).T)
            for j in range(MATMUL_ROW_TILE // RET_BLOCK):
                chunk_rows = slice(rs.start + j * RET_BLOCK, rs.start + (j + 1) * RET_BLOCK)
                o_ref[0, _packed(chunk_rows), cs] = rot_t[:, j * RET_BLOCK:(j + 1) * RET_BLOCK]

    @pl.when(seg == SEG_V)
    def _():
        for rs, cs, z in z_tiles():
            store(rs, cs, z)

    @pl.when(seg == SEG_G)
    def _():
        for rs, cs, z in z_tiles():
            store(rs, cs, z * _sigmoid(z) * gain_ref[:, cs])

    @pl.when(seg == SEG_U)
    def _():
        for rs, cs, z in z_tiles():
            store(rs, cs, _gelu_tanh(z))

    @pl.when(seg == SEG_VS)
    def _():
        for rs, _ in _tiles(rows, 1, LN_ROW_TILE, 1):
            for _, cs in _tiles(1, d, 1, V7X_MXU_COLS):
                a_scr[rs, cs] = _gelu_tanh(_dot(_unpack(n_ref[_packed(rs), :]), _unpack(w_ref[:, cs])))
            a = a_scr[rs, :]
            cen = a - jnp.mean(a, axis=-1, keepdims=True)
            var = jnp.mean(cen * cen, axis=-1, keepdims=True)
            o_ref[0, _packed(rs), :] = _pack(cen * lax.rsqrt(var + EPS) * lnw_ref[...])

    @pl.when(seg >= SEG_GATE_R)
    def _():
        for rs, cs, z in z_tiles():
            store(rs, cs, _sigmoid(z + gb_ref[pl.ds(seg - SEG_GATE_R, 1), cs]))


def _in_proj(n, w_in, cos, sin, gn_gain, ln_w, gate_bias):
    t, d = n.shape[0] * PACK, n.shape[1]
    n_seg = w_in.shape[1] // d
    seq, half = cos.shape
    pos_blocks = seq // IN_ROWS
    assert t % IN_ROWS == 0 and seq % IN_ROWS == 0 and n_seg == SEG_GATE_S + 1
    assert d // RET_HEADS == V7X_MXU_COLS == RET_BLOCK and MATMUL_ROW_TILE % RET_BLOCK == 0
    vec = lambda k: pl.BlockSpec((k, d), lambda s, i: (0, 0))
    pos = lambda: pl.BlockSpec((IN_ROWS, half), lambda s, i: (i % pos_blocks, 0))
    vmem = (4 * _nbytes((IN_ROWS, d), BF16) + 2 * _nbytes((d, d), BF16) + 4 * _nbytes((IN_ROWS, half), F32)
            + 2 * _nbytes((IN_ROWS, d), F32) + 16 * _nbytes((MATMUL_ROW_TILE, V7X_MXU_COLS), F32))
    return pl.pallas_call(
        functools.partial(_in_proj_kernel, k_scale=float(2 * half) ** -0.5),
        grid=(n_seg, t // IN_ROWS),
        in_specs=[pl.BlockSpec((IN_ROWS // PACK, d), lambda s, i: (i, 0)),
                  pl.BlockSpec((d // PACK, d), lambda s, i: (0, s)),
                  pos(), pos(), vec(1), vec(1), vec(2)],
        out_specs=pl.BlockSpec((1, IN_ROWS // PACK, d), lambda s, i: (s, i, 0)),
        out_shape=jax.ShapeDtypeStruct((n_seg, t // PACK, d), U32),
        scratch_shapes=[pltpu.VMEM((IN_ROWS, d), F32)],
        compiler_params=_params(("arbitrary", "arbitrary"), vmem),
        name="in_proj",
    )(n, w_in, cos, sin, gn_gain, ln_w, gate_bias)


def _decay_rate(dec_ref, h, shape):
    return -jnp.exp(jnp.full(shape, dec_ref[h], F32))


def _ret_bwd_state_kernel(decb_ref, kt_ref, v_ref, sb_ref, state, kdec, cdec):
    c = pl.program_id(1)
    blk = RET_BLOCK
    head = v_ref.shape[2] // RET_HEADS

    @pl.when((pl.program_id(0) == 0) & (c == 0))
    def _():
        col = lax.broadcasted_iota(jnp.int32, (1, blk), 1).astype(F32)
        for h in range(RET_HEADS):
            kdec[h] = jnp.exp(_decay_rate(decb_ref, h, (1, blk)) * col)
            cdec[h] = jnp.exp(_decay_rate(decb_ref, h, (1, head)) * float(blk))

    @pl.when(c == 0)
    def _():
        state[...] = jnp.zeros_like(state)

    for j in reversed(range(sb_ref.shape[1])):
        rows = slice(j * blk // PACK, (j + 1) * blk // PACK)
        for h in range(RET_HEADS):
            sl = slice(h * head, (h + 1) * head)
            st = state[h]
            sb_ref[0, j, h] = _pack(st)
            kd = (_unpack(kt_ref[0, rows, sl]).astype(F32) * kdec[h]).astype(BF16)
            state[h] = st * cdec[h] + _dot(kd, _unpack(v_ref[0, rows, sl]))


def _ret_bwd_states(z, dec_b, batch):
    _, t_packed, d = z.shape
    n_blk = t_packed * PACK // batch // RET_BLOCK
    head = d // RET_HEADS
    per_step = RET_BWD_CHUNKS_PER_STEP
    n_steps = n_blk // per_step
    assert n_blk % per_step == 0
    blk_bytes = per_step * _nbytes((RET_BLOCK, d), BF16)
    state_bytes = _nbytes((RET_HEADS, head, head), F32)
    return pl.pallas_call(
        _ret_bwd_state_kernel,
        grid=(batch, n_steps),
        in_specs=[
            pl.BlockSpec(memory_space=pltpu.SMEM),
            pl.BlockSpec((1, per_step * RET_BLOCK // PACK, d), lambda b, c: (SEG_K, b * n_steps + (n_steps - 1 - c), 0)),
            pl.BlockSpec((1, per_step * RET_BLOCK // PACK, d), lambda b, c: (SEG_V, b * n_steps + (n_steps - 1 - c), 0)),
        ],
        out_specs=pl.BlockSpec((1, per_step, RET_HEADS, head // PACK, head), lambda b, c: (b, n_steps - 1 - c, 0, 0, 0)),
        out_shape=jax.ShapeDtypeStruct((batch, n_blk, RET_HEADS, head // PACK, head), U32),
        scratch_shapes=[
            pltpu.VMEM((RET_HEADS, head, head), F32),
            pltpu.VMEM((RET_HEADS, 1, RET_BLOCK), F32),
            pltpu.VMEM((RET_HEADS, 1, head), F32),
        ],
        compiler_params=_params(("arbitrary", "arbitrary"), 6 * blk_bytes + 4 * state_bytes),
        name="ret_bwd_states",
    )(dec_b, z, z)


def _project_slice(act_ref, w_ref, acc_ref, i):
    width = act_ref.shape[2]
    w_rows = slice(i * width // PACK, (i + 1) * width // PACK)
    for _, cs in _tiles(1, acc_ref.shape[1], 1, RESIDUAL_COL_TILE):
        part = _dot(act_ref[i], _unpack(w_ref[w_rows, cs]))
        acc_ref[:, cs] = part if i == 0 else acc_ref[:, cs] + part


def _ret_out_kernel(decf_ref, decb_ref, q_ref, kt_ref, v_ref, g_ref, sb_ref, wr_ref, o_ref,
                    state, dmat, qdf, qdb, kdf, cdf, act, acc):
    c = pl.program_id(1)
    blk = RET_BLOCK
    head = q_ref.shape[2] // RET_HEADS

    @pl.when((pl.program_id(0) == 0) & (c == 0))
    def _():
        row = lax.broadcasted_iota(jnp.int32, (blk, head), 0).astype(F32)
        col = lax.broadcasted_iota(jnp.int32, (1, blk), 1).astype(F32)
        ri = lax.broadcasted_iota(jnp.int32, (blk, blk), 0)
        ci = lax.broadcasted_iota(jnp.int32, (blk, blk), 1)
        dist = (ri - ci).astype(F32)
        for h in range(RET_HEADS):
            qdf[h] = jnp.exp(_decay_rate(decf_ref, h, (blk, head)) * (row + 1.0))
            qdb[h] = jnp.exp(_decay_rate(decb_ref, h, (blk, head)) * (float(blk) - row))
            kdf[h] = jnp.exp(_decay_rate(decf_ref, h, (1, blk)) * (float(blk) - 1.0 - col))
            cdf[h] = jnp.exp(_decay_rate(decf_ref, h, (1, head)) * float(blk))
            lf2 = _decay_rate(decf_ref, h, (blk, blk))
            lb2 = _decay_rate(decb_ref, h, (blk, blk))
            dmat[h] = jnp.where(ri >= ci, jnp.exp(lf2 * jnp.maximum(dist, 0.0)), jnp.exp(lb2 * jnp.maximum(-dist, 0.0)))

    @pl.when(c == 0)
    def _():
        state[...] = jnp.zeros_like(state)

    n_sub = sb_ref.shape[1]

    def project(item):
        j, h = divmod(item, RET_HEADS)
        _project_slice(act.at[j], wr_ref, acc.at[j], h)
        if h == RET_HEADS - 1:
            o_ref[j * blk // PACK:(j + 1) * blk // PACK, :] = _pack(acc[j])

    for item in range(n_sub * RET_HEADS):
        j, h = divmod(item, RET_HEADS)
        rows = slice(j * blk // PACK, (j + 1) * blk // PACK)
        sl = slice(h * head, (h + 1) * head)
        q = _unpack(q_ref[0, rows, sl])
        kt = _unpack(kt_ref[0, rows, sl])
        v = _unpack(v_ref[0, rows, sl])
        st = state[h]
        scores = _dot(q, kt)
        cross = _dot(q, st.astype(BF16)) * qdf[h] + _dot(q, _unpack(sb_ref[0, j, h])) * qdb[h]
        state[h] = st * cdf[h] + _dot((kt.astype(F32) * kdf[h]).astype(BF16), v)
        o = _dot((scores * dmat[h]).astype(BF16), v) + cross
        cen = o - jnp.mean(o, axis=-1, keepdims=True)
        var = jnp.mean(cen * cen, axis=-1, keepdims=True)
        act[j, h] = (_unpack(g_ref[0, rows, sl]).astype(F32) * (cen * lax.rsqrt(var + EPS))).astype(BF16)
        if item >= RET_PROJECT_LAG:
            project(item - RET_PROJECT_LAG)
    for item in range(n_sub * RET_HEADS - RET_PROJECT_LAG, n_sub * RET_HEADS):
        project(item)


def _ret_out(z, sb, dec_f, dec_b, w_ret_out, batch):
    _, t_packed, d = z.shape
    n_blk = t_packed * PACK // batch // RET_BLOCK
    head = d // RET_HEADS
    per_step = RET_OUT_CHUNKS_PER_STEP
    n_steps = n_blk // per_step
    assert n_blk % per_step == 0
    rows = lambda seg: pl.BlockSpec((1, per_step * RET_BLOCK // PACK, d), lambda b, c: (seg, b * n_steps + c, 0))
    blk_bytes = per_step * _nbytes((RET_BLOCK, d), BF16)
    table_bytes = _nbytes((RET_HEADS, RET_BLOCK, head), F32)
    return pl.pallas_call(
        _ret_out_kernel,
        grid=(batch, n_steps),
        in_specs=[
            pl.BlockSpec(memory_space=pltpu.SMEM),
            pl.BlockSpec(memory_space=pltpu.SMEM),
            rows(SEG_Q), rows(SEG_K), rows(SEG_V), rows(SEG_G),
            pl.BlockSpec((1, per_step, RET_HEADS, head // PACK, head), lambda b, c: (b, c, 0, 0, 0)),
            _resident((d // PACK, d)),
        ],
        out_specs=pl.BlockSpec((per_step * RET_BLOCK // PACK, d), lambda b, c: (b * n_steps + c, 0)),
        out_shape=jax.ShapeDtypeStruct((t_packed, d), U32),
        scratch_shapes=[
            pltpu.VMEM((RET_HEADS, head, head), F32),
            pltpu.VMEM((RET_HEADS, RET_BLOCK, RET_BLOCK), F32),
            pltpu.VMEM((RET_HEADS, RET_BLOCK, head), F32),
            pltpu.VMEM((RET_HEADS, RET_BLOCK, head), F32),
            pltpu.VMEM((RET_HEADS, 1, RET_BLOCK), F32),
            pltpu.VMEM((RET_HEADS, 1, head), F32),
            pltpu.VMEM((per_step, RET_HEADS, RET_BLOCK, head), BF16),
            pltpu.VMEM((per_step, RET_BLOCK, d), F32),
        ],
        compiler_params=_params(("arbitrary", "arbitrary"),
                                12 * blk_bytes + 6 * table_bytes + _nbytes((d, d), BF16)
                                + (per_step + 3) * _nbytes((RET_BLOCK, d), F32)),
        name="ret_out",
    )(dec_f, dec_b, z, z, z, z, sb, w_ret_out)


def _mix_out_kernel(u_ref, vs_ref, ws_ref, bs_ref, ret_ref, gr_ref, gs_ref, h_ref, wsg_ref, wo_ref, o_ref, act, acc):
    rows, d = h_ref.shape
    gdim = d // SGU_GROUPS
    for g in range(SGU_GROUPS):
        cs = slice(g * gdim, (g + 1) * gdim)
        for rs, _ in _tiles(rows, 1, SGU_CHUNK, 1):
            mixed = _dot(ws_ref[g], _unpack(vs_ref[0, _packed(rs), cs])) + bs_ref[g]
            act[g, rs, :] = (_unpack(u_ref[0, _packed(rs), cs]).astype(F32) * mixed).astype(BF16)
        if g >= SGU_PROJECT_LAG:
            _project_slice(act, wsg_ref, acc, g - SGU_PROJECT_LAG)
    for g in range(SGU_GROUPS - SGU_PROJECT_LAG, SGU_GROUPS):
        _project_slice(act, wsg_ref, acc, g)
    for k in range(SGU_GROUPS):
        ks = slice(k * gdim, (k + 1) * gdim)
        merged = (_unpack(gr_ref[0, :, ks]).astype(F32) * _unpack(ret_ref[:, ks]).astype(F32)
                  + _unpack(gs_ref[0, :, ks]).astype(F32) * acc[:, ks]).astype(BF16)
        w_rows = slice(k * gdim // PACK, (k + 1) * gdim // PACK)
        for _, cs in _tiles(1, d, 1, RESIDUAL_COL_TILE):
            o_ref[:, cs] = (h_ref[:, cs] if k == 0 else o_ref[:, cs]) + _dot(merged, _unpack(wo_ref[w_rows, cs]))


def _mix_out(z, w_s, b_s, ret_proj, h, w_sgu_out, w_out):
    t, d = h.shape
    gdim = d // SGU_GROUPS
    assert t % OUT_ROWS == 0 and OUT_ROWS % SGU_CHUNK == 0
    rows = lambda: pl.BlockSpec((OUT_ROWS, d), lambda i: (i, 0))
    packed_rows = lambda: pl.BlockSpec((OUT_ROWS // PACK, d), lambda i: (i, 0))
    seg_rows = lambda s: pl.BlockSpec((1, OUT_ROWS // PACK, d), lambda i: (s, i, 0))
    weight = lambda: _resident((d // PACK, d))
    vmem = (2 * _nbytes((d, d), BF16) + 10 * _nbytes((OUT_ROWS, d), BF16) + 4 * _nbytes((OUT_ROWS, d), F32)
            + _nbytes((OUT_ROWS, d), BF16) + 4 * _nbytes((OUT_ROWS, d), F32))
    return pl.pallas_call(
        _mix_out_kernel,
        grid=(t // OUT_ROWS,),
        in_specs=[
            seg_rows(SEG_U), seg_rows(SEG_VS),
            pl.BlockSpec((SGU_GROUPS, SGU_CHUNK, SGU_CHUNK), lambda i: (0, 0, 0)),
            pl.BlockSpec((SGU_GROUPS, SGU_CHUNK, 1), lambda i: (0, 0, 0)),
            packed_rows(), seg_rows(SEG_GATE_R), seg_rows(SEG_GATE_S), rows(), weight(), weight(),
        ],
        out_specs=rows(),
        out_shape=jax.ShapeDtypeStruct((t, d), F32),
        scratch_shapes=[pltpu.VMEM((SGU_GROUPS, OUT_ROWS, gdim), BF16), pltpu.VMEM((OUT_ROWS, d), F32)],
        compiler_params=_params(("parallel",), vmem),
        name="mix_out",
    )(z, z, w_s, b_s, ret_proj, z, z, h, w_sgu_out, w_out)


def _mem_kv_kernel(m_ref, nw_ref, w_ref, o_ref, n_scr):
    @pl.when(pl.program_id(1) == 0)
    def _():
        n_scr[...] = _rms(m_ref[...], nw_ref[...]).astype(BF16)

    o_ref[...] = _pack(_dot(n_scr[...], _unpack(w_ref[...])))


def _mem_kv(mem, norm_w, w_kv):
    t, d = mem.shape
    rows = min(KV_ROWS, t)
    n_out = w_kv.shape[1]
    assert t % rows == 0 and n_out % KV_COLS == 0
    vmem = (2 * _nbytes((rows, d), F32) + _nbytes((rows, d), BF16) + 2 * _nbytes((d, KV_COLS), BF16)
            + 2 * _nbytes((rows, KV_COLS), BF16) + 2 * _nbytes((rows, d), F32))
    return pl.pallas_call(
        _mem_kv_kernel,
        grid=(t // rows, n_out // KV_COLS),
        in_specs=[
            pl.BlockSpec((rows, d), lambda i, j: (i, 0)),
            pl.BlockSpec((1, d), lambda i, j: (0, 0)),
            pl.BlockSpec((d // PACK, KV_COLS), lambda i, j: (0, j)),
        ],
        out_specs=pl.BlockSpec((rows // PACK, KV_COLS), lambda i, j: (i, j)),
        out_shape=jax.ShapeDtypeStruct((t // PACK, n_out), U32),
        scratch_shapes=[pltpu.VMEM((rows, d), BF16)],
        compiler_params=_params(("parallel", "arbitrary"), vmem),
        name="mem_kv",
    )(mem, norm_w, w_kv)


def _xattn_kernel(h_ref, nw_ref, kv_ref, wq_ref, wo_ref, next_nw_ref, o_ref, n_out_ref, n_scr, q_scr, a_scr, acc):
    rows, d = h_ref.shape
    hd = d // XATTN_HEADS
    scale = float(hd) ** -0.5
    head_cols = lambda h: slice(h * hd, (h + 1) * hd)

    def project_q(h, rs=slice(None)):
        q_scr[rs, head_cols(h)] = _dot(n_scr[rs, :], _unpack(wq_ref[:, head_cols(h)])).astype(BF16)

    def probs(h):
        s = _dot_nt(q_scr[:, head_cols(h)], _unpack(kv_ref[:, head_cols(h)])) * scale
        e = jnp.exp(s - jnp.max(s, axis=-1, keepdims=True))
        return (e / jnp.sum(e, axis=-1, keepdims=True)).astype(BF16)

    def emit(rs, normed):
        n_out_ref[_packed(rs), :] = _pack(normed)

    for rs, _ in _tiles(rows, 1, rows // 2, 1):
        n_scr[rs, :] = _rms(h_ref[rs, :], nw_ref[...]).astype(BF16)
        project_q(0, rs)
    last = XATTN_HEADS - 1
    p = probs(0)
    for h in range(XATTN_HEADS):
        if h < last:
            project_q(h + 1)
        p_next = probs(h + 1) if h < last else None
        a_scr[h] = _dot(p, _unpack(kv_ref[:, d + h * hd:d + (h + 1) * hd])).astype(BF16)
        if h < last:
            _project_slice(a_scr, wo_ref, acc, h)
        else:
            _residual_proj_norm(h_ref, lambda rs: a_scr[last, rs, :], wo_ref, o_ref, None, next_nw_ref[...], emit,
                                w_rows=slice(last * hd // PACK, (last + 1) * hd // PACK), base_ref=acc)
        p = p_next


def _xattn(h, kv, norm_w, w_q, w_o, next_norm_w, batch):
    t, d = h.shape
    n_mem = kv.shape[0] * PACK // batch
    row_blocks = t // batch // XATTN_ROWS
    assert t % (batch * XATTN_ROWS) == 0
    rows = lambda: pl.BlockSpec((XATTN_ROWS, d), lambda b, i: (b * row_blocks + i, 0))
    vec = lambda: pl.BlockSpec((1, d), lambda b, i: (0, 0))
    weight = lambda: _resident((d // PACK, d))
    vmem = (2 * _nbytes((d, d), BF16) + 5 * _nbytes((XATTN_ROWS, d), F32) + 5 * _nbytes((XATTN_ROWS, d), BF16)
            + 2 * _nbytes((n_mem, 2 * d), BF16) + 2 * _nbytes((XATTN_ROWS, d), F32))
    return pl.pallas_call(
        _xattn_kernel,
        grid=(batch, row_blocks),
        in_specs=[rows(), vec(), pl.BlockSpec((n_mem // PACK, 2 * d), lambda b, i: (b, 0)), weight(), weight(), vec()],
        out_specs=(rows(), pl.BlockSpec((XATTN_ROWS // PACK, d), lambda b, i: (b * row_blocks + i, 0))),
        out_shape=(jax.ShapeDtypeStruct((t, d), F32), jax.ShapeDtypeStruct((t // PACK, d), U32)),
        scratch_shapes=[pltpu.VMEM((XATTN_ROWS, d), BF16), pltpu.VMEM((XATTN_ROWS, d), BF16),
                        pltpu.VMEM((XATTN_HEADS, XATTN_ROWS, d // XATTN_HEADS), BF16),
                        pltpu.VMEM((XATTN_ROWS, d), F32)],
        compiler_params=_params(("parallel", "parallel"), vmem),
        name="xattn",
    )(h, norm_w, kv, w_q, w_o, next_norm_w)


def _rope_tables(seq, half):
    freqs = ROPE_BASE ** (-jnp.linspace(0.0, 1.0, half, dtype=F32))
    ang = jnp.arange(seq, dtype=F32)[:, None] * freqs[None, :]
    return jnp.cos(ang), jnp.sin(ang)


def _trunk(x, mem, w):
    batch, seq, d = x.shape
    t = batch * seq
    x = x.reshape(t, d)
    half = d // RET_HEADS // 2
    cos, sin = _rope_tables(seq, half)

    hidden = _ffn_up(x, w["ffn1_norm"], w["ffn1_w_gu"], t)
    h, n = _ffn_down(x, hidden, w["ffn1_w_down"], w["mix_norm"], final_norm=False)

    z = _in_proj(n, w["w_in"], cos, sin, w["ret_gn_w"], w["sgu_norm_w"], w["gate_bias"])
    sb = _ret_bwd_states(z, w["ret_decay_bwd"], batch)
    ret_proj = _ret_out(z, sb, w["ret_decay_fwd"], w["ret_decay_bwd"], w["w_ret_out"], batch)
    h = _mix_out(z, w["sgu_w_s"], w["sgu_b_s"], ret_proj, h, w["w_sgu_out"], w["w_out"])

    kv = _mem_kv(mem.reshape(-1, d), w["xattn_norm_mem"], w["xattn_w_kv"])
    h, n = _xattn(h, kv, w["xattn_norm_q"], w["xattn_w_q"], w["xattn_w_o"], w["ffn2_norm"], batch)

    hidden = _ffn_up(n, None, w["ffn2_w_gu"], t)
    y = _ffn_down(h, hidden, w["ffn2_w_down"], w["final_norm"], final_norm=True)
    return y.reshape(batch, seq, d)


def _prepare(ffn1_norm, ffn1_w_gu, ffn1_w_down, mix_norm, w_in, gate_bias, ret_decay_fwd, ret_decay_bwd, ret_gn_w, w_ret_out, sgu_norm_w, sgu_w_s, sgu_b_s, w_sgu_out, w_out, xattn_norm_q, xattn_norm_mem, xattn_w_q, xattn_w_kv, xattn_w_o, ffn2_norm, ffn2_w_gu, ffn2_w_down, final_norm):
    assert ffn1_norm.shape[0] == 1, "single-layer trunk"
    pw = lambda w: _pack_weight(w[0])
    return {
        "ffn1_norm": ffn1_norm, "ffn1_w_gu": _pack_gate_up(ffn1_w_gu[0]), "ffn1_w_down": pw(ffn1_w_down),
        "mix_norm": mix_norm, "w_in": pw(w_in), "gate_bias": gate_bias[0],
        "ret_decay_fwd": ret_decay_fwd[0], "ret_decay_bwd": ret_decay_bwd[0], "ret_gn_w": ret_gn_w,
        "w_ret_out": pw(w_ret_out), "sgu_norm_w": sgu_norm_w,
        "sgu_w_s": sgu_w_s[0].astype(BF16), "sgu_b_s": sgu_b_s[0][:, :, None],
        "w_sgu_out": pw(w_sgu_out), "w_out": pw(w_out),
        "xattn_norm_q": xattn_norm_q, "xattn_norm_mem": xattn_norm_mem,
        "xattn_w_q": pw(xattn_w_q), "xattn_w_kv": pw(xattn_w_kv), "xattn_w_o": pw(xattn_w_o),
        "ffn2_norm": ffn2_norm, "ffn2_w_gu": _pack_gate_up(ffn2_w_gu[0]), "ffn2_w_down": pw(ffn2_w_down),
        "final_norm": final_norm[None, :],
    }


def kernel(x_prompt, x_sample, mem_prompt, mem_sample, ffn1_norm, ffn1_w_gu, ffn1_w_down, mix_norm, w_in, gate_bias, ret_decay_fwd, ret_decay_bwd, ret_gn_w, w_ret_out, sgu_norm_w, sgu_w_s, sgu_b_s, w_sgu_out, w_out, xattn_norm_q, xattn_norm_mem, xattn_w_q, xattn_w_kv, xattn_w_o, ffn2_norm, ffn2_w_gu, ffn2_w_down, final_norm):
    w = _prepare(ffn1_norm, ffn1_w_gu, ffn1_w_down, mix_norm, w_in, gate_bias, ret_decay_fwd, ret_decay_bwd, ret_gn_w, w_ret_out, sgu_norm_w, sgu_w_s, sgu_b_s, w_sgu_out, w_out, xattn_norm_q, xattn_norm_mem, xattn_w_q, xattn_w_kv, xattn_w_o, ffn2_norm, ffn2_w_gu, ffn2_w_down, final_norm)
    return _trunk(x_prompt, mem_prompt, w), _trunk(x_sample, mem_sample, w)
```

```python
import functools
import math

import jax
import jax.numpy as jnp
from jax import lax
from jax.experimental import pallas as pl
from jax.experimental.pallas import tpu as pltpu

F32 = jnp.float32
BF16 = jnp.bfloat16
U32 = jnp.uint32
PACK = 2

EPS = 1e-6
FFN_RES_SCALE = 0.5
ROPE_BASE = 10000.0
RET_HEADS = 8
SGU_GROUPS = 8
SGU_CHUNK = 128
XATTN_HEADS = 4
SEG_Q, SEG_K, SEG_V, SEG_G, SEG_U, SEG_VS, SEG_GATE_R, SEG_GATE_S = range(8)

V7X_VMEM_BYTES = 64 * 1024 * 1024
V7X_SUBLANES = 8
V7X_LANES = 128
V7X_MXU_COLS = 256

RET_BLOCK = 256
RET_BWD_CHUNKS_PER_STEP = 4
RET_OUT_CHUNKS_PER_STEP = 2
RET_PROJECT_LAG = 2
SGU_PROJECT_LAG = 1

PACK_ROWS = 512
PACK_MAX_COLS = 2048
PACK_VMEM_RESERVATION = V7X_VMEM_BYTES - 8 * 1024 * 1024
FFN_UP_ROWS = 1024
FFN_UP_COLS = 1408
FFN_NORM_ROW_TILE = 256
GATE_UP_LANES = V7X_LANES
FFN_DOWN_ROWS = 512
IN_ROWS = 2048
IN_LN_ROWS = 1024
LN_ROW_TILE = 256
OUT_ROWS = 256
KV_ROWS = 512
KV_COLS = 1024
XATTN_ROWS = 512
RESIDUAL_COL_TILE = 512
MATMUL_ROW_TILE = 512


def _nbytes(shape, dtype):
    return math.prod(shape) * jnp.dtype(dtype).itemsize


def _params(semantics, vmem_bytes):
    assert vmem_bytes <= V7X_VMEM_BYTES, vmem_bytes
    return pltpu.CompilerParams(dimension_semantics=semantics, vmem_limit_bytes=int(vmem_bytes))


def _resident(shape):
    return pl.BlockSpec(shape, lambda *_: (0,) * len(shape), pipeline_mode=pl.Buffered(1))


def _unpack(words):
    return pltpu.bitcast(words, BF16)


def _pack(vals):
    return pltpu.bitcast(vals.astype(BF16), U32)


def _packed(rows):
    return slice(rows.start // PACK, rows.stop // PACK)


def _tiles(n_rows, n_cols, row_tile, col_tile):
    for r in range(n_rows // row_tile):
        for c in range(n_cols // col_tile):
            yield slice(r * row_tile, (r + 1) * row_tile), slice(c * col_tile, (c + 1) * col_tile)


def _rms(x, w):
    return x * lax.rsqrt(jnp.mean(x * x, axis=-1, keepdims=True) + EPS) * w


def _dot(a, b):
    return jnp.dot(a, b, preferred_element_type=F32)


def _dot_nt(a, b):
    return lax.dot_general(a, b, (((1,), (1,)), ((), ())), preferred_element_type=F32)


def _sigmoid(x):
    return 0.5 * jnp.tanh(0.5 * x) + 0.5


def _gelu_tanh(x):
    c = math.sqrt(2.0 / math.pi)
    return x * (0.5 * jnp.tanh(x * (c + (c * 0.044715) * (x * x))) + 0.5)


def _residual_proj(x_ref, a, w_ref, o_ref):
    for _, cs in _tiles(1, o_ref.shape[1], 1, RESIDUAL_COL_TILE):
        o_ref[:, cs] = x_ref[:, cs] + _dot(a, _unpack(w_ref[:, cs]))


def _residual_proj_norm(x_ref, lhs, w_ref, o_ref, scale, norm_w, emit, w_rows=slice(None), base_ref=None):
    rows, d = o_ref.shape
    for rs, _ in _tiles(rows, 1, rows // 2, 1):
        a = lhs(rs)
        ss = jnp.zeros((a.shape[0], 1), F32)
        for _, cs in _tiles(1, d, 1, RESIDUAL_COL_TILE):
            y = _dot(a, _unpack(w_ref[w_rows, cs]))
            if base_ref is not None:
                y = base_ref[rs, cs] + y
            o = x_ref[rs, cs] + (y if scale is None else scale * y)
            o_ref[rs, cs] = o
            ss += jnp.sum(o * o, axis=-1, keepdims=True)
        emit(rs, o_ref[rs, :] * lax.rsqrt(ss * (1.0 / d) + EPS) * norm_w)


def _pack_kernel(w_ref, o_ref):
    o_ref[...] = _pack(w_ref[...])


def _pack_weight(w):
    k, n = w.shape
    rows = min(PACK_ROWS, k)
    cols = max(c for c in range(V7X_LANES, PACK_MAX_COLS + 1, V7X_LANES) if n % c == 0)
    assert k % rows == 0 and rows % (V7X_SUBLANES * PACK) == 0
    return pl.pallas_call(
        _pack_kernel,
        grid=(k // rows, n // cols),
        in_specs=[pl.BlockSpec((rows, cols), lambda i, j: (i, j))],
        out_specs=pl.BlockSpec((rows // PACK, cols), lambda i, j: (i, j)),
        out_shape=jax.ShapeDtypeStruct((k // PACK, n), U32),
        compiler_params=_params(("parallel", "parallel"), PACK_VMEM_RESERVATION),
        name="pack_weight",
    )(w)


def _pack_gate_up_kernel(g_ref, u_ref, o_ref):
    for _, cs in _tiles(1, g_ref.shape[1], 1, GATE_UP_LANES):
        o_ref[:, 2 * cs.start:2 * cs.start + GATE_UP_LANES] = _pack(g_ref[:, cs])
        o_ref[:, 2 * cs.start + GATE_UP_LANES:2 * cs.stop] = _pack(u_ref[:, cs])


def _pack_gate_up(w_gu):
    k, n = w_gu.shape
    rows = min(PACK_ROWS, k)
    n_col_steps = n // 2 // FFN_UP_COLS
    assert n % (2 * FFN_UP_COLS) == 0 and k % rows == 0
    return pl.pallas_call(
        _pack_gate_up_kernel,
        grid=(k // rows, n_col_steps),
        in_specs=[pl.BlockSpec((rows, FFN_UP_COLS), lambda i, j: (i, j)),
                  pl.BlockSpec((rows, FFN_UP_COLS), lambda i, j: (i, j + n_col_steps))],
        out_specs=pl.BlockSpec((rows // PACK, 2 * FFN_UP_COLS), lambda i, j: (i, j)),
        out_shape=jax.ShapeDtypeStruct((k // PACK, n), U32),
        compiler_params=_params(("parallel", "parallel"), PACK_VMEM_RESERVATION),
        name="pack_gate_up",
    )(w_gu, w_gu)


def _ffn_up_kernel(*refs, norm_input):
    rows = refs[-2 if norm_input else -1].shape[0] * PACK

    def gate_up(lhs, o_ref, w_ref, row_tile, before_row_tile=None):
        for rs, cs in _tiles(rows, w_ref.shape[1], row_tile, 2 * GATE_UP_LANES):
            if before_row_tile is not None and cs.start == 0:
                before_row_tile(rs)
            z = _dot(lhs(rs), _unpack(w_ref[:, cs]))
            g, u = z[:, :GATE_UP_LANES], z[:, GATE_UP_LANES:]
            o_ref[_packed(rs), cs.start // 2:cs.stop // 2] = _pack(g * _sigmoid(g) * u)

    if norm_input:
        x_ref, nw_ref, w_ref, o_ref, n_scr = refs
        lhs = lambda rs: n_scr[rs, :]

        def normalise(rs):
            n_scr[rs, :] = _rms(x_ref[rs, :], nw_ref[...]).astype(BF16)

        @pl.when(pl.program_id(1) == 0)
        def _():
            gate_up(lhs, o_ref, w_ref, FFN_NORM_ROW_TILE, normalise)

        @pl.when(pl.program_id(1) != 0)
        def _():
            gate_up(lhs, o_ref, w_ref, MATMUL_ROW_TILE)
    else:
        n_ref, w_ref, o_ref = refs
        gate_up(lambda rs: _unpack(n_ref[_packed(rs), :]), o_ref, w_ref, MATMUL_ROW_TILE)


def _ffn_up(x_or_n, norm_w, w_gu, t):
    d = x_or_n.shape[1]
    d_ff = w_gu.shape[1] // 2
    assert t % FFN_UP_ROWS == 0 and d_ff % FFN_UP_COLS == 0 and FFN_UP_COLS % GATE_UP_LANES == 0
    norm_input = norm_w is not None
    w_spec = pl.BlockSpec((d // PACK, 2 * FFN_UP_COLS), lambda i, j: (0, j))
    if norm_input:
        in_specs = [pl.BlockSpec((FFN_UP_ROWS, d), lambda i, j: (i, 0)), pl.BlockSpec((1, d), lambda i, j: (0, 0)), w_spec]
        args = (x_or_n, norm_w, w_gu)
        scratch = [pltpu.VMEM((FFN_UP_ROWS, d), BF16)]
    else:
        in_specs = [pl.BlockSpec((FFN_UP_ROWS // PACK, d), lambda i, j: (i, 0)), w_spec]
        args = (x_or_n, w_gu)
        scratch = []
    vmem = (2 * _nbytes((FFN_UP_ROWS, d), F32 if norm_input else BF16) + _nbytes((FFN_UP_ROWS, d), BF16)
            + 4 * _nbytes((d, FFN_UP_COLS), BF16) + 2 * _nbytes((FFN_UP_ROWS, FFN_UP_COLS), BF16)
            + 8 * _nbytes((MATMUL_ROW_TILE, 2 * GATE_UP_LANES), F32))
    return pl.pallas_call(
        functools.partial(_ffn_up_kernel, norm_input=norm_input),
        grid=(t // FFN_UP_ROWS, d_ff // FFN_UP_COLS),
        in_specs=in_specs,
        out_specs=pl.BlockSpec((FFN_UP_ROWS // PACK, FFN_UP_COLS), lambda i, j: (i, j)),
        out_shape=jax.ShapeDtypeStruct((t // PACK, d_ff), U32),
        scratch_shapes=scratch,
        compiler_params=_params(("parallel", "arbitrary"), vmem),
        name="ffn_up_norm" if norm_input else "ffn_up",
    )(*args)


def _ffn_down_kernel(x_ref, hid_ref, wd_ref, nw_ref, o_ref, *n_out, final_norm):
    def emit(rs, normed):
        if final_norm:
            o_ref[rs, :] = normed
        else:
            n_out[0][_packed(rs), :] = _pack(normed)

    _residual_proj_norm(x_ref, lambda rs: _unpack(hid_ref[_packed(rs), :]), wd_ref, o_ref, FFN_RES_SCALE,
                        nw_ref[...], emit)


def _ffn_down(x, hidden, w_down, norm_w, final_norm):
    t, d = x.shape
    d_ff = hidden.shape[1]
    assert t % FFN_DOWN_ROWS == 0
    row_spec = lambda: pl.BlockSpec((FFN_DOWN_ROWS, d), lambda i: (i, 0))
    packed_spec = lambda cols: pl.BlockSpec((FFN_DOWN_ROWS // PACK, cols), lambda i: (i, 0))
    vmem = (4 * _nbytes((FFN_DOWN_ROWS, d), F32) + 2 * _nbytes((FFN_DOWN_ROWS, d_ff), BF16)
            + _nbytes((d_ff, d), BF16) + 2 * _nbytes((FFN_DOWN_ROWS, d), BF16)
            + 4 * _nbytes((FFN_DOWN_ROWS, RESIDUAL_COL_TILE), F32))
    out_f32 = jax.ShapeDtypeStruct((t, d), F32)
    return pl.pallas_call(
        functools.partial(_ffn_down_kernel, final_norm=final_norm),
        grid=(t // FFN_DOWN_ROWS,),
        in_specs=[row_spec(), packed_spec(d_ff), _resident((d_ff // PACK, d)), pl.BlockSpec((1, d), lambda i: (0, 0))],
        out_specs=row_spec() if final_norm else (row_spec(), packed_spec(d)),
        out_shape=out_f32 if final_norm else (out_f32, jax.ShapeDtypeStruct((t // PACK, d), U32)),
        compiler_params=_params(("parallel",), vmem),
        name="ffn_down_final" if final_norm else "ffn_down",
    )(x, hidden, w_down, norm_w)


def _in_proj_kernel(n_ref, w_ref, *refs, kind, k_scale):
    seg = pl.program_id(0)
    rows, d = n_ref.shape[0] * PACK, n_ref.shape[1]
    o_ref = refs[1] if kind == "gelu_layernorm" else refs[-1]

    def z_tiles():
        for rs, cs in _tiles(rows, d, MATMUL_ROW_TILE, V7X_MXU_COLS):
            yield rs, cs, _dot(_unpack(n_ref[_packed(rs), :]), _unpack(w_ref[:, cs]))

    def store(rs, cs, val):
        o_ref[0, _packed(rs), cs] = _pack(val)

    if kind in ("rotary", "rotary_transposed"):
        cos_ref, sin_ref, _ = refs
        half = V7X_MXU_COLS // 2
        for rs, cs, z in z_tiles():
            x1, x2 = z[:, :half], z[:, half:]
            cos, sin = cos_ref[rs, :], sin_ref[rs, :]
            if kind == "rotary":
                store(rs, slice(cs.start, cs.start + half), x1 * cos - x2 * sin)
                store(rs, slice(cs.start + half, cs.stop), x1 * sin + x2 * cos)
            else:
                rot = jnp.concatenate([x1 * cos - x2 * sin, x1 * sin + x2 * cos], axis=1) * k_scale
                rot_t = _pack(rot.T)
                for j in range(MATMUL_ROW_TILE // RET_BLOCK):
                    o_ref[rs.start // RET_BLOCK + j, cs.start // V7X_MXU_COLS] = rot_t[:, j * RET_BLOCK:(j + 1) * RET_BLOCK]
    elif kind == "plain":
        for rs, cs, z in z_tiles():
            store(rs, cs, z)
    elif kind == "silu_gain":
        gain_ref, _ = refs
        for rs, cs, z in z_tiles():
            store(rs, cs, z * _sigmoid(z) * gain_ref[:, cs])
    elif kind == "gelu":
        for rs, cs, z in z_tiles():
            store(rs, cs, _gelu_tanh(z))
    elif kind == "gelu_layernorm":
        lnw_ref, _, a_scr = refs
        for rs, _ in _tiles(rows, 1, LN_ROW_TILE, 1):
            for _, cs in _tiles(1, d, 1, V7X_MXU_COLS):
                a_scr[rs, cs] = _gelu_tanh(_dot(_unpack(n_ref[_packed(rs), :]), _unpack(w_ref[:, cs])))
            a = a_scr[rs, :]
            cen = a - jnp.mean(a, axis=-1, keepdims=True)
            var = jnp.mean(cen * cen, axis=-1, keepdims=True)
            o_ref[0, _packed(rs), :] = _pack(cen * lax.rsqrt(var + EPS) * lnw_ref[...])
    elif kind == "sigmoid_bias":
        gb_ref, _ = refs
        for rs, cs, z in z_tiles():
            store(rs, cs, _sigmoid(z + gb_ref[pl.ds(seg, 1), cs]))
    else:
        raise ValueError(kind)


def _in_proj(n, w_in, first_seg, n_seg, kind, extra=(), extra_specs=(), rows=IN_ROWS, k_scale=None):
    t, d = n.shape[0] * PACK, n.shape[1]
    assert t % rows == 0 and w_in.shape[1] % d == 0
    scratch = [pltpu.VMEM((rows, d), F32)] if kind == "gelu_layernorm" else []
    vmem = (4 * _nbytes((rows, d), BF16) + 2 * _nbytes((d, d), BF16) + 4 * _nbytes((rows, d // RET_HEADS // 2), F32)
            + (2 * _nbytes((rows, d), F32) if scratch else 0) + 16 * _nbytes((MATMUL_ROW_TILE, V7X_MXU_COLS), F32))
    if kind in ("rotary", "rotary_transposed"):
        assert d // RET_HEADS == V7X_MXU_COLS and MATMUL_ROW_TILE % RET_BLOCK == 0
    if kind == "rotary_transposed":
        assert n_seg == 1
        head = d // RET_HEADS
        out_spec = pl.BlockSpec((rows // RET_BLOCK, RET_HEADS, head // PACK, RET_BLOCK), lambda s, i: (i, 0, 0, 0))
        out_shape = jax.ShapeDtypeStruct((t // RET_BLOCK, RET_HEADS, head // PACK, RET_BLOCK), U32)
    else:
        out_spec = pl.BlockSpec((1, rows // PACK, d), lambda s, i: (s, i, 0))
        out_shape = jax.ShapeDtypeStruct((n_seg, t // PACK, d), U32)
    return pl.pallas_call(
        functools.partial(_in_proj_kernel, kind=kind, k_scale=k_scale),
        grid=(n_seg, t // rows),
        in_specs=[pl.BlockSpec((rows // PACK, d), lambda s, i: (i, 0)),
                  pl.BlockSpec((d // PACK, d), lambda s, i: (0, first_seg + s))] + list(extra_specs),
        out_specs=out_spec,
        out_shape=out_shape,
        scratch_shapes=scratch,
        compiler_params=_params(("parallel", "parallel"), vmem),
        name="in_proj_" + kind,
    )(n, w_in, *extra)


def _decay_rate(dec_ref, h, shape):
    return -jnp.exp(jnp.full(shape, dec_ref[h], F32))


def _ret_bwd_state_kernel(decb_ref, kt_ref, v_ref, sb_ref, state, kdec, cdec):
    c = pl.program_id(1)
    blk = kt_ref.shape[3]
    head = v_ref.shape[2] // RET_HEADS

    @pl.when((pl.program_id(0) == 0) & (c == 0))
    def _():
        col = lax.broadcasted_iota(jnp.int32, (1, blk), 1).astype(F32)
        for h in range(RET_HEADS):
            kdec[h] = jnp.exp(_decay_rate(decb_ref, h, (1, blk)) * col)
            cdec[h] = jnp.exp(_decay_rate(decb_ref, h, (1, head)) * float(blk))

    @pl.when(c == 0)
    def _():
        state[...] = jnp.zeros_like(state)

    for j in reversed(range(kt_ref.shape[0])):
        rows = slice(j * blk // PACK, (j + 1) * blk // PACK)
        for h in range(RET_HEADS):
            sl = slice(h * head, (h + 1) * head)
            st = state[h]
            sb_ref[0, j, h] = _pack(st)
            kd = (_unpack(kt_ref[j, h]).astype(F32) * kdec[h]).astype(BF16)
            state[h] = st * cdec[h] + _dot(kd, _unpack(v_ref[0, rows, sl]))


def _ret_bwd_states(kt, v, dec_b, batch):
    _, t_packed, d = v.shape
    n_blk = t_packed * PACK // batch // RET_BLOCK
    head = d // RET_HEADS
    per_step = RET_BWD_CHUNKS_PER_STEP
    n_steps = n_blk // per_step
    assert n_blk % per_step == 0
    blk_bytes = per_step * _nbytes((RET_BLOCK, d), BF16)
    state_bytes = _nbytes((RET_HEADS, head, head), F32)
    return pl.pallas_call(
        _ret_bwd_state_kernel,
        grid=(batch, n_steps),
        in_specs=[
            pl.BlockSpec(memory_space=pltpu.SMEM),
            pl.BlockSpec((per_step, RET_HEADS, head // PACK, RET_BLOCK),
                         lambda b, c: (b * n_steps + (n_steps - 1 - c), 0, 0, 0)),
            pl.BlockSpec((1, per_step * RET_BLOCK // PACK, d), lambda b, c: (0, b * n_steps + (n_steps - 1 - c), 0)),
        ],
        out_specs=pl.BlockSpec((1, per_step, RET_HEADS, head // PACK, head), lambda b, c: (b, n_steps - 1 - c, 0, 0, 0)),
        out_shape=jax.ShapeDtypeStruct((batch, n_blk, RET_HEADS, head // PACK, head), U32),
        scratch_shapes=[
            pltpu.VMEM((RET_HEADS, head, head), F32),
            pltpu.VMEM((RET_HEADS, 1, RET_BLOCK), F32),
            pltpu.VMEM((RET_HEADS, 1, head), F32),
        ],
        compiler_params=_params(("arbitrary", "arbitrary"), 6 * blk_bytes + 4 * state_bytes),
        name="ret_bwd_states",
    )(dec_b, kt, v)


def _project_slice(act_ref, w_ref, acc_ref, i):
    width = act_ref.shape[2]
    w_rows = slice(i * width // PACK, (i + 1) * width // PACK)
    for _, cs in _tiles(1, acc_ref.shape[1], 1, RESIDUAL_COL_TILE):
        part = _dot(act_ref[i], _unpack(w_ref[w_rows, cs]))
        acc_ref[:, cs] = part if i == 0 else acc_ref[:, cs] + part


def _ret_out_kernel(decf_ref, decb_ref, q_ref, kt_ref, v_ref, g_ref, sb_ref, wr_ref, o_ref,
                    state, dmat, qdf, qdb, kdf, cdf, act, acc):
    c = pl.program_id(1)
    blk = RET_BLOCK
    head = q_ref.shape[2] // RET_HEADS

    @pl.when((pl.program_id(0) == 0) & (c == 0))
    def _():
        row = lax.broadcasted_iota(jnp.int32, (blk, head), 0).astype(F32)
        col = lax.broadcasted_iota(jnp.int32, (1, blk), 1).astype(F32)
        ri = lax.broadcasted_iota(jnp.int32, (blk, blk), 0)
        ci = lax.broadcasted_iota(jnp.int32, (blk, blk), 1)
        dist = (ri - ci).astype(F32)
        for h in range(RET_HEADS):
            qdf[h] = jnp.exp(_decay_rate(decf_ref, h, (blk, head)) * (row + 1.0))
            qdb[h] = jnp.exp(_decay_rate(decb_ref, h, (blk, head)) * (float(blk) - row))
            kdf[h] = jnp.exp(_decay_rate(decf_ref, h, (1, blk)) * (float(blk) - 1.0 - col))
            cdf[h] = jnp.exp(_decay_rate(decf_ref, h, (1, head)) * float(blk))
            lf2 = _decay_rate(decf_ref, h, (blk, blk))
            lb2 = _decay_rate(decb_ref, h, (blk, blk))
            dmat[h] = jnp.where(ri >= ci, jnp.exp(lf2 * jnp.maximum(dist, 0.0)), jnp.exp(lb2 * jnp.maximum(-dist, 0.0)))

    @pl.when(c == 0)
    def _():
        state[...] = jnp.zeros_like(state)

    n_sub = kt_ref.shape[0]

    def project(item):
        j, h = divmod(item, RET_HEADS)
        _project_slice(act.at[j], wr_ref, acc.at[j], h)
        if h == RET_HEADS - 1:
            o_ref[j * blk // PACK:(j + 1) * blk // PACK, :] = _pack(acc[j])

    for item in range(n_sub * RET_HEADS):
        j, h = divmod(item, RET_HEADS)
        rows = slice(j * blk // PACK, (j + 1) * blk // PACK)
        sl = slice(h * head, (h + 1) * head)
        q = _unpack(q_ref[0, rows, sl])
        kt = _unpack(kt_ref[j, h])
        v = _unpack(v_ref[0, rows, sl])
        st = state[h]
        scores = _dot(q, kt)
        cross = _dot(q, st.astype(BF16)) * qdf[h] + _dot(q, _unpack(sb_ref[0, j, h])) * qdb[h]
        state[h] = st * cdf[h] + _dot((kt.astype(F32) * kdf[h]).astype(BF16), v)
        o = _dot((scores * dmat[h]).astype(BF16), v) + cross
        cen = o - jnp.mean(o, axis=-1, keepdims=True)
        var = jnp.mean(cen * cen, axis=-1, keepdims=True)
        act[j, h] = (_unpack(g_ref[0, rows, sl]).astype(F32) * (cen * lax.rsqrt(var + EPS))).astype(BF16)
        if item >= RET_PROJECT_LAG:
            project(item - RET_PROJECT_LAG)
    for item in range(n_sub * RET_HEADS - RET_PROJECT_LAG, n_sub * RET_HEADS):
        project(item)


def _ret_out(q, kt, v, g, sb, dec_f, dec_b, w_ret_out, batch):
    _, t_packed, d = q.shape
    n_blk = t_packed * PACK // batch // RET_BLOCK
    head = d // RET_HEADS
    per_step = RET_OUT_CHUNKS_PER_STEP
    n_steps = n_blk // per_step
    assert n_blk % per_step == 0
    rows = lambda: pl.BlockSpec((1, per_step * RET_BLOCK // PACK, d), lambda b, c: (0, b * n_steps + c, 0))
    blk_bytes = per_step * _nbytes((RET_BLOCK, d), BF16)
    table_bytes = _nbytes((RET_HEADS, RET_BLOCK, head), F32)
    return pl.pallas_call(
        _ret_out_kernel,
        grid=(batch, n_steps),
        in_specs=[
            pl.BlockSpec(memory_space=pltpu.SMEM),
            pl.BlockSpec(memory_space=pltpu.SMEM),
            rows(),
            pl.BlockSpec((per_step, RET_HEADS, head // PACK, RET_BLOCK), lambda b, c: (b * n_steps + c, 0, 0, 0)),
            rows(), rows(),
            pl.BlockSpec((1, per_step, RET_HEADS, head // PACK, head), lambda b, c: (b, c, 0, 0, 0)),
            _resident((d // PACK, d)),
        ],
        out_specs=pl.BlockSpec((per_step * RET_BLOCK // PACK, d), lambda b, c: (b * n_steps + c, 0)),
        out_shape=jax.ShapeDtypeStruct((t_packed, d), U32),
        scratch_shapes=[
            pltpu.VMEM((RET_HEADS, head, head), F32),
            pltpu.VMEM((RET_HEADS, RET_BLOCK, RET_BLOCK), F32),
            pltpu.VMEM((RET_HEADS, RET_BLOCK, head), F32),
            pltpu.VMEM((RET_HEADS, RET_BLOCK, head), F32),
            pltpu.VMEM((RET_HEADS, 1, RET_BLOCK), F32),
            pltpu.VMEM((RET_HEADS, 1, head), F32),
            pltpu.VMEM((per_step, RET_HEADS, RET_BLOCK, head), BF16),
            pltpu.VMEM((per_step, RET_BLOCK, d), F32),
        ],
        compiler_params=_params(("arbitrary", "arbitrary"),
                                12 * blk_bytes + 6 * table_bytes + _nbytes((d, d), BF16)
                                + (per_step + 3) * _nbytes((RET_BLOCK, d), F32)),
        name="ret_out",
    )(dec_f, dec_b, q, kt, v, g, sb, w_ret_out)


def _mix_out_kernel(u_ref, vs_ref, ws_ref, bs_ref, ret_ref, gr_ref, gs_ref, h_ref, wsg_ref, wo_ref, o_ref, act, acc):
    rows, d = h_ref.shape
    gdim = d // SGU_GROUPS
    for g in range(SGU_GROUPS):
        cs = slice(g * gdim, (g + 1) * gdim)
        for rs, _ in _tiles(rows, 1, SGU_CHUNK, 1):
            mixed = _dot(ws_ref[g], _unpack(vs_ref[0, _packed(rs), cs])) + bs_ref[g]
            act[g, rs, :] = (_unpack(u_ref[0, _packed(rs), cs]).astype(F32) * mixed).astype(BF16)
        if g >= SGU_PROJECT_LAG:
            _project_slice(act, wsg_ref, acc, g - SGU_PROJECT_LAG)
    for g in range(SGU_GROUPS - SGU_PROJECT_LAG, SGU_GROUPS):
        _project_slice(act, wsg_ref, acc, g)
    for k in range(SGU_GROUPS):
        ks = slice(k * gdim, (k + 1) * gdim)
        merged = (_unpack(gr_ref[0, :, ks]).astype(F32) * _unpack(ret_ref[:, ks]).astype(F32)
                  + _unpack(gs_ref[0, :, ks]).astype(F32) * acc[:, ks]).astype(BF16)
        w_rows = slice(k * gdim // PACK, (k + 1) * gdim // PACK)
        for _, cs in _tiles(1, d, 1, RESIDUAL_COL_TILE):
            o_ref[:, cs] = (h_ref[:, cs] if k == 0 else o_ref[:, cs]) + _dot(merged, _unpack(wo_ref[w_rows, cs]))


def _mix_out(u, vs, w_s, b_s, ret_proj, gates, h, w_sgu_out, w_out):
    t, d = h.shape
    gdim = d // SGU_GROUPS
    assert t % OUT_ROWS == 0 and OUT_ROWS % SGU_CHUNK == 0
    rows = lambda: pl.BlockSpec((OUT_ROWS, d), lambda i: (i, 0))
    packed_rows = lambda: pl.BlockSpec((OUT_ROWS // PACK, d), lambda i: (i, 0))
    seg_rows = lambda s: pl.BlockSpec((1, OUT_ROWS // PACK, d), lambda i: (s, i, 0))
    weight = lambda: _resident((d // PACK, d))
    vmem = (2 * _nbytes((d, d), BF16) + 10 * _nbytes((OUT_ROWS, d), BF16) + 4 * _nbytes((OUT_ROWS, d), F32)
            + _nbytes((OUT_ROWS, d), BF16) + 4 * _nbytes((OUT_ROWS, d), F32))
    return pl.pallas_call(
        _mix_out_kernel,
        grid=(t // OUT_ROWS,),
        in_specs=[
            seg_rows(0), seg_rows(0),
            pl.BlockSpec((SGU_GROUPS, SGU_CHUNK, SGU_CHUNK), lambda i: (0, 0, 0)),
            pl.BlockSpec((SGU_GROUPS, SGU_CHUNK, 1), lambda i: (0, 0, 0)),
            packed_rows(), seg_rows(0), seg_rows(1), rows(), weight(), weight(),
        ],
        out_specs=rows(),
        out_shape=jax.ShapeDtypeStruct((t, d), F32),
        scratch_shapes=[pltpu.VMEM((SGU_GROUPS, OUT_ROWS, gdim), BF16), pltpu.VMEM((OUT_ROWS, d), F32)],
        compiler_params=_params(("parallel",), vmem),
        name="mix_out",
    )(u, vs, w_s, b_s, ret_proj, gates, gates, h, w_sgu_out, w_out)


def _mem_kv_kernel(m_ref, nw_ref, w_ref, o_ref, n_scr):
    @pl.when(pl.program_id(1) == 0)
    def _():
        n_scr[...] = _rms(m_ref[...], nw_ref[...]).astype(BF16)

    o_ref[...] = _pack(_dot(n_scr[...], _unpack(w_ref[...])))


def _mem_kv(mem, norm_w, w_kv):
    t, d = mem.shape
    rows = min(KV_ROWS, t)
    n_out = w_kv.shape[1]
    assert t % rows == 0 and n_out % KV_COLS == 0
    vmem = (2 * _nbytes((rows, d), F32) + _nbytes((rows, d), BF16) + 2 * _nbytes((d, KV_COLS), BF16)
            + 2 * _nbytes((rows, KV_COLS), BF16) + 2 * _nbytes((rows, d), F32))
    return pl.pallas_call(
        _mem_kv_kernel,
        grid=(t // rows, n_out // KV_COLS),
        in_specs=[
            pl.BlockSpec((rows, d), lambda i, j: (i, 0)),
            pl.BlockSpec((1, d), lambda i, j: (0, 0)),
            pl.BlockSpec((d // PACK, KV_COLS), lambda i, j: (0, j)),
        ],
        out_specs=pl.BlockSpec((rows // PACK, KV_COLS), lambda i, j: (i, j)),
        out_shape=jax.ShapeDtypeStruct((t // PACK, n_out), U32),
        scratch_shapes=[pltpu.VMEM((rows, d), BF16)],
        compiler_params=_params(("parallel", "arbitrary"), vmem),
        name="mem_kv",
    )(mem, norm_w, w_kv)


def _xattn_kernel(h_ref, nw_ref, kv_ref, wq_ref, wo_ref, next_nw_ref, o_ref, n_out_ref, n_scr, q_scr, a_scr, acc):
    rows, d = h_ref.shape
    hd = d // XATTN_HEADS
    scale = float(hd) ** -0.5
    head_cols = lambda h: slice(h * hd, (h + 1) * hd)

    def project_q(h, rs=slice(None)):
        q_scr[rs, head_cols(h)] = _dot(n_scr[rs, :], _unpack(wq_ref[:, head_cols(h)])).astype(BF16)

    def probs(h):
        s = _dot_nt(q_scr[:, head_cols(h)], _unpack(kv_ref[:, head_cols(h)])) * scale
        e = jnp.exp(s - jnp.max(s, axis=-1, keepdims=True))
        return (e / jnp.sum(e, axis=-1, keepdims=True)).astype(BF16)

    def emit(rs, normed):
        n_out_ref[_packed(rs), :] = _pack(normed)

    for rs, _ in _tiles(rows, 1, rows // 2, 1):
        n_scr[rs, :] = _rms(h_ref[rs, :], nw_ref[...]).astype(BF16)
        project_q(0, rs)
    last = XATTN_HEADS - 1
    p = probs(0)
    for h in range(XATTN_HEADS):
        if h < last:
            project_q(h + 1)
        p_next = probs(h + 1) if h < last else None
        a_scr[h] = _dot(p, _unpack(kv_ref[:, d + h * hd:d + (h + 1) * hd])).astype(BF16)
        if h < last:
            _project_slice(a_scr, wo_ref, acc, h)
        else:
            _residual_proj_norm(h_ref, lambda rs: a_scr[last, rs, :], wo_ref, o_ref, None, next_nw_ref[...], emit,
                                w_rows=slice(last * hd // PACK, (last + 1) * hd // PACK), base_ref=acc)
        p = p_next


def _xattn(h, kv, norm_w, w_q, w_o, next_norm_w, batch):
    t, d = h.shape
    n_mem = kv.shape[0] * PACK // batch
    row_blocks = t // batch // XATTN_ROWS
    assert t % (batch * XATTN_ROWS) == 0
    rows = lambda: pl.BlockSpec((XATTN_ROWS, d), lambda b, i: (b * row_blocks + i, 0))
    vec = lambda: pl.BlockSpec((1, d), lambda b, i: (0, 0))
    weight = lambda: _resident((d // PACK, d))
    vmem = (2 * _nbytes((d, d), BF16) + 5 * _nbytes((XATTN_ROWS, d), F32) + 5 * _nbytes((XATTN_ROWS, d), BF16)
            + 2 * _nbytes((n_mem, 2 * d), BF16) + 2 * _nbytes((XATTN_ROWS, d), F32))
    return pl.pallas_call(
        _xattn_kernel,
        grid=(batch, row_blocks),
        in_specs=[rows(), vec(), pl.BlockSpec((n_mem // PACK, 2 * d), lambda b, i: (b, 0)), weight(), weight(), vec()],
        out_specs=(rows(), pl.BlockSpec((XATTN_ROWS // PACK, d), lambda b, i: (b * row_blocks + i, 0))),
        out_shape=(jax.ShapeDtypeStruct((t, d), F32), jax.ShapeDtypeStruct((t // PACK, d), U32)),
        scratch_shapes=[pltpu.VMEM((XATTN_ROWS, d), BF16), pltpu.VMEM((XATTN_ROWS, d), BF16),
                        pltpu.VMEM((XATTN_HEADS, XATTN_ROWS, d // XATTN_HEADS), BF16),
                        pltpu.VMEM((XATTN_ROWS, d), F32)],
        compiler_params=_params(("parallel", "parallel"), vmem),
        name="xattn",
    )(h, norm_w, kv, w_q, w_o, next_norm_w)


def _rope_tables(seq, half):
    freqs = ROPE_BASE ** (-jnp.linspace(0.0, 1.0, half, dtype=F32))
    ang = jnp.arange(seq, dtype=F32)[:, None] * freqs[None, :]
    return jnp.cos(ang), jnp.sin(ang)


def _trunk(x, mem, w):
    batch, seq, d = x.shape
    t = batch * seq
    x = x.reshape(t, d)
    half = d // RET_HEADS // 2
    cos, sin = _rope_tables(seq, half)
    pos_blocks = seq // IN_ROWS
    assert seq % IN_ROWS == 0
    pos_spec = lambda: pl.BlockSpec((IN_ROWS, half), lambda s, i: (i % pos_blocks, 0))

    hidden = _ffn_up(x, w["ffn1_norm"], w["ffn1_w_gu"], t)
    h, n = _ffn_down(x, hidden, w["ffn1_w_down"], w["mix_norm"], final_norm=False)

    w_in = w["w_in"]
    vec_spec = lambda: pl.BlockSpec((1, d), lambda s, i: (0, 0))
    q = _in_proj(n, w_in, SEG_Q, 1, "rotary", (cos, sin), (pos_spec(), pos_spec()))
    kt = _in_proj(n, w_in, SEG_K, 1, "rotary_transposed", (cos, sin), (pos_spec(), pos_spec()),
                  k_scale=float(2 * half) ** -0.5)
    v = _in_proj(n, w_in, SEG_V, 1, "plain")
    g = _in_proj(n, w_in, SEG_G, 1, "silu_gain", (w["ret_gn_w"],), (vec_spec(),))
    u = _in_proj(n, w_in, SEG_U, 1, "gelu")
    vs = _in_proj(n, w_in, SEG_VS, 1, "gelu_layernorm", (w["sgu_norm_w"],), (vec_spec(),), rows=IN_LN_ROWS)
    gates = _in_proj(n, w_in, SEG_GATE_R, 2, "sigmoid_bias", (w["gate_bias"],),
                     (pl.BlockSpec((2, d), lambda s, i: (0, 0)),))

    sb = _ret_bwd_states(kt, v, w["ret_decay_bwd"], batch)
    ret_proj = _ret_out(q, kt, v, g, sb, w["ret_decay_fwd"], w["ret_decay_bwd"], w["w_ret_out"], batch)
    h = _mix_out(u, vs, w["sgu_w_s"], w["sgu_b_s"], ret_proj, gates, h, w["w_sgu_out"], w["w_out"])

    kv = _mem_kv(mem.reshape(-1, d), w["xattn_norm_mem"], w["xattn_w_kv"])
    h, n = _xattn(h, kv, w["xattn_norm_q"], w["xattn_w_q"], w["xattn_w_o"], w["ffn2_norm"], batch)

    hidden = _ffn_up(n, None, w["ffn2_w_gu"], t)
    y = _ffn_down(h, hidden, w["ffn2_w_down"], w["final_norm"], final_norm=True)
    return y.reshape(batch, seq, d)


def _prepare(ffn1_norm, ffn1_w_gu, ffn1_w_down, mix_norm, w_in, gate_bias, ret_decay_fwd, ret_decay_bwd, ret_gn_w, w_ret_out, sgu_norm_w, sgu_w_s, sgu_b_s, w_sgu_out, w_out, xattn_norm_q, xattn_norm_mem, xattn_w_q, xattn_w_kv, xattn_w_o, ffn2_norm, ffn2_w_gu, ffn2_w_down, final_norm):
    assert ffn1_norm.shape[0] == 1, "single-layer trunk"
    pw = lambda w: _pack_weight(w[0])
    return {
        "ffn1_norm": ffn1_norm, "ffn1_w_gu": _pack_gate_up(ffn1_w_gu[0]), "ffn1_w_down": pw(ffn1_w_down),
        "mix_norm": mix_norm, "w_in": pw(w_in), "gate_bias": gate_bias[0],
        "ret_decay_fwd": ret_decay_fwd[0], "ret_decay_bwd": ret_decay_bwd[0], "ret_gn_w": ret_gn_w,
        "w_ret_out": pw(w_ret_out), "sgu_norm_w": sgu_norm_w,
        "sgu_w_s": sgu_w_s[0].astype(BF16), "sgu_b_s": sgu_b_s[0][:, :, None],
        "w_sgu_out": pw(w_sgu_out), "w_out": pw(w_out),
        "xattn_norm_q": xattn_norm_q, "xattn_norm_mem": xattn_norm_mem,
        "xattn_w_q": pw(xattn_w_q), "xattn_w_kv": pw(xattn_w_kv), "xattn_w_o": pw(xattn_w_o),
        "ffn2_norm": ffn2_norm, "ffn2_w_gu": _pack_gate_up(ffn2_w_gu[0]), "ffn2_w_down": pw(ffn2_w_down),
        "final_norm": final_norm[None, :],
    }


def kernel(x_prompt, x_sample, mem_prompt, mem_sample, ffn1_norm, ffn1_w_gu, ffn1_w_down, mix_norm, w_in, gate_bias, ret_decay_fwd, ret_decay_bwd, ret_gn_w, w_ret_out, sgu_norm_w, sgu_w_s, sgu_b_s, w_sgu_out, w_out, xattn_norm_q, xattn_norm_mem, xattn_w_q, xattn_w_kv, xattn_w_o, ffn2_norm, ffn2_w_gu, ffn2_w_down, final_norm):
    w = _prepare(ffn1_norm, ffn1_w_gu, ffn1_w_down, mix_norm, w_in, gate_bias, ret_decay_fwd, ret_decay_bwd, ret_gn_w, w_ret_out, sgu_norm_w, sgu_w_s, sgu_b_s, w_sgu_out, w_out, xattn_norm_q, xattn_norm_mem, xattn_w_q, xattn_w_kv, xattn_w_o, ffn2_norm, ffn2_w_gu, ffn2_w_down, final_norm)
    return _trunk(x_prompt, mem_prompt, w), _trunk(x_sample, mem_sample, w)
```
